```python
import jax, jax.numpy as jnp
from jax import lax
import numpy as np

D_MODEL = 2048
BATCH = 2
SEQ = 4096
DEPTH = 2

MEM_LEN = 256
ATTN_HEADS = 8
ATTN_HD = 128
ATTN_W = ATTN_HEADS * ATTN_HD
MOBA_BLOCK = 256
MOBA_TOPK = 3
Q_CHUNK = 64
SSD_HD = 64
SSD_HEADS = 48
D_SSD = SSD_HEADS * SSD_HD
SSD_GROUPS = 8
SSD_HPG = SSD_HEADS // SSD_GROUPS
SSD_STATE = 128
SSD_CONV = 4
SSD_CHUNK = 128
CONV_DIM = D_SSD + 2 * SSD_GROUPS * SSD_STATE
MIX_W = ATTN_W + D_SSD
N_IN = 3 * ATTN_W + D_SSD + CONV_DIM + SSD_HEADS
MEM_HEADS = 4
MEM_HD = 128
MEM_W = MEM_HEADS * MEM_HD
D_FF = 5632
EPS = 1e-6

kernel_name = "hymba_moba_ssd_macaron_alibi"


def rmsnorm(x, w):
    xf = x.astype(jnp.float32)
    y = xf * lax.rsqrt(jnp.mean(xf * xf, axis=-1, keepdims=True) + EPS)
    return (y * w.astype(jnp.float32)).astype(x.dtype)


def swiglu(h, w_gu, w_down):
    g, u = jnp.split(h @ w_gu, 2, axis=-1)
    return (jax.nn.silu(g) * u) @ w_down


def alibi_slopes(n):
    return jnp.exp2(-8.0 * jnp.arange(1, n + 1, dtype=jnp.float32) / n)


def moba_attention(q, k, v, slopes):
    B_, H, S, hd = q.shape
    nb = max(-(-S // MOBA_BLOCK), MOBA_TOPK)
    pad = nb * MOBA_BLOCK - S
    kp = jnp.pad(k, ((0, 0), (0, 0), (0, pad), (0, 0)))
    vp = jnp.pad(v, ((0, 0), (0, 0), (0, pad), (0, 0)))
    kb = kp.reshape(B_, H, nb, MOBA_BLOCK, hd)
    vb = vp.reshape(B_, H, nb, MOBA_BLOCK, hd)
    kmean = jnp.mean(kb.astype(jnp.float32), axis=3)
    n_chunks = S // Q_CHUNK
    qc = jnp.moveaxis(q.reshape(B_, H, n_chunks, Q_CHUNK, hd), 2, 0)
    scale = hd ** -0.5
    b_ix = jnp.arange(B_)[:, None, None, None]
    h_ix = jnp.arange(H)[None, :, None, None]
    blk_ar = jnp.arange(MOBA_BLOCK)

    def one_chunk(args):
        q_blk, c = args
        t = c * Q_CHUNK + jnp.arange(Q_CHUNK)
        cur = (c * Q_CHUNK) // MOBA_BLOCK
        gate = jnp.einsum('bhqd,bhnd->bhqn', q_blk.astype(jnp.float32), kmean)
        gate = jnp.where(jnp.arange(nb) < cur, gate, -jnp.inf)
        _, idx = lax.top_k(gate, MOBA_TOPK)
        slot_ok = jnp.arange(MOBA_TOPK) < cur
        kg = kb[b_ix, h_ix, idx]
        vg = vb[b_ix, h_ix, idx]
        s_pos = idx[..., None] * MOBA_BLOCK + blk_ar
        dist_sel = (t[:, None, None] - s_pos).astype(jnp.float32)
        logit_sel = (jnp.einsum('bhqd,bhqnkd->bhqnk', q_blk, kg).astype(jnp.float32) * scale
                     - slopes[:, None, None, None] * dist_sel)
        logit_sel = jnp.where(slot_ok[:, None], logit_sel, -jnp.inf)
        own_start = cur * MOBA_BLOCK
        k_own = lax.dynamic_slice_in_dim(kp, own_start, MOBA_BLOCK, axis=2)
        v_own = lax.dynamic_slice_in_dim(vp, own_start, MOBA_BLOCK, axis=2)
        dist_own = t[:, None] - (own_start + blk_ar)[None, :]
        logit_own = (jnp.einsum('bhqd,bhkd->bhqk', q_blk, k_own).astype(jnp.float32) * scale
                     - slopes[:, None, None] * dist_own.astype(jnp.float32))
        logit_own = jnp.where(dist_own >= 0, logit_own, -jnp.inf)
        n_sel = MOBA_TOPK * MOBA_BLOCK
        logits = jnp.concatenate(
            [logit_sel.reshape(B_, H, Q_CHUNK, n_sel), logit_own], axis=-1)
        p = jax.nn.softmax(logits, axis=-1).astype(v.dtype)
        p_sel = p[..., :n_sel].reshape(B_, H, Q_CHUNK, MOBA_TOPK, MOBA_BLOCK)
        p_own = p[..., n_sel:]
        return (jnp.einsum('bhqnk,bhqnkd->bhqd', p_sel, vg)
                + jnp.einsum('bhqk,bhkd->bhqd', p_own, v_own))

    out = lax.map(one_chunk, (qc, jnp.arange(n_chunks)))
    return jnp.moveaxis(out, 0, 2).reshape(B_, H, S, hd)


def causal_dwconv(u, w, b):
    out = lax.conv_general_dilated(
        u, w[:, None, :], window_strides=(1,), padding=[(w.shape[0] - 1, 0)],
        dimension_numbers=('NWC', 'WIO', 'NWC'), feature_group_count=u.shape[-1])
    return out + b


def ssd_scan(xs, dt, a, bm, cm):
    B_, S, G, J, P = xs.shape
    N = bm.shape[-1]
    nc, L = S // SSD_CHUNK, SSD_CHUNK
    x = (xs.astype(jnp.float32) * dt[..., None]).reshape(B_, nc, L, G, J, P)
    da = (dt * a).reshape(B_, nc, L, G, J)
    acs = jnp.cumsum(da, axis=2)
    bc = bm.astype(jnp.float32).reshape(B_, nc, L, G, N)
    cc = cm.astype(jnp.float32).reshape(B_, nc, L, G, N)
    acs_t = jnp.transpose(acs, (0, 1, 3, 4, 2))
    seg = acs_t[..., :, None] - acs_t[..., None, :]
    causal = jnp.tril(jnp.ones((L, L), dtype=bool))
    decay_in = jnp.exp(jnp.where(causal, seg, -jnp.inf))
    cb = jnp.einsum('bclgn,bcsgn->bcgls', cc, bc)
    scores = cb[:, :, :, None] * decay_in
    y_diag = jnp.einsum('bcgjls,bcsgjp->bclgjp', scores, x)
    decay_st = jnp.exp(acs[:, :, -1:] - acs)
    states = jnp.einsum('bclgn,bclgjp->cbgjpn', bc, x * decay_st[..., None])
    chunk_decay = jnp.transpose(jnp.exp(acs[:, :, -1]), (1, 0, 2, 3))

    def step(h, inp):
        st, dec = inp
        return h * dec[..., None, None] + st, h

    h0 = jnp.zeros((B_, G, J, P, N), jnp.float32)
    _, h_prev = lax.scan(step, h0, (states, chunk_decay))
    y_off = jnp.einsum('bclgn,cbgjpn->bclgjp', cc, h_prev) * jnp.exp(acs)[..., None]
    return (y_diag + y_off).reshape(B_, S, G, J, P)


def ssd_mixer(z, xbc, dt_raw, conv_w, conv_b, dt_bias, a_log, d_skip, norm_w):
    B_, S, _ = xbc.shape
    xbc = jax.nn.silu(causal_dwconv(xbc, conv_w, conv_b))
    GN = SSD_GROUPS * SSD_STATE
    xs, bm, cm = jnp.split(xbc, [D_SSD, D_SSD + GN], axis=-1)
    xs = xs.reshape(B_, S, SSD_GROUPS, SSD_HPG, SSD_HD)
    bm = bm.reshape(B_, S, SSD_GROUPS, SSD_STATE)
    cm = cm.reshape(B_, S, SSD_GROUPS, SSD_STATE)
    dt = jax.nn.softplus(dt_raw.astype(jnp.float32) + dt_bias.astype(jnp.float32))
    dt = dt.reshape(B_, S, SSD_GROUPS, SSD_HPG)
    a = -jnp.exp(a_log.astype(jnp.float32)).reshape(SSD_GROUPS, SSD_HPG)
    y = ssd_scan(xs, dt, a, bm, cm)
    y = y + d_skip.astype(jnp.float32).reshape(SSD_GROUPS, SSD_HPG, 1) * xs.astype(jnp.float32)
    g = y.reshape(B_, S, D_SSD) * jax.nn.silu(z.astype(jnp.float32))
    gg = g.reshape(B_, S, SSD_GROUPS, D_SSD // SSD_GROUPS)
    gg = gg * lax.rsqrt(jnp.mean(gg * gg, axis=-1, keepdims=True) + EPS)
    return (gg.reshape(B_, S, D_SSD) * norm_w.astype(jnp.float32)).astype(z.dtype)


def hybrid_mixer(h, w_in, q_norm, k_norm, conv_w, conv_b, dt_bias, a_log, d_skip,
                 ssd_norm, w_out, slopes):
    B_, S, _ = h.shape
    proj = h @ w_in
    o = [ATTN_W, 2 * ATTN_W, 3 * ATTN_W, 3 * ATTN_W + D_SSD, 3 * ATTN_W + D_SSD + CONV_DIM]
    q, k, v, z, xbc, dt_raw = jnp.split(proj, o, axis=-1)
    heads = lambda t: jnp.transpose(t.reshape(B_, S, ATTN_HEADS, ATTN_HD), (0, 2, 1, 3))
    q = heads(rmsnorm(q.reshape(B_, S, ATTN_HEADS, ATTN_HD), q_norm).reshape(B_, S, ATTN_W))
    k = heads(rmsnorm(k.reshape(B_, S, ATTN_HEADS, ATTN_HD), k_norm).reshape(B_, S, ATTN_W))
    v = heads(v)
    y_attn = moba_attention(q, k, v, slopes)
    y_attn = jnp.transpose(y_attn, (0, 2, 1, 3)).reshape(B_, S, ATTN_W).astype(h.dtype)
    y_ssd = ssd_mixer(z, xbc, dt_raw, conv_w, conv_b, dt_bias, a_log, d_skip, ssd_norm)
    return jnp.concatenate([y_attn, y_ssd], axis=-1) @ w_out


def memory_cross_attention(h, m, wq, wk, wv, qn, kn, wo):
    B_, S, _ = h.shape
    M = m.shape[1]
    q = rmsnorm((h @ wq).reshape(B_, S, MEM_HEADS, MEM_HD), qn)
    k = rmsnorm((m @ wk).reshape(B_, M, MEM_HEADS, MEM_HD), kn)
    v = (m @ wv).reshape(B_, M, MEM_HEADS, MEM_HD)
    s = jnp.einsum('bqhd,bkhd->bhqk', q, k).astype(jnp.float32) * MEM_HD ** -0.5
    p = jax.nn.softmax(s, axis=-1).astype(v.dtype)
    o = jnp.einsum('bhqk,bkhd->bqhd', p, v).reshape(B_, S, MEM_W)
    return o @ wo


def setup_inputs(seed: int = 0) -> dict:
    key = jax.random.key(seed)
    ks = jax.random.split(key, 32)
    L = DEPTH
    nrm = lambda k, shape, fan: jax.random.normal(k, shape, jnp.float32) * fan ** -0.5
    gain = lambda k, shape: 1.0 + 0.02 * jax.random.normal(k, shape, jnp.float32)
    dt0 = jnp.exp(jax.random.uniform(ks[12], (L, SSD_HEADS), jnp.float32)
                  * (np.log(0.1) - np.log(0.001)) + np.log(0.001))
    dt_bias = dt0 + jnp.log(-jnp.expm1(-dt0))
    a_log = jnp.log(jax.random.uniform(ks[13], (L, SSD_HEADS), jnp.float32, 1.0, 16.0))
    return {
        "x": jax.random.normal(ks[0], (BATCH, SEQ, D_MODEL), jnp.float32),
        "mem": jax.random.normal(ks[1], (BATCH, MEM_LEN, D_MODEL), jnp.float32),
        "ff1_norm": gain(ks[2], (L, D_MODEL)),
        "ff1_w_gu": nrm(ks[3], (L, D_MODEL, 2 * D_FF), D_MODEL),
        "ff1_w_down": nrm(ks[4], (L, D_FF, D_MODEL), D_FF),
        "mix_norm": gain(ks[5], (L, D_MODEL)),
        "w_in": nrm(ks[6], (L, D_MODEL, N_IN), D_MODEL),
        "q_norm": gain(ks[7], (L, ATTN_HD)),
        "k_norm": gain(ks[8], (L, ATTN_HD)),
        "conv_w": nrm(ks[9], (L, SSD_CONV, CONV_DIM), SSD_CONV),
        "conv_b": 0.02 * jax.random.normal(ks[10], (L, CONV_DIM), jnp.float32),
        "dt_bias": dt_bias,
        "a_log": a_log,
        "d_skip": gain(ks[14], (L, SSD_HEADS)),
        "ssd_norm": gain(ks[15], (L, D_SSD)),
        "w_out": nrm(ks[16], (L, MIX_W, D_MODEL), MIX_W),
        "xmem_norm": gain(ks[17], (L, D_MODEL)),
        "mem_norm": gain(ks[18], (L, D_MODEL)),
        "mem_wq": nrm(ks[19], (L, D_MODEL, MEM_W), D_MODEL),
        "mem_wk": nrm(ks[20], (L, D_MODEL, MEM_W), D_MODEL),
        "mem_wv": nrm(ks[21], (L, D_MODEL, MEM_W), D_MODEL),
        "mem_q_norm": gain(ks[22], (L, MEM_HD)),
        "mem_k_norm": gain(ks[23], (L, MEM_HD)),
        "mem_wo": nrm(ks[24], (L, MEM_W, D_MODEL), MEM_W),
        "ff2_norm": gain(ks[25], (L, D_MODEL)),
        "ff2_w_gu": nrm(ks[26], (L, D_MODEL, 2 * D_FF), D_MODEL),
        "ff2_w_down": nrm(ks[27], (L, D_FF, D_MODEL), D_FF),
    }


def reference(x, mem, ff1_norm, ff1_w_gu, ff1_w_down, mix_norm, w_in, q_norm, k_norm,
              conv_w, conv_b, dt_bias, a_log, d_skip, ssd_norm, w_out, xmem_norm, mem_norm,
              mem_wq, mem_wk, mem_wv, mem_q_norm, mem_k_norm, mem_wo, ff2_norm, ff2_w_gu,
              ff2_w_down):
    slopes = alibi_slopes(ATTN_HEADS)
    for l in range(DEPTH):
        x = x + 0.5 * swiglu(rmsnorm(x, ff1_norm[l]), ff1_w_gu[l], ff1_w_down[l])
        x = x + hybrid_mixer(rmsnorm(x, mix_norm[l]), w_in[l], q_norm[l], k_norm[l],
                             conv_w[l], conv_b[l], dt_bias[l], a_log[l], d_skip[l],
                             ssd_norm[l], w_out[l], slopes)
        x = x + memory_cross_attention(rmsnorm(x, xmem_norm[l]), rmsnorm(mem, mem_norm[l]),
                                       mem_wq[l], mem_wk[l], mem_wv[l], mem_q_norm[l],
                                       mem_k_norm[l], mem_wo[l])
        x = x + 0.5 * swiglu(rmsnorm(x, ff2_norm[l]), ff2_w_gu[l], ff2_w_down[l])
    return x
```

```python
import functools

import jax
import jax.numpy as jnp
from jax import lax
from jax.experimental import pallas as pl
from jax.experimental.pallas import tpu as pltpu

F32 = jnp.float32
BF16 = jnp.bfloat16
EPS = 1e-6
NEG = -1e30

LANES = 128
SUBLANES = 8
VMEM_LIMIT = 56 * 1024 * 1024

ATTN_HEADS = 8
ATTN_HD = 128
MOBA_BLOCK = 256
MOBA_TOPK = 3
SSD_HD = 64
SSD_GROUPS = 8
SSD_HPG = 6
SSD_STATE = 128
SSD_CONV = 4
SSD_CHUNK = 128
MEM_HEADS = 4
MEM_HD = 128


def _cparams(sem):
    return pltpu.CompilerParams(dimension_semantics=sem, vmem_limit_bytes=VMEM_LIMIT)


def _rms(x, w):
    return x * lax.rsqrt(jnp.mean(x * x, axis=-1, keepdims=True) + EPS) * w


def _dot(a, b):
    return jnp.dot(a, b, preferred_element_type=F32)


def _dot_nt(a, b, precision=None):
    return lax.dot_general(a, b, (((1,), (1,)), ((), ())), preferred_element_type=F32,
                           precision=precision)


def _ffn_body(x_ref, nw_ref, wg_ref, wu_ref, wd_ref, o_ref, h_ref):
    j = pl.program_id(1)

    @pl.when(j == 0)
    def _():
        x = x_ref[...]
        h_ref[...] = _rms(x, nw_ref[...]).astype(BF16)
        o_ref[...] = x

    h = h_ref[...]
    g = _dot(h, wg_ref[...])
    u = _dot(h, wu_ref[...])
    a = (0.5 * (g * jax.nn.sigmoid(g)) * u).astype(BF16)
    o_ref[...] += _dot(a, wd_ref[...])


def _ffn(x, nw, w_gu, w_down, *, tm=512, tf=512):
    T, D = x.shape
    F = w_down.shape[0]
    nf = F // tf
    return pl.pallas_call(
        _ffn_body,
        grid=(T // tm, nf),
        in_specs=[
            pl.BlockSpec((tm, D), lambda i, j: (i, 0)),
            pl.BlockSpec((1, D), lambda i, j: (0, 0)),
            pl.BlockSpec((D, tf), lambda i, j: (0, j)),
            pl.BlockSpec((D, tf), lambda i, j: (0, j + nf)),
            pl.BlockSpec((tf, D), lambda i, j: (j, 0)),
        ],
        out_specs=pl.BlockSpec((tm, D), lambda i, j: (i, 0)),
        out_shape=jax.ShapeDtypeStruct((T, D), F32),
        scratch_shapes=[pltpu.VMEM((tm, D), BF16)],
        compiler_params=_cparams(("parallel", "arbitrary")),
        name="ffn",
    )(x, nw, w_gu, w_gu, w_down)


def _inproj_body(x_ref, nw_ref, w_ref, wdt_ref, p_ref, dt_ref, h_ref):
    j = pl.program_id(1)

    @pl.when(j == 0)
    def _():
        h = _rms(x_ref[...], nw_ref[...]).astype(BF16)
        h_ref[...] = h
        dt_ref[...] = _dot(h, wdt_ref[...])

    p_ref[...] = _dot(h_ref[...], w_ref[...])


def _in_proj(x, nw, w_main, w_dt, *, tm=1024, tn=1024):
    T, D = x.shape
    N = w_main.shape[1]
    return pl.pallas_call(
        _inproj_body,
        grid=(T // tm, N // tn),
        in_specs=[
            pl.BlockSpec((tm, D), lambda i, j: (i, 0)),
            pl.BlockSpec((1, D), lambda i, j: (0, 0)),
            pl.BlockSpec((D, tn), lambda i, j: (0, j)),
            pl.BlockSpec((D, LANES), lambda i, j: (0, 0)),
        ],
        out_specs=[
            pl.BlockSpec((tm, tn), lambda i, j: (i, j)),
            pl.BlockSpec((tm, LANES), lambda i, j: (i, 0)),
        ],
        out_shape=[jax.ShapeDtypeStruct((T, N), F32), jax.ShapeDtypeStruct((T, LANES), F32)],
        scratch_shapes=[pltpu.VMEM((tm, D), BF16)],
        compiler_params=_cparams(("parallel", "arbitrary")),
        name="in_proj",
    )(x, nw, w_main, w_dt)


def _moba_body(q_ref, k_ref, v_ref, qn_ref, kn_ref, slope_ref, o_ref,
               knb_ref, vt_ref, kmean_ref, selb_ref, m_ref, l_ref, acc_ref, *, nb):
    qi = pl.program_id(2)
    blk = MOBA_BLOCK
    scale = ATTN_HD ** -0.5
    slope = slope_ref[0:1, 0:1]

    @pl.when(qi == 0)
    def _():
        kf = _rms(k_ref[...], kn_ref[...])
        knb_ref[...] = kf.astype(BF16)
        for n in range(nb):
            kmean_ref[n:n + 1, :] = jnp.mean(kf[n * blk:(n + 1) * blk], axis=0, keepdims=True)
        vt_ref[...] = v_ref[...].T.astype(BF16)

    qn = _rms(q_ref[...], qn_ref[...])
    qt = qn.T.astype(BF16)

    gate = _dot_nt(kmean_ref[...], qn, precision=lax.Precision.HIGHEST)
    n_iota = lax.broadcasted_iota(jnp.int32, gate.shape, 0).astype(F32)
    past_blk = n_iota < qi.astype(F32)
    gate = jnp.where(past_blk, gate, -jnp.inf)
    selb = jnp.full(gate.shape, NEG, F32)
    for _ in range(MOBA_TOPK):
        best = jnp.max(gate, axis=0, keepdims=True)
        first = jnp.min(jnp.where(gate == best, n_iota, float(nb)), axis=0, keepdims=True)
        pick = n_iota == first
        selb = jnp.where(pick, 0.0, selb)
        gate = jnp.where(pick, -jnp.inf, gate)
    selb_ref[...] = jnp.where(past_blk, selb, NEG)

    key_i = lax.broadcasted_iota(jnp.int32, (blk, blk), 0)
    qry_i = lax.broadcasted_iota(jnp.int32, (blk, blk), 1)
    rel = (qry_i - key_i).astype(F32)

    own = pl.multiple_of(qi * blk, blk)
    s = _dot(knb_ref[pl.ds(own, blk), :], qt) * scale - slope * rel
    s = jnp.where(rel >= 0, s, NEG)
    m0 = jnp.max(s, axis=0, keepdims=True)
    p = jnp.exp(s - m0)
    m_ref[...] = m0
    l_ref[...] = jnp.sum(p, axis=0, keepdims=True)
    acc_ref[...] = _dot(vt_ref[:, pl.ds(own, blk)], p.astype(BF16))

    def past(n, carry):
        off = pl.multiple_of(n * blk, blk)
        dist0 = ((qi - n) * blk).astype(F32)
        rowb = selb_ref[pl.ds(n, 1), :] - slope * dist0
        s = _dot(knb_ref[pl.ds(off, blk), :], qt) * scale - slope * rel + rowb
        m_old = m_ref[...]
        m_new = jnp.maximum(m_old, jnp.max(s, axis=0, keepdims=True))
        alpha = jnp.exp(m_old - m_new)
        p = jnp.exp(s - m_new)
        m_ref[...] = m_new
        l_ref[...] = alpha * l_ref[...] + jnp.sum(p, axis=0, keepdims=True)
        acc_ref[...] = alpha * acc_ref[...] + _dot(vt_ref[:, pl.ds(off, blk)], p.astype(BF16))
        return carry

    lax.fori_loop(0, qi, past, 0)
    o_ref[...] = (acc_ref[...] / l_ref[...]).T.astype(o_ref.dtype)


def _moba(p_main, q_norm, k_norm, slopes, *, batch, seq):
    T = p_main.shape[0]
    blk = MOBA_BLOCK
    nb = seq // blk
    H = ATTN_HEADS
    body = functools.partial(_moba_body, nb=nb)
    return pl.pallas_call(
        body,
        grid=(batch, H, nb),
        in_specs=[
            pl.BlockSpec((blk, ATTN_HD), lambda b, h, i: (b * nb + i, h)),
            pl.BlockSpec((seq, ATTN_HD), lambda b, h, i: (b, H + h)),
            pl.BlockSpec((seq, ATTN_HD), lambda b, h, i: (b, 2 * H + h)),
            pl.BlockSpec((1, ATTN_HD), lambda b, h, i: (0, 0)),
            pl.BlockSpec((1, ATTN_HD), lambda b, h, i: (0, 0)),
            pl.BlockSpec((None, 1, LANES), lambda b, h, i: (h, 0, 0)),
        ],
        out_specs=pl.BlockSpec((blk, ATTN_HD), lambda b, h, i: (b * nb + i, h)),
        out_shape=jax.ShapeDtypeStruct((T, H * ATTN_HD), BF16),
        scratch_shapes=[
            pltpu.VMEM((seq, ATTN_HD), BF16),
            pltpu.VMEM((ATTN_HD, seq), BF16),
            pltpu.VMEM((nb, ATTN_HD), F32),
            pltpu.VMEM((nb, blk), F32),
            pltpu.VMEM((1, blk), F32),
            pltpu.VMEM((1, blk), F32),
            pltpu.VMEM((ATTN_HD, blk), F32),
        ],
        compiler_params=_cparams(("parallel", "parallel", "arbitrary")),
        name="moba",
    )(p_main, p_main, p_main, q_norm, k_norm, slopes)


def _softplus(x):
    return jnp.maximum(x, 0.0) + jnp.log1p(jnp.exp(-jnp.abs(x)))


def _split_hi_lo(x):
    hi = x.astype(BF16)
    lo = (x - hi.astype(F32)).astype(BF16)
    return jnp.concatenate([hi, lo], axis=1)


def _ssd_body(xs_ref, bm_ref, cm_ref, xs_h_ref, bm_h_ref, cm_h_ref, z_ref, dt_ref,
              wx_ref, wb_ref, wc_ref, bx_ref, bb_ref, bc_ref,
              dtb_ref, alog_ref, dsk_ref, nw_ref, o_ref,
              ux_ref, ub_ref, uc_ref, acst_ref, st_ref, *, rows):
    g = pl.program_id(1)
    r = pl.program_id(2)
    L = SSD_CHUNK
    P = SSD_HD
    W = SSD_HPG * P
    HALO = SUBLANES

    @pl.when(r == 0)
    def _():
        st_ref[...] = jnp.zeros_like(st_ref)

    first = r == 0

    def conv(u_ref, halo_ref, buf_ref, w_ref, b_ref):
        halo = halo_ref[...]
        buf_ref[0:HALO, :] = jnp.where(first, jnp.zeros_like(halo), halo)
        buf_ref[HALO:, :] = u_ref[...]
        acc = b_ref[...] + w_ref[SSD_CONV - 1:SSD_CONV, :] * u_ref[...]
        for k in range(SSD_CONV - 1):
            shift = SSD_CONV - 1 - k
            acc = acc + w_ref[k:k + 1, :] * buf_ref[HALO - shift:HALO - shift + rows, :]
        buf_ref[HALO:, :] = acc * jax.nn.sigmoid(acc)

    conv(xs_ref, xs_h_ref, ux_ref, wx_ref, bx_ref)
    conv(bm_ref, bm_h_ref, ub_ref, wb_ref, bb_ref)
    conv(cm_ref, cm_h_ref, uc_ref, wc_ref, bc_ref)

    k_i = lax.broadcasted_iota(jnp.int32, (2 * LANES, W), 0)
    c_i = lax.broadcasted_iota(jnp.int32, (2 * LANES, W), 1)
    head = g * SSD_HPG + lax.shift_right_logical(c_i, P.bit_length() - 1)
    expand = ((k_i == head) | (k_i == head + LANES)).astype(BF16)

    li = lax.broadcasted_iota(jnp.int32, (L, L), 0)
    si = lax.broadcasted_iota(jnp.int32, (L, L), 1)
    causal = li >= si
    tril = causal.astype(F32)
    lane = lax.broadcasted_iota(jnp.int32, (L, LANES), 1)
    a_row = -jnp.exp(alog_ref[...])

    def chunk(c, carry):
        r0 = pl.multiple_of(c * L, L)
        rs = pl.ds(HALO + r0, L)
        dtv = _softplus(dt_ref[pl.ds(r0, L), :] + dtb_ref[...])
        acs_c = jnp.dot(tril, dtv * a_row, preferred_element_type=F32,
                        precision=lax.Precision.HIGHEST)
        acst_ref[...] = acs_c.T
        dt_x = _dot(_split_hi_lo(dtv), expand)
        acs_x = _dot(_split_hi_lo(acs_c), expand)
        acs_last = acs_x[L - 1:L, :]

        xs = ux_ref[rs, :]
        bmat = ub_ref[rs, :]
        cmat = uc_ref[rs, :]
        x = xs * dt_x
        xb = x.astype(BF16)
        cb16 = cmat.astype(BF16)
        cb = _dot_nt(cb16, bmat.astype(BF16))

        tiles = []
        for m in range(W // LANES):
            res = []
            for jj in range(2):
                j = 2 * m + jj
                col = acs_x[:, j * P:j * P + 1]
                row = acst_ref[pl.ds(g * SSD_HPG + j, 1), :]
                dec = jnp.exp(jnp.where(causal, col - row, NEG))
                sc = (cb * dec).astype(BF16)
                res.append(_dot(sc, xb[:, m * LANES:(m + 1) * LANES]))
            tiles.append(jnp.where(lane < P, res[0], res[1]))
        y = jnp.concatenate(tiles, axis=1)

        st = st_ref[...]
        y = y + _dot(cb16, st.astype(BF16)) * jnp.exp(acs_x)
        xd = (x * jnp.exp(acs_last - acs_x)).astype(BF16)
        st_ref[...] = st * jnp.exp(acs_last) + _dot(bmat.T.astype(BF16), xd)

        y = y + dsk_ref[...] * xs
        z = z_ref[pl.ds(r0, L), :]
        gt = y * (z * jax.nn.sigmoid(z))
        gn = gt * lax.rsqrt(jnp.mean(gt * gt, axis=-1, keepdims=True) + EPS)
        o_ref[pl.ds(r0, L), :] = (gn * nw_ref[...]).astype(o_ref.dtype)
        return carry

    lax.fori_loop(0, rows // L, chunk, 0)


def _ssd(p_main, dt_raw, conv_w, conv_b, dt_bias_p, a_log_p, d_skip_x, ssd_norm, *, batch, seq, rows=512):
    T = p_main.shape[0]
    G = SSD_GROUPS
    N = SSD_STATE
    W = SSD_HPG * SSD_HD
    nr = seq // rows
    hb = rows // SUBLANES
    z0, x0 = 3 * ATTN_HEADS * ATTN_HD // W, (3 * ATTN_HEADS * ATTN_HD + G * W) // W
    b0 = (3 * ATTN_HEADS * ATTN_HD + 2 * G * W) // N
    c0 = b0 + G
    cwb0, cwc0 = G * W // N, G * W // N + G

    def rowblk(b, g, r):
        return b * nr + r

    def halo(b, g, r):
        return jnp.maximum((b * nr + r) * hb - 1, 0)

    body = functools.partial(_ssd_body, rows=rows)
    return pl.pallas_call(
        body,
        grid=(batch, G, nr),
        in_specs=[
            pl.BlockSpec((rows, W), lambda b, g, r: (rowblk(b, g, r), x0 + g)),
            pl.BlockSpec((rows, N), lambda b, g, r: (rowblk(b, g, r), b0 + g)),
            pl.BlockSpec((rows, N), lambda b, g, r: (rowblk(b, g, r), c0 + g)),
            pl.BlockSpec((SUBLANES, W), lambda b, g, r: (halo(b, g, r), x0 + g)),
            pl.BlockSpec((SUBLANES, N), lambda b, g, r: (halo(b, g, r), b0 + g)),
            pl.BlockSpec((SUBLANES, N), lambda b, g, r: (halo(b, g, r), c0 + g)),
            pl.BlockSpec((rows, W), lambda b, g, r: (rowblk(b, g, r), z0 + g)),
            pl.BlockSpec((rows, LANES), lambda b, g, r: (rowblk(b, g, r), 0)),
            pl.BlockSpec((SSD_CONV, W), lambda b, g, r: (0, g)),
            pl.BlockSpec((SSD_CONV, N), lambda b, g, r: (0, cwb0 + g)),
            pl.BlockSpec((SSD_CONV, N), lambda b, g, r: (0, cwc0 + g)),
            pl.BlockSpec((1, W), lambda b, g, r: (0, g)),
            pl.BlockSpec((1, N), lambda b, g, r: (0, cwb0 + g)),
            pl.BlockSpec((1, N), lambda b, g, r: (0, cwc0 + g)),
            pl.BlockSpec((1, LANES), lambda b, g, r: (0, 0)),
            pl.BlockSpec((1, LANES), lambda b, g, r: (0, 0)),
            pl.BlockSpec((1, W), lambda b, g, r: (0, g)),
            pl.BlockSpec((1, W), lambda b, g, r: (0, g)),
        ],
        out_specs=pl.BlockSpec((rows, W), lambda b, g, r: (rowblk(b, g, r), g)),
        out_shape=jax.ShapeDtypeStruct((T, G * W), BF16),
        scratch_shapes=[
            pltpu.VMEM((SUBLANES + rows, W), F32),
            pltpu.VMEM((SUBLANES + rows, N), F32),
            pltpu.VMEM((SUBLANES + rows, N), F32),
            pltpu.VMEM((LANES, SSD_CHUNK), F32),
            pltpu.VMEM((N, W), F32),
        ],
        compiler_params=_cparams(("parallel", "parallel", "arbitrary")),
        name="ssd",
    )(p_main, p_main, p_main, p_main, p_main, p_main, p_main, dt_raw,
      conv_w, conv_w, conv_w, conv_b, conv_b, conv_b,
      dt_bias_p, a_log_p, d_skip_x, ssd_norm)


def _outproj_body(x_ref, ya_ref, ys_ref, wa_ref, ws_ref, o_ref):
    o_ref[...] = x_ref[...] + _dot(ya_ref[...], wa_ref[...]) + _dot(ys_ref[...], ws_ref[...])


def _out_proj(x, ya, ys, w_a, w_s, *, tm=512, tn=1024):
    T, D = x.shape
    Ka, Ks = ya.shape[1], ys.shape[1]
    return pl.pallas_call(
        _outproj_body,
        grid=(T // tm, D // tn),
        in_specs=[
            pl.BlockSpec((tm, tn), lambda i, j: (i, j)),
            pl.BlockSpec((tm, Ka), lambda i, j: (i, 0)),
            pl.BlockSpec((tm, Ks), lambda i, j: (i, 0)),
            pl.BlockSpec((Ka, tn), lambda i, j: (0, j)),
            pl.BlockSpec((Ks, tn), lambda i, j: (0, j)),
        ],
        out_specs=pl.BlockSpec((tm, tn), lambda i, j: (i, j)),
        out_shape=jax.ShapeDtypeStruct((T, D), F32),
        compiler_params=_cparams(("parallel", "arbitrary")),
        name="out_proj",
    )(x, ya, ys, w_a, w_s)


def _memkv_body(m_ref, nw_ref, wk_ref, wv_ref, kn_ref, k_ref, v_ref):
    h = _rms(m_ref[...], nw_ref[...]).astype(BF16)
    k = _dot(h, wk_ref[...])
    v_ref[...] = _dot(h, wv_ref[...]).astype(BF16)
    for hh in range(MEM_HEADS):
        sl = slice(hh * MEM_HD, (hh + 1) * MEM_HD)
        k_ref[:, sl] = _rms(k[:, sl], kn_ref[...]).astype(BF16)


def _mem_kv(mem, nw, wk, wv, kn):
    M, D = mem.shape
    Wm = wk.shape[1]
    full = lambda shape: pl.BlockSpec(shape, lambda i: (0,) * len(shape))
    return pl.pallas_call(
        _memkv_body,
        grid=(1,),
        in_specs=[full((M, D)), full((1, D)), full((D, Wm)), full((D, Wm)), full((1, MEM_HD))],
        out_specs=[full((M, Wm)), full((M, Wm))],
        out_shape=[jax.ShapeDtypeStruct((M, Wm), BF16)] * 2,
        compiler_params=_cparams(("arbitrary",)),
        name="mem_kv",
    )(mem, nw, wk, wv, kn)


def _memattn_body(x_ref, nw_ref, wq_ref, qn_ref, k_ref, v_ref, wo_ref, o_ref):
    x = x_ref[...]
    h = _rms(x, nw_ref[...]).astype(BF16)
    q = _dot(h, wq_ref[...])
    scale = MEM_HD ** -0.5
    outs = []
    for hh in range(MEM_HEADS):
        sl = slice(hh * MEM_HD, (hh + 1) * MEM_HD)
        qh = _rms(q[:, sl], qn_ref[...]).astype(BF16)
        s = _dot_nt(qh, k_ref[:, sl]) * scale
        e = jnp.exp(s - jnp.max(s, axis=-1, keepdims=True))
        p = e / jnp.sum(e, axis=-1, keepdims=True)
        outs.append(_dot(p.astype(BF16), v_ref[:, sl]))
    o = jnp.concatenate(outs, axis=1).astype(BF16)
    o_ref[...] = x + _dot(o, wo_ref[...])


def _mem_attn(x, nw, wq, qn, k, v, wo, *, seq, mem_len, tm=512):
    T, D = x.shape
    Wm = wq.shape[1]
    per_b = seq // tm
    return pl.pallas_call(
        _memattn_body,
        grid=(T // tm,),
        in_specs=[
            pl.BlockSpec((tm, D), lambda i: (i, 0)),
            pl.BlockSpec((1, D), lambda i: (0, 0)),
            pl.BlockSpec((D, Wm), lambda i: (0, 0)),
            pl.BlockSpec((1, MEM_HD), lambda i: (0, 0)),
            pl.BlockSpec((mem_len, Wm), lambda i: (i // per_b, 0)),
            pl.BlockSpec((mem_len, Wm), lambda i: (i // per_b, 0)),
            pl.BlockSpec((Wm, D), lambda i: (0, 0)),
        ],
        out_specs=pl.BlockSpec((tm, D), lambda i: (i, 0)),
        out_shape=jax.ShapeDtypeStruct((T, D), F32),
        compiler_params=_cparams(("parallel",)),
        name="mem_attn",
    )(x, nw, wq, qn, k, v, wo)


def _pad_lanes(v):
    return jnp.pad(v.astype(F32), (0, LANES - v.shape[0])).reshape(1, LANES)


def kernel(x, mem, ff1_norm, ff1_w_gu, ff1_w_down, mix_norm, w_in, q_norm, k_norm, conv_w, conv_b,
           dt_bias, a_log, d_skip, ssd_norm, w_out, xmem_norm, mem_norm, mem_wq, mem_wk, mem_wv,
           mem_q_norm, mem_k_norm, mem_wo, ff2_norm, ff2_w_gu, ff2_w_down):
    B, S, D = x.shape
    M = mem.shape[1]
    depth = w_in.shape[0]
    attn_w = ATTN_HEADS * ATTN_HD
    n_main = w_in.shape[2] - SSD_GROUPS * SSD_HPG
    row = lambda v: v.reshape(1, -1)

    slopes = jnp.exp2(-8.0 * jnp.arange(1, ATTN_HEADS + 1, dtype=F32) / ATTN_HEADS)
    slopes = jnp.broadcast_to(slopes[:, None, None], (ATTN_HEADS, 1, LANES))
    xf = x.reshape(B * S, D)
    memf = mem.reshape(B * M, D)
    for l in range(depth):
        xf = _ffn(xf, row(ff1_norm[l]), ff1_w_gu[l].astype(BF16), ff1_w_down[l].astype(BF16))

        w_l = w_in[l]
        w_dt = jnp.pad(w_l[:, n_main:], ((0, 0), (0, LANES - (w_l.shape[1] - n_main)))).astype(BF16)
        p_main, dt_raw = _in_proj(xf, row(mix_norm[l]), w_l[:, :n_main].astype(BF16), w_dt)
        y_attn = _moba(p_main, row(q_norm[l]), row(k_norm[l]), slopes, batch=B, seq=S)
        y_ssd = _ssd(p_main, dt_raw, conv_w[l], row(conv_b[l]), _pad_lanes(dt_bias[l]),
                     _pad_lanes(a_log[l]), row(jnp.repeat(d_skip[l], SSD_HD)), row(ssd_norm[l]),
                     batch=B, seq=S)
        w_o = w_out[l].astype(BF16)
        xf = _out_proj(xf, y_attn, y_ssd, w_o[:attn_w], w_o[attn_w:])

        k_mem, v_mem = _mem_kv(memf, row(mem_norm[l]), mem_wk[l].astype(BF16), mem_wv[l].astype(BF16),
                               row(mem_k_norm[l]))
        xf = _mem_attn(xf, row(xmem_norm[l]), mem_wq[l].astype(BF16), row(mem_q_norm[l]), k_mem, v_mem,
                       mem_wo[l].astype(BF16), seq=S, mem_len=M)

        xf = _ffn(xf, row(ff2_norm[l]), ff2_w_gu[l].astype(BF16), ff2_w_down[l].astype(BF16))
    return xf.reshape(B, S, D)
```

```python
import functools

import jax
import jax.numpy as jnp
from jax import lax
from jax.experimental import pallas as pl
from jax.experimental.pallas import tpu as pltpu

F32 = jnp.float32
BF16 = jnp.bfloat16
EPS = 1e-6
NEG = -1e30

LANES = 128
SUBLANES = 8
VMEM_LIMIT = 56 * 1024 * 1024

ATTN_HEADS = 8
ATTN_HD = 128
MOBA_BLOCK = 256
MOBA_TOPK = 3
SSD_HD = 64
SSD_GROUPS = 8
SSD_HPG = 6
SSD_STATE = 128
SSD_CONV = 4
SSD_CHUNK = 128
MEM_HEADS = 4
MEM_HD = 128


def _cparams(sem):
    return pltpu.CompilerParams(dimension_semantics=sem, vmem_limit_bytes=VMEM_LIMIT)


def _rms(x, w):
    return x * lax.rsqrt(jnp.mean(x * x, axis=-1, keepdims=True) + EPS) * w


def _dot(a, b):
    return jnp.dot(a, b, preferred_element_type=F32)


def _dot_nt(a, b, precision=None):
    return lax.dot_general(a, b, (((1,), (1,)), ((), ())), preferred_element_type=F32,
                           precision=precision)


def _ffn_body(x_ref, nw_ref, wg_ref, wu_ref, wd_ref, o_ref, h_ref):
    j = pl.program_id(1)

    @pl.when(j == 0)
    def _():
        x = x_ref[...]
        h_ref[...] = _rms(x, nw_ref[...]).astype(BF16)
        o_ref[...] = x

    h = h_ref[...]
    g = _dot(h, wg_ref[...])
    u = _dot(h, wu_ref[...])
    a = (0.5 * (g * jax.nn.sigmoid(g)) * u).astype(BF16)
    o_ref[...] += _dot(a, wd_ref[...])


def _ffn(x, nw, w_gu, w_down, *, tm=512, tf=512):
    T, D = x.shape
    F = w_down.shape[0]
    nf = F // tf
    return pl.pallas_call(
        _ffn_body,
        grid=(T // tm, nf),
        in_specs=[
            pl.BlockSpec((tm, D), lambda i, j: (i, 0)),
            pl.BlockSpec((1, D), lambda i, j: (0, 0)),
            pl.BlockSpec((D, tf), lambda i, j: (0, j)),
            pl.BlockSpec((D, tf), lambda i, j: (0, j + nf)),
            pl.BlockSpec((tf, D), lambda i, j: (j, 0)),
        ],
        out_specs=pl.BlockSpec((tm, D), lambda i, j: (i, 0)),
        out_shape=jax.ShapeDtypeStruct((T, D), F32),
        scratch_shapes=[pltpu.VMEM((tm, D), BF16)],
        compiler_params=_cparams(("parallel", "arbitrary")),
        name="ffn",
    )(x, nw, w_gu, w_gu, w_down)


def _inproj_body(x_ref, nw_ref, w_ref, wdt_ref, p_ref, dt_ref, h_ref):
    j = pl.program_id(1)

    @pl.when(j == 0)
    def _():
        h = _rms(x_ref[...], nw_ref[...]).astype(BF16)
        h_ref[...] = h
        dt_ref[...] = _dot(h, wdt_ref[...])

    p_ref[...] = _dot(h_ref[...], w_ref[...])


def _in_proj(x, nw, w_main, w_dt, *, tm=1024, tn=1024):
    T, D = x.shape
    N = w_main.shape[1]
    return pl.pallas_call(
        _inproj_body,
        grid=(T // tm, N // tn),
        in_specs=[
            pl.BlockSpec((tm, D), lambda i, j: (i, 0)),
            pl.BlockSpec((1, D), lambda i, j: (0, 0)),
            pl.BlockSpec((D, tn), lambda i, j: (0, j)),
            pl.BlockSpec((D, LANES), lambda i, j: (0, 0)),
        ],
        out_specs=[
            pl.BlockSpec((tm, tn), lambda i, j: (i, j)),
            pl.BlockSpec((tm, LANES), lambda i, j: (i, 0)),
        ],
        out_shape=[jax.ShapeDtypeStruct((T, N), F32), jax.ShapeDtypeStruct((T, LANES), F32)],
        scratch_shapes=[pltpu.VMEM((tm, D), BF16)],
        compiler_params=_cparams(("parallel", "arbitrary")),
        name="in_proj",
    )(x, nw, w_main, w_dt)


MOBA_EXT = LANES
MOBA_SEL0 = 16
MOBA_VARIANT = 4


def _split3(x):
    hi = x.astype(BF16).astype(F32)
    r = x - hi
    mid = r.astype(BF16).astype(F32)
    lo = (r - mid).astype(BF16).astype(F32)
    return hi, mid, lo


def _rows16(rows, width):
    r_i = lax.broadcasted_iota(jnp.int32, (2 * SUBLANES, width), 0)
    out = jnp.zeros((2 * SUBLANES, width), F32)
    for k, row in enumerate(rows):
        out = jnp.where(r_i == k, row, out)
    return out


def _moba_body(q_ref, k_ref, v_ref, qn_ref, kn_ref, slope_ref, o_ref,
               kx_ref, vt_ref, kmean_ref, s_ref, *, nb):
    qi = pl.program_id(2)
    blk = MOBA_BLOCK
    seq = nb * blk
    scale = ATTN_HD ** -0.5
    slope = slope_ref[0:1, 0:1]

    @pl.when(qi == 0)
    def _():
        kf = _rms(k_ref[...], kn_ref[...])
        kx_ref[:, 0:ATTN_HD] = kf.astype(BF16)
        for n in range(nb):
            kmean_ref[n:n + 1, :] = jnp.mean(kf[n * blk:(n + 1) * blk], axis=0, keepdims=True)
        vt_ref[...] = v_ref[...].T.astype(BF16)
        pos = lax.broadcasted_iota(jnp.int32, (seq, MOBA_EXT), 0)
        c = lax.broadcasted_iota(jnp.int32, (seq, MOBA_EXT), 1)
        n_of = lax.shift_right_logical(pos, blk.bit_length() - 1)
        j_of = pos & (blk - 1)
        ext = jnp.where(c < 3, 1, jnp.where(c < 6, n_of, jnp.where(c < 9, j_of, 0)))
        ext = jnp.where(c - MOBA_SEL0 == n_of, 1, ext)
        kx_ref[:, ATTN_HD:] = ext.astype(F32).astype(BF16)

    qn = _rms(q_ref[...], qn_ref[...])

    gate = _dot_nt(kmean_ref[...], qn, precision=lax.Precision.HIGHEST)
    n_iota = lax.broadcasted_iota(jnp.int32, gate.shape, 0).astype(F32)
    qi_f = qi.astype(F32)
    past_blk = n_iota < qi_f
    gate = jnp.where(past_blk, gate, -jnp.inf)
    selb = jnp.full(gate.shape, NEG, F32)
    for _ in range(MOBA_TOPK):
        best = jnp.max(gate, axis=0, keepdims=True)
        first = jnp.min(jnp.where(gate == best, n_iota, float(nb)), axis=0, keepdims=True)
        pick = n_iota == first
        selb = jnp.where(pick, 0.0, selb)
        gate = jnp.where(pick, -jnp.inf, gate)
    selb = jnp.where(past_blk, selb, jnp.where(n_iota == qi_f, 0.0, NEG))

    t_row = (qi * blk + lax.broadcasted_iota(jnp.int32, (1, blk), 1)).astype(F32)
    ones = jnp.ones((1, blk), F32)
    bias_rows = (list(_split3(-slope * t_row)) + [v * ones for v in _split3(slope * float(blk))]
                 + [v * ones for v in _split3(slope)])
    qx = jnp.concatenate([
        (qn * scale).T.astype(BF16),
        _rows16(bias_rows, blk).astype(BF16),
        selb.astype(BF16),
        jnp.zeros((MOBA_EXT - MOBA_SEL0 - nb, blk), BF16),
    ], axis=0)

    key_i = lax.broadcasted_iota(jnp.int32, (blk, blk), 0)
    qry_i = lax.broadcasted_iota(jnp.int32, (blk, blk), 1)
    own = pl.multiple_of(qi * blk, blk)

    def attend(nk):
        n = nk * blk
        s_ref[0:n, :] = _dot(kx_ref[0:n, :], qx)
        s_own = s_ref[pl.ds(own, blk), :]
        s_ref[pl.ds(own, blk), :] = jnp.where(qry_i >= key_i, s_own, NEG)
        s = s_ref[0:n, :]
        p = jnp.exp(s - jnp.max(s, axis=0, keepdims=True))
        l = jnp.sum(p, axis=0, keepdims=True)
        acc = _dot(vt_ref[:, 0:n], p.astype(BF16))
        o_ref[...] = (acc / l).T.astype(o_ref.dtype)

    for nk in range(MOBA_VARIANT, nb + 1, MOBA_VARIANT):
        pl.when((qi >= nk - MOBA_VARIANT) & (qi < nk))(functools.partial(attend, nk))


def _moba(p_main, q_norm, k_norm, slopes, *, batch, seq):
    T = p_main.shape[0]
    blk = MOBA_BLOCK
    nb = seq // blk
    assert seq % blk == 0 and nb % MOBA_VARIANT == 0 and nb <= MOBA_EXT - MOBA_SEL0
    H = ATTN_HEADS
    body = functools.partial(_moba_body, nb=nb)
    return pl.pallas_call(
        body,
        grid=(batch, H, nb),
        in_specs=[
            pl.BlockSpec((blk, ATTN_HD), lambda b, h, i: (b * nb + i, h)),
            pl.BlockSpec((seq, ATTN_HD), lambda b, h, i: (b, H + h)),
            pl.BlockSpec((seq, ATTN_HD), lambda b, h, i: (b, 2 * H + h)),
            pl.BlockSpec((1, ATTN_HD), lambda b, h, i: (0, 0)),
            pl.BlockSpec((1, ATTN_HD), lambda b, h, i: (0, 0)),
            pl.BlockSpec((None, 1, LANES), lambda b, h, i: (h, 0, 0)),
        ],
        out_specs=pl.BlockSpec((blk, ATTN_HD), lambda b, h, i: (b * nb + i, h)),
        out_shape=jax.ShapeDtypeStruct((T, H * ATTN_HD), BF16),
        scratch_shapes=[
            pltpu.VMEM((seq, ATTN_HD + MOBA_EXT), BF16),
            pltpu.VMEM((ATTN_HD, seq), BF16),
            pltpu.VMEM((nb, ATTN_HD), F32),
            pltpu.VMEM((seq, blk), F32),
        ],
        compiler_params=_cparams(("parallel", "parallel", "arbitrary")),
        name="moba",
    )(p_main, p_main, p_main, q_norm, k_norm, slopes)


def _softplus(x):
    return jnp.maximum(x, 0.0) + jnp.log1p(jnp.exp(-jnp.abs(x)))


def _split_hi_lo(x):
    hi = x.astype(BF16)
    lo = (x - hi.astype(F32)).astype(BF16)
    return jnp.concatenate([hi, lo], axis=1)


def _ssd_body(xs_ref, bm_ref, cm_ref, xs_h_ref, bm_h_ref, cm_h_ref, z_ref, dt_ref,
              wx_ref, wb_ref, wc_ref, bx_ref, bb_ref, bc_ref,
              dtb_ref, alog_ref, dsk_ref, nw_ref, o_ref,
              ux_ref, ub_ref, uc_ref, acst_ref, st_ref, *, rows):
    g = pl.program_id(1)
    r = pl.program_id(2)
    L = SSD_CHUNK
    P = SSD_HD
    W = SSD_HPG * P
    HALO = SUBLANES

    @pl.when(r == 0)
    def _():
        st_ref[...] = jnp.zeros_like(st_ref)

    first = r == 0

    def conv(u_ref, halo_ref, buf_ref, w_ref, b_ref):
        halo = halo_ref[...]
        buf_ref[0:HALO, :] = jnp.where(first, jnp.zeros_like(halo), halo)
        buf_ref[HALO:, :] = u_ref[...]
        acc = b_ref[...] + w_ref[SSD_CONV - 1:SSD_CONV, :] * u_ref[...]
        for k in range(SSD_CONV - 1):
            shift = SSD_CONV - 1 - k
            acc = acc + w_ref[k:k + 1, :] * buf_ref[HALO - shift:HALO - shift + rows, :]
        buf_ref[HALO:, :] = acc * jax.nn.sigmoid(acc)

    conv(xs_ref, xs_h_ref, ux_ref, wx_ref, bx_ref)
    conv(bm_ref, bm_h_ref, ub_ref, wb_ref, bb_ref)
    conv(cm_ref, cm_h_ref, uc_ref, wc_ref, bc_ref)

    k_i = lax.broadcasted_iota(jnp.int32, (2 * LANES, W), 0)
    c_i = lax.broadcasted_iota(jnp.int32, (2 * LANES, W), 1)
    head = g * SSD_HPG + lax.shift_right_logical(c_i, P.bit_length() - 1)
    expand = ((k_i == head) | (k_i == head + LANES)).astype(BF16)

    li = lax.broadcasted_iota(jnp.int32, (L, L), 0)
    si = lax.broadcasted_iota(jnp.int32, (L, L), 1)
    causal = li >= si
    tril = causal.astype(F32)
    lane = lax.broadcasted_iota(jnp.int32, (L, LANES), 1)
    a_row = -jnp.exp(alog_ref[...])

    def chunk(c, carry):
        r0 = pl.multiple_of(c * L, L)
        rs = pl.ds(HALO + r0, L)
        dtv = _softplus(dt_ref[pl.ds(r0, L), :] + dtb_ref[...])
        acs_c = jnp.dot(tril, dtv * a_row, preferred_element_type=F32,
                        precision=lax.Precision.HIGHEST)
        acst_ref[...] = acs_c.T
        dt_x = _dot(_split_hi_lo(dtv), expand)
        acs_x = _dot(_split_hi_lo(acs_c), expand)
        acs_last = acs_x[L - 1:L, :]

        xs = ux_ref[rs, :]
        bmat = ub_ref[rs, :]
        cmat = uc_ref[rs, :]
        x = xs * dt_x
        xb = x.astype(BF16)
        cb16 = cmat.astype(BF16)
        cb = _dot_nt(cb16, bmat.astype(BF16))

        tiles = []
        for m in range(W // LANES):
            res = []
            for jj in range(2):
                j = 2 * m + jj
                col = acs_x[:, j * P:j * P + 1]
                row = acst_ref[pl.ds(g * SSD_HPG + j, 1), :]
                dec = jnp.exp(jnp.where(causal, col - row, NEG))
                sc = (cb * dec).astype(BF16)
                res.append(_dot(sc, xb[:, m * LANES:(m + 1) * LANES]))
            tiles.append(jnp.where(lane < P, res[0], res[1]))
        y = jnp.concatenate(tiles, axis=1)

        st = st_ref[...]
        y = y + _dot(cb16, st.astype(BF16)) * jnp.exp(acs_x)
        xd = (x * jnp.exp(acs_last - acs_x)).astype(BF16)
        st_ref[...] = st * jnp.exp(acs_last) + _dot(bmat.T.astype(BF16), xd)

        y = y + dsk_ref[...] * xs
        z = z_ref[pl.ds(r0, L), :]
        gt = y * (z * jax.nn.sigmoid(z))
        gn = gt * lax.rsqrt(jnp.mean(gt * gt, axis=-1, keepdims=True) + EPS)
        o_ref[pl.ds(r0, L), :] = (gn * nw_ref[...]).astype(o_ref.dtype)
        return carry

    lax.fori_loop(0, rows // L, chunk, 0)


def _ssd(p_main, dt_raw, conv_w, conv_b, dt_bias_p, a_log_p, d_skip_x, ssd_norm, *, batch, seq, rows=512):
    T = p_main.shape[0]
    G = SSD_GROUPS
    N = SSD_STATE
    W = SSD_HPG * SSD_HD
    nr = seq // rows
    hb = rows // SUBLANES
    z0, x0 = 3 * ATTN_HEADS * ATTN_HD // W, (3 * ATTN_HEADS * ATTN_HD + G * W) // W
    b0 = (3 * ATTN_HEADS * ATTN_HD + 2 * G * W) // N
    c0 = b0 + G
    cwb0, cwc0 = G * W // N, G * W // N + G

    def rowblk(b, g, r):
        return b * nr + r

    def halo(b, g, r):
        return jnp.maximum((b * nr + r) * hb - 1, 0)

    body = functools.partial(_ssd_body, rows=rows)
    return pl.pallas_call(
        body,
        grid=(batch, G, nr),
        in_specs=[
            pl.BlockSpec((rows, W), lambda b, g, r: (rowblk(b, g, r), x0 + g)),
            pl.BlockSpec((rows, N), lambda b, g, r: (rowblk(b, g, r), b0 + g)),
            pl.BlockSpec((rows, N), lambda b, g, r: (rowblk(b, g, r), c0 + g)),
            pl.BlockSpec((SUBLANES, W), lambda b, g, r: (halo(b, g, r), x0 + g)),
            pl.BlockSpec((SUBLANES, N), lambda b, g, r: (halo(b, g, r), b0 + g)),
            pl.BlockSpec((SUBLANES, N), lambda b, g, r: (halo(b, g, r), c0 + g)),
            pl.BlockSpec((rows, W), lambda b, g, r: (rowblk(b, g, r), z0 + g)),
            pl.BlockSpec((rows, LANES), lambda b, g, r: (rowblk(b, g, r), 0)),
            pl.BlockSpec((SSD_CONV, W), lambda b, g, r: (0, g)),
            pl.BlockSpec((SSD_CONV, N), lambda b, g, r: (0, cwb0 + g)),
            pl.BlockSpec((SSD_CONV, N), lambda b, g, r: (0, cwc0 + g)),
            pl.BlockSpec((1, W), lambda b, g, r: (0, g)),
            pl.BlockSpec((1, N), lambda b, g, r: (0, cwb0 + g)),
            pl.BlockSpec((1, N), lambda b, g, r: (0, cwc0 + g)),
            pl.BlockSpec((1, LANES), lambda b, g, r: (0, 0)),
            pl.BlockSpec((1, LANES), lambda b, g, r: (0, 0)),
            pl.BlockSpec((1, W), lambda b, g, r: (0, g)),
            pl.BlockSpec((1, W), lambda b, g, r: (0, g)),
        ],
        out_specs=pl.BlockSpec((rows, W), lambda b, g, r: (rowblk(b, g, r), g)),
        out_shape=jax.ShapeDtypeStruct((T, G * W), BF16),
        scratch_shapes=[
            pltpu.VMEM((SUBLANES + rows, W), F32),
            pltpu.VMEM((SUBLANES + rows, N), F32),
            pltpu.VMEM((SUBLANES + rows, N), F32),
            pltpu.VMEM((LANES, SSD_CHUNK), F32),
            pltpu.VMEM((N, W), F32),
        ],
        compiler_params=_cparams(("parallel", "parallel", "arbitrary")),
        name="ssd",
    )(p_main, p_main, p_main, p_main, p_main, p_main, p_main, dt_raw,
      conv_w, conv_w, conv_w, conv_b, conv_b, conv_b,
      dt_bias_p, a_log_p, d_skip_x, ssd_norm)


def _outproj_body(x_ref, ya_ref, ys_ref, wa_ref, ws_ref, o_ref):
    o_ref[...] = x_ref[...] + _dot(ya_ref[...], wa_ref[...]) + _dot(ys_ref[...], ws_ref[...])


def _out_proj(x, ya, ys, w_a, w_s, *, tm=512, tn=1024):
    T, D = x.shape
    Ka, Ks = ya.shape[1], ys.shape[1]
    return pl.pallas_call(
        _outproj_body,
        grid=(T // tm, D // tn),
        in_specs=[
            pl.BlockSpec((tm, tn), lambda i, j: (i, j)),
            pl.BlockSpec((tm, Ka), lambda i, j: (i, 0)),
            pl.BlockSpec((tm, Ks), lambda i, j: (i, 0)),
            pl.BlockSpec((Ka, tn), lambda i, j: (0, j)),
            pl.BlockSpec((Ks, tn), lambda i, j: (0, j)),
        ],
        out_specs=pl.BlockSpec((tm, tn), lambda i, j: (i, j)),
        out_shape=jax.ShapeDtypeStruct((T, D), F32),
        compiler_params=_cparams(("parallel", "arbitrary")),
        name="out_proj",
    )(x, ya, ys, w_a, w_s)


def _memkv_body(m_ref, nw_ref, wk_ref, wv_ref, kn_ref, k_ref, v_ref):
    h = _rms(m_ref[...], nw_ref[...]).astype(BF16)
    k = _dot(h, wk_ref[...])
    v_ref[...] = _dot(h, wv_ref[...]).astype(BF16)
    for hh in range(MEM_HEADS):
        sl = slice(hh * MEM_HD, (hh + 1) * MEM_HD)
        k_ref[:, sl] = _rms(k[:, sl], kn_ref[...]).astype(BF16)


def _mem_kv(mem, nw, wk, wv, kn):
    M, D = mem.shape
    Wm = wk.shape[1]
    full = lambda shape: pl.BlockSpec(shape, lambda i: (0,) * len(shape))
    return pl.pallas_call(
        _memkv_body,
        grid=(1,),
        in_specs=[full((M, D)), full((1, D)), full((D, Wm)), full((D, Wm)), full((1, MEM_HD))],
        out_specs=[full((M, Wm)), full((M, Wm))],
        out_shape=[jax.ShapeDtypeStruct((M, Wm), BF16)] * 2,
        compiler_params=_cparams(("arbitrary",)),
        name="mem_kv",
    )(mem, nw, wk, wv, kn)


def _memattn_body(x_ref, nw_ref, wq_ref, qn_ref, k_ref, v_ref, wo_ref, o_ref):
    x = x_ref[...]
    h = _rms(x, nw_ref[...]).astype(BF16)
    q = _dot(h, wq_ref[...])
    scale = MEM_HD ** -0.5
    outs = []
    for hh in range(MEM_HEADS):
        sl = slice(hh * MEM_HD, (hh + 1) * MEM_HD)
        qh = _rms(q[:, sl], qn_ref[...]).astype(BF16)
        s = _dot_nt(qh, k_ref[:, sl]) * scale
        e = jnp.exp(s - jnp.max(s, axis=-1, keepdims=True))
        p = e / jnp.sum(e, axis=-1, keepdims=True)
        outs.append(_dot(p.astype(BF16), v_ref[:, sl]))
    o = jnp.concatenate(outs, axis=1).astype(BF16)
    o_ref[...] = x + _dot(o, wo_ref[...])


def _mem_attn(x, nw, wq, qn, k, v, wo, *, seq, mem_len, tm=512):
    T, D = x.shape
    Wm = wq.shape[1]
    per_b = seq // tm
    return pl.pallas_call(
        _memattn_body,
        grid=(T // tm,),
        in_specs=[
            pl.BlockSpec((tm, D), lambda i: (i, 0)),
            pl.BlockSpec((1, D), lambda i: (0, 0)),
            pl.BlockSpec((D, Wm), lambda i: (0, 0)),
            pl.BlockSpec((1, MEM_HD), lambda i: (0, 0)),
            pl.BlockSpec((mem_len, Wm), lambda i: (i // per_b, 0)),
            pl.BlockSpec((mem_len, Wm), lambda i: (i // per_b, 0)),
            pl.BlockSpec((Wm, D), lambda i: (0, 0)),
        ],
        out_specs=pl.BlockSpec((tm, D), lambda i: (i, 0)),
        out_shape=jax.ShapeDtypeStruct((T, D), F32),
        compiler_params=_cparams(("parallel",)),
        name="mem_attn",
    )(x, nw, wq, qn, k, v, wo)


def _pad_lanes(v):
    return jnp.pad(v.astype(F32), (0, LANES - v.shape[0])).reshape(1, LANES)


def kernel(x, mem, ff1_norm, ff1_w_gu, ff1_w_down, mix_norm, w_in, q_norm, k_norm, conv_w, conv_b,
           dt_bias, a_log, d_skip, ssd_norm, w_out, xmem_norm, mem_norm, mem_wq, mem_wk, mem_wv,
           mem_q_norm, mem_k_norm, mem_wo, ff2_norm, ff2_w_gu, ff2_w_down):
    B, S, D = x.shape
    M = mem.shape[1]
    depth = w_in.shape[0]
    attn_w = ATTN_HEADS * ATTN_HD
    n_main = w_in.shape[2] - SSD_GROUPS * SSD_HPG
    row = lambda v: v.reshape(1, -1)

    slopes = jnp.exp2(-8.0 * jnp.arange(1, ATTN_HEADS + 1, dtype=F32) / ATTN_HEADS)
    slopes = jnp.broadcast_to(slopes[:, None, None], (ATTN_HEADS, 1, LANES))
    xf = x.reshape(B * S, D)
    memf = mem.reshape(B * M, D)
    for l in range(depth):
        xf = _ffn(xf, row(ff1_norm[l]), ff1_w_gu[l].astype(BF16), ff1_w_down[l].astype(BF16))

        w_l = w_in[l]
        w_dt = jnp.pad(w_l[:, n_main:], ((0, 0), (0, LANES - (w_l.shape[1] - n_main)))).astype(BF16)
        p_main, dt_raw = _in_proj(xf, row(mix_norm[l]), w_l[:, :n_main].astype(BF16), w_dt)
        y_attn = _moba(p_main, row(q_norm[l]), row(k_norm[l]), slopes, batch=B, seq=S)
        y_ssd = _ssd(p_main, dt_raw, conv_w[l], row(conv_b[l]), _pad_lanes(dt_bias[l]),
                     _pad_lanes(a_log[l]), row(jnp.repeat(d_skip[l], SSD_HD)), row(ssd_norm[l]),
                     batch=B, seq=S)
        w_o = w_out[l].astype(BF16)
        xf = _out_proj(xf, y_attn, y_ssd, w_o[:attn_w], w_o[attn_w:])

        k_mem, v_mem = _mem_kv(memf, row(mem_norm[l]), mem_wk[l].astype(BF16), mem_wv[l].astype(BF16),
                               row(mem_k_norm[l]))
        xf = _mem_attn(xf, row(xmem_norm[l]), mem_wq[l].astype(BF16), row(mem_q_norm[l]), k_mem, v_mem,
                       mem_wo[l].astype(BF16), seq=S, mem_len=M)

        xf = _ffn(xf, row(ff2_norm[l]), ff2_w_gu[l].astype(BF16), ff2_w_down[l].astype(BF16))
    return xf.reshape(B, S, D)
```

```python
import functools

import jax
import jax.numpy as jnp
from jax import lax
from jax.experimental import pallas as pl
from jax.experimental.pallas import tpu as pltpu

F32 = jnp.float32
BF16 = jnp.bfloat16
EPS = 1e-6
NEG = -1e30

LANES = 128
SUBLANES = 8
VMEM_LIMIT = 56 * 1024 * 1024

ATTN_HEADS = 8
ATTN_HD = 128
MOBA_BLOCK = 256
MOBA_TOPK = 3
SSD_HD = 64
SSD_GROUPS = 8
SSD_HPG = 6
SSD_STATE = 128
SSD_CONV = 4
SSD_CHUNK = 128
MEM_HEADS = 4
MEM_HD = 128


def _cparams(sem):
    return pltpu.CompilerParams(dimension_semantics=sem, vmem_limit_bytes=VMEM_LIMIT)


def _rms(x, w):
    return x * lax.rsqrt(jnp.mean(x * x, axis=-1, keepdims=True) + EPS) * w


def _dot(a, b):
    return jnp.dot(a, b, preferred_element_type=F32)


def _dot_nt(a, b, precision=None):
    return lax.dot_general(a, b, (((1,), (1,)), ((), ())), preferred_element_type=F32,
                           precision=precision)


def _ffn_body(x_ref, nw_ref, wg_ref, wu_ref, wd_ref, o_ref, h_ref):
    j = pl.program_id(1)

    @pl.when(j == 0)
    def _():
        x = x_ref[...]
        h_ref[...] = _rms(x, nw_ref[...]).astype(BF16)
        o_ref[...] = x

    h = h_ref[...]
    g = _dot(h, wg_ref[...])
    u = _dot(h, wu_ref[...])
    a = (0.5 * (g * jax.nn.sigmoid(g)) * u).astype(BF16)
    o_ref[...] += _dot(a, wd_ref[...])


def _ffn(x, nw, w_gu, w_down, l, *, tm=512, tf=512):
    T, D = x.shape
    F = w_down.shape[1]
    nf = F // tf
    return pl.pallas_call(
        _ffn_body,
        grid=(T // tm, nf),
        in_specs=[
            pl.BlockSpec((tm, D), lambda i, j: (i, 0)),
            pl.BlockSpec((1, D), lambda i, j: (0, 0)),
            pl.BlockSpec((None, D, tf), lambda i, j: (l, 0, j)),
            pl.BlockSpec((None, D, tf), lambda i, j: (l, 0, j + nf)),
            pl.BlockSpec((None, tf, D), lambda i, j: (l, j, 0)),
        ],
        out_specs=pl.BlockSpec((tm, D), lambda i, j: (i, 0)),
        out_shape=jax.ShapeDtypeStruct((T, D), F32),
        scratch_shapes=[pltpu.VMEM((tm, D), BF16)],
        compiler_params=_cparams(("parallel", "arbitrary")),
        name="ffn",
    )(x, nw, w_gu, w_gu, w_down)


def _inproj_body(x_ref, nw_ref, w_ref, wdt_ref, p_ref, dt_ref, h_ref):
    j = pl.program_id(1)

    @pl.when(j == 0)
    def _():
        h = _rms(x_ref[...], nw_ref[...]).astype(BF16)
        h_ref[...] = h
        dt_ref[...] = _dot(h, wdt_ref[...])

    p_ref[...] = _dot(h_ref[...], w_ref[...])


def _in_proj(x, nw, w_in, l, *, tm=1024, tn=1024):
    T, D = x.shape
    N = w_in.shape[2] - LANES
    assert N % tn == 0
    return pl.pallas_call(
        _inproj_body,
        grid=(T // tm, N // tn),
        in_specs=[
            pl.BlockSpec((tm, D), lambda i, j: (i, 0)),
            pl.BlockSpec((1, D), lambda i, j: (0, 0)),
            pl.BlockSpec((None, D, tn), lambda i, j: (l, 0, j)),
            pl.BlockSpec((None, D, LANES), lambda i, j: (l, 0, N // LANES)),
        ],
        out_specs=[
            pl.BlockSpec((tm, tn), lambda i, j: (i, j)),
            pl.BlockSpec((tm, LANES), lambda i, j: (i, 0)),
        ],
        out_shape=[jax.ShapeDtypeStruct((T, N), F32), jax.ShapeDtypeStruct((T, LANES), F32)],
        scratch_shapes=[pltpu.VMEM((tm, D), BF16)],
        compiler_params=_cparams(("parallel", "arbitrary")),
        name="in_proj",
    )(x, nw, w_in, w_in)


MOBA_EXT = LANES
MOBA_SEL0 = 16
MOBA_VARIANT = 4


def _split3(x):
    hi = x.astype(BF16).astype(F32)
    r = x - hi
    mid = r.astype(BF16).astype(F32)
    lo = (r - mid).astype(BF16).astype(F32)
    return hi, mid, lo


def _rows16(rows, width):
    r_i = lax.broadcasted_iota(jnp.int32, (2 * SUBLANES, width), 0)
    out = jnp.zeros((2 * SUBLANES, width), F32)
    for k, row in enumerate(rows):
        out = jnp.where(r_i == k, row, out)
    return out


def _moba_body(q_ref, k_ref, v_ref, qn_ref, kn_ref, slope_ref, o_ref,
               kx_ref, vt_ref, kmean_ref, s_ref, *, nb):
    qi = pl.program_id(2)
    blk = MOBA_BLOCK
    seq = nb * blk
    scale = ATTN_HD ** -0.5
    slope = slope_ref[0:1, 0:1]

    @pl.when(qi == 0)
    def _():
        kf = _rms(k_ref[...], kn_ref[...])
        kx_ref[:, 0:ATTN_HD] = kf.astype(BF16)
        for n in range(nb):
            kmean_ref[n:n + 1, :] = jnp.mean(kf[n * blk:(n + 1) * blk], axis=0, keepdims=True)
        vt_ref[...] = v_ref[...].T.astype(BF16)
        pos = lax.broadcasted_iota(jnp.int32, (seq, MOBA_EXT), 0)
        c = lax.broadcasted_iota(jnp.int32, (seq, MOBA_EXT), 1)
        n_of = lax.shift_right_logical(pos, blk.bit_length() - 1)
        j_of = pos & (blk - 1)
        ext = jnp.where(c < 3, 1, jnp.where(c < 6, n_of, jnp.where(c < 9, j_of, 0)))
        ext = jnp.where(c - MOBA_SEL0 == n_of, 1, ext)
        kx_ref[:, ATTN_HD:] = ext.astype(F32).astype(BF16)

    qn = _rms(q_ref[...], qn_ref[...])

    gate = _dot_nt(kmean_ref[...], qn, precision=lax.Precision.HIGHEST)
    n_iota = lax.broadcasted_iota(jnp.int32, gate.shape, 0).astype(F32)
    qi_f = qi.astype(F32)
    past_blk = n_iota < qi_f
    gate = jnp.where(past_blk, gate, -jnp.inf)
    selb = jnp.full(gate.shape, NEG, F32)
    for _ in range(MOBA_TOPK):
        best = jnp.max(gate, axis=0, keepdims=True)
        first = jnp.min(jnp.where(gate == best, n_iota, float(nb)), axis=0, keepdims=True)
        pick = n_iota == first
        selb = jnp.where(pick, 0.0, selb)
        gate = jnp.where(pick, -jnp.inf, gate)
    selb = jnp.where(past_blk, selb, jnp.where(n_iota == qi_f, 0.0, NEG))

    t_row = (qi * blk + lax.broadcasted_iota(jnp.int32, (1, blk), 1)).astype(F32)
    ones = jnp.ones((1, blk), F32)
    bias_rows = (list(_split3(-slope * t_row)) + [v * ones for v in _split3(slope * float(blk))]
                 + [v * ones for v in _split3(slope)])
    qx = jnp.concatenate([
        (qn * scale).T.astype(BF16),
        _rows16(bias_rows, blk).astype(BF16),
        selb.astype(BF16),
        jnp.zeros((MOBA_EXT - MOBA_SEL0 - nb, blk), BF16),
    ], axis=0)

    key_i = lax.broadcasted_iota(jnp.int32, (blk, blk), 0)
    qry_i = lax.broadcasted_iota(jnp.int32, (blk, blk), 1)
    own = pl.multiple_of(qi * blk, blk)

    def attend(nk):
        n = nk * blk
        s_ref[0:n, :] = _dot(kx_ref[0:n, :], qx)
        s_own = s_ref[pl.ds(own, blk), :]
        s_ref[pl.ds(own, blk), :] = jnp.where(qry_i >= key_i, s_own, NEG)
        s = s_ref[0:n, :]
        p = jnp.exp(s - jnp.max(s, axis=0, keepdims=True))
        l = jnp.sum(p, axis=0, keepdims=True)
        acc = _dot(vt_ref[:, 0:n], p.astype(BF16))
        o_ref[...] = (acc / l).T.astype(o_ref.dtype)

    for nk in range(MOBA_VARIANT, nb + 1, MOBA_VARIANT):
        pl.when((qi >= nk - MOBA_VARIANT) & (qi < nk))(functools.partial(attend, nk))


def _moba(p_main, q_norm, k_norm, slopes, *, batch, seq):
    T = p_main.shape[0]
    blk = MOBA_BLOCK
    nb = seq // blk
    assert seq % blk == 0 and nb % MOBA_VARIANT == 0 and nb <= MOBA_EXT - MOBA_SEL0
    H = ATTN_HEADS
    body = functools.partial(_moba_body, nb=nb)
    return pl.pallas_call(
        body,
        grid=(batch, H, nb),
        in_specs=[
            pl.BlockSpec((blk, ATTN_HD), lambda b, h, i: (b * nb + i, h)),
            pl.BlockSpec((seq, ATTN_HD), lambda b, h, i: (b, H + h)),
            pl.BlockSpec((seq, ATTN_HD), lambda b, h, i: (b, 2 * H + h)),
            pl.BlockSpec((1, ATTN_HD), lambda b, h, i: (0, 0)),
            pl.BlockSpec((1, ATTN_HD), lambda b, h, i: (0, 0)),
            pl.BlockSpec((None, 1, LANES), lambda b, h, i: (h, 0, 0)),
        ],
        out_specs=pl.BlockSpec((blk, ATTN_HD), lambda b, h, i: (b * nb + i, h)),
        out_shape=jax.ShapeDtypeStruct((T, H * ATTN_HD), BF16),
        scratch_shapes=[
            pltpu.VMEM((seq, ATTN_HD + MOBA_EXT), BF16),
            pltpu.VMEM((ATTN_HD, seq), BF16),
            pltpu.VMEM((nb, ATTN_HD), F32),
            pltpu.VMEM((seq, blk), F32),
        ],
        compiler_params=_cparams(("parallel", "parallel", "arbitrary")),
        name="moba",
    )(p_main, p_main, p_main, q_norm, k_norm, slopes)


def _softplus(x):
    return jnp.maximum(x, 0.0) + jnp.log1p(jnp.exp(-jnp.abs(x)))


def _split_hi_lo(x):
    hi = x.astype(BF16)
    lo = (x - hi.astype(F32)).astype(BF16)
    return jnp.concatenate([hi, lo], axis=1)


LOG2E = 1.4426950408889634


def _ssd_dt_body(dt_ref, dtb_ref, alog_ref, dts_ref, acss_ref, acst_ref, *, rows):
    L = SSD_CHUNK
    li = lax.broadcasted_iota(jnp.int32, (L, L), 0)
    si = lax.broadcasted_iota(jnp.int32, (L, L), 1)
    tril = (li >= si).astype(F32)
    a2_row = -jnp.exp(alog_ref[...]) * LOG2E
    for c in range(rows // L):
        rs = slice(c * L, (c + 1) * L)
        dtv = _softplus(dt_ref[rs, :] + dtb_ref[...])
        acs = jnp.dot(tril, dtv * a2_row, preferred_element_type=F32,
                      precision=lax.Precision.HIGHEST)
        dts_ref[rs, :] = _split_hi_lo(dtv)
        acss_ref[rs, :] = _split_hi_lo(acs)
        acst_ref[c] = acs.T


def _ssd_dt(dt_raw, dt_bias_p, a_log_p, *, rows=1024):
    T = dt_raw.shape[0]
    L = SSD_CHUNK
    body = functools.partial(_ssd_dt_body, rows=rows)
    return pl.pallas_call(
        body,
        grid=(T // rows,),
        in_specs=[
            pl.BlockSpec((rows, LANES), lambda i: (i, 0)),
            pl.BlockSpec((1, LANES), lambda i: (0, 0)),
            pl.BlockSpec((1, LANES), lambda i: (0, 0)),
        ],
        out_specs=[
            pl.BlockSpec((rows, 2 * LANES), lambda i: (i, 0)),
            pl.BlockSpec((rows, 2 * LANES), lambda i: (i, 0)),
            pl.BlockSpec((rows // L, LANES, L), lambda i: (i, 0, 0)),
        ],
        out_shape=[jax.ShapeDtypeStruct((T, 2 * LANES), BF16), jax.ShapeDtypeStruct((T, 2 * LANES), BF16),
                   jax.ShapeDtypeStruct((T // L, LANES, L), F32)],
        compiler_params=_cparams(("parallel",)),
        name="ssd_dt",
    )(dt_raw, dt_bias_p, a_log_p)


def _ssd_body(xs_ref, bm_ref, cm_ref, xs_h_ref, bm_h_ref, cm_h_ref, z_ref, dts_ref, acss_ref, acst_ref,
              wx_ref, wb_ref, wc_ref, bx_ref, bb_ref, bc_ref, dsk_ref, nw_ref, o_ref,
              ux_ref, ub_ref, uc_ref, st_ref, *, rows):
    g = pl.program_id(1)
    r = pl.program_id(2)
    L = SSD_CHUNK
    P = SSD_HD
    W = SSD_HPG * P
    HALO = SUBLANES

    @pl.when(r == 0)
    def _():
        st_ref[...] = jnp.zeros_like(st_ref)

    first = r == 0

    def conv(u_ref, halo_ref, buf_ref, w_ref, b_ref):
        halo = halo_ref[...]
        buf_ref[0:HALO, :] = jnp.where(first, jnp.zeros_like(halo), halo)
        buf_ref[HALO:, :] = u_ref[...]
        acc = b_ref[...] + w_ref[SSD_CONV - 1:SSD_CONV, :] * u_ref[...]
        for k in range(SSD_CONV - 1):
            shift = SSD_CONV - 1 - k
            acc = acc + w_ref[k:k + 1, :] * buf_ref[HALO - shift:HALO - shift + rows, :]
        buf_ref[HALO:, :] = acc * jax.nn.sigmoid(acc)

    conv(xs_ref, xs_h_ref, ux_ref, wx_ref, bx_ref)
    conv(bm_ref, bm_h_ref, ub_ref, wb_ref, bb_ref)
    conv(cm_ref, cm_h_ref, uc_ref, wc_ref, bc_ref)

    k_i = lax.broadcasted_iota(jnp.int32, (2 * LANES, W), 0)
    c_i = lax.broadcasted_iota(jnp.int32, (2 * LANES, W), 1)
    head = g * SSD_HPG + lax.shift_right_logical(c_i, P.bit_length() - 1)
    expand = ((k_i == head) | (k_i == head + LANES)).astype(BF16)

    li = lax.broadcasted_iota(jnp.int32, (L, L), 0)
    si = lax.broadcasted_iota(jnp.int32, (L, L), 1)
    causal = li >= si
    lane = lax.broadcasted_iota(jnp.int32, (L, LANES), 1)

    for c in range(rows // L):
        r0 = c * L
        rs = slice(HALO + r0, HALO + r0 + L)
        dt_x = _dot(dts_ref[r0:r0 + L, :], expand)
        acs_x = _dot(acss_ref[r0:r0 + L, :], expand)
        acs_last = acs_x[L - 1:L, :]

        xs = ux_ref[rs, :]
        bmat = ub_ref[rs, :]
        cmat = uc_ref[rs, :]
        x = xs * dt_x
        xb = x.astype(BF16)
        cb16 = cmat.astype(BF16)
        cb = jnp.where(causal, _dot_nt(cb16, bmat.astype(BF16)), 0.0)

        tiles = []
        for m in range(W // LANES):
            res = []
            for jj in range(2):
                j = 2 * m + jj
                col = acs_x[:, j * P:j * P + 1]
                row = acst_ref[c, pl.ds(g * SSD_HPG + j, 1), :]
                dec = jnp.exp2(jnp.minimum(col - row, 0.0))
                sc = (cb * dec).astype(BF16)
                res.append(_dot(sc, xb[:, m * LANES:(m + 1) * LANES]))
            tiles.append(jnp.where(lane < P, res[0], res[1]))
        y = jnp.concatenate(tiles, axis=1)

        st = st_ref[...]
        y = y + _dot(cb16, st.astype(BF16)) * jnp.exp2(acs_x)
        xd = (x * jnp.exp2(acs_last - acs_x)).astype(BF16)
        st_ref[...] = st * jnp.exp2(acs_last) + _dot(bmat.T.astype(BF16), xd)

        y = y + dsk_ref[...] * xs
        z = z_ref[r0:r0 + L, :]
        gt = y * (z * jax.nn.sigmoid(z))
        gn = gt * lax.rsqrt(jnp.mean(gt * gt, axis=-1, keepdims=True) + EPS)
        o_ref[r0:r0 + L, :] = (gn * nw_ref[...]).astype(o_ref.dtype)


def _ssd(p_main, dts, acss, acst, conv_w, conv_b, d_skip_x, ssd_norm, *, batch, seq, rows=512):
    T = p_main.shape[0]
    G = SSD_GROUPS
    N = SSD_STATE
    W = SSD_HPG * SSD_HD
    nr = seq // rows
    hb = rows // SUBLANES
    z0, x0 = 3 * ATTN_HEADS * ATTN_HD // W, (3 * ATTN_HEADS * ATTN_HD + G * W) // W
    b0 = (3 * ATTN_HEADS * ATTN_HD + 2 * G * W) // N
    c0 = b0 + G
    cwb0, cwc0 = G * W // N, G * W // N + G

    def rowblk(b, g, r):
        return b * nr + r

    def halo(b, g, r):
        return jnp.maximum((b * nr + r) * hb - 1, 0)

    body = functools.partial(_ssd_body, rows=rows)
    return pl.pallas_call(
        body,
        grid=(batch, G, nr),
        in_specs=[
            pl.BlockSpec((rows, W), lambda b, g, r: (rowblk(b, g, r), x0 + g)),
            pl.BlockSpec((rows, N), lambda b, g, r: (rowblk(b, g, r), b0 + g)),
            pl.BlockSpec((rows, N), lambda b, g, r: (rowblk(b, g, r), c0 + g)),
            pl.BlockSpec((SUBLANES, W), lambda b, g, r: (halo(b, g, r), x0 + g)),
            pl.BlockSpec((SUBLANES, N), lambda b, g, r: (halo(b, g, r), b0 + g)),
            pl.BlockSpec((SUBLANES, N), lambda b, g, r: (halo(b, g, r), c0 + g)),
            pl.BlockSpec((rows, W), lambda b, g, r: (rowblk(b, g, r), z0 + g)),
            pl.BlockSpec((rows, 2 * LANES), lambda b, g, r: (rowblk(b, g, r), 0)),
            pl.BlockSpec((rows, 2 * LANES), lambda b, g, r: (rowblk(b, g, r), 0)),
            pl.BlockSpec((rows // SSD_CHUNK, LANES, SSD_CHUNK), lambda b, g, r: (rowblk(b, g, r), 0, 0)),
            pl.BlockSpec((SSD_CONV, W), lambda b, g, r: (0, g)),
            pl.BlockSpec((SSD_CONV, N), lambda b, g, r: (0, cwb0 + g)),
            pl.BlockSpec((SSD_CONV, N), lambda b, g, r: (0, cwc0 + g)),
            pl.BlockSpec((1, W), lambda b, g, r: (0, g)),
            pl.BlockSpec((1, N), lambda b, g, r: (0, cwb0 + g)),
            pl.BlockSpec((1, N), lambda b, g, r: (0, cwc0 + g)),
            pl.BlockSpec((1, W), lambda b, g, r: (0, g)),
            pl.BlockSpec((1, W), lambda b, g, r: (0, g)),
        ],
        out_specs=pl.BlockSpec((rows, W), lambda b, g, r: (rowblk(b, g, r), g)),
        out_shape=jax.ShapeDtypeStruct((T, G * W), BF16),
        scratch_shapes=[
            pltpu.VMEM((SUBLANES + rows, W), F32),
            pltpu.VMEM((SUBLANES + rows, N), F32),
            pltpu.VMEM((SUBLANES + rows, N), F32),
            pltpu.VMEM((N, W), F32),
        ],
        compiler_params=_cparams(("parallel", "parallel", "arbitrary")),
        name="ssd",
    )(p_main, p_main, p_main, p_main, p_main, p_main, p_main, dts, acss, acst,
      conv_w, conv_w, conv_w, conv_b, conv_b, conv_b, d_skip_x, ssd_norm)


def _outproj_body(x_ref, ya_ref, ys_ref, w_ref, o_ref):
    ka = ya_ref.shape[1]
    o_ref[...] = x_ref[...] + _dot(ya_ref[...], w_ref[0:ka, :]) + _dot(ys_ref[...], w_ref[ka:, :])


def _out_proj(x, ya, ys, w_out, l, *, tm=512, tn=1024):
    T, D = x.shape
    Ka, Ks = ya.shape[1], ys.shape[1]
    return pl.pallas_call(
        _outproj_body,
        grid=(T // tm, D // tn),
        in_specs=[
            pl.BlockSpec((tm, tn), lambda i, j: (i, j)),
            pl.BlockSpec((tm, Ka), lambda i, j: (i, 0)),
            pl.BlockSpec((tm, Ks), lambda i, j: (i, 0)),
            pl.BlockSpec((None, Ka + Ks, tn), lambda i, j: (l, 0, j)),
        ],
        out_specs=pl.BlockSpec((tm, tn), lambda i, j: (i, j)),
        out_shape=jax.ShapeDtypeStruct((T, D), F32),
        compiler_params=_cparams(("parallel", "arbitrary")),
        name="out_proj",
    )(x, ya, ys, w_out)


def _memkv_body(m_ref, nw_ref, wk_ref, wv_ref, kn_ref, k_ref, v_ref):
    h = _rms(m_ref[...], nw_ref[...]).astype(BF16)
    k = _dot(h, wk_ref[...])
    v_ref[...] = _dot(h, wv_ref[...]).astype(BF16)
    for hh in range(MEM_HEADS):
        sl = slice(hh * MEM_HD, (hh + 1) * MEM_HD)
        k_ref[:, sl] = _rms(k[:, sl], kn_ref[...]).astype(BF16)


def _mem_kv(mem, nw, wk, wv, kn, l):
    M, D = mem.shape
    Wm = wk.shape[2]
    full = lambda shape: pl.BlockSpec(shape, lambda i: (0,) * len(shape))
    layer = lambda shape: pl.BlockSpec((None,) + shape, lambda i: (l,) + (0,) * len(shape))
    return pl.pallas_call(
        _memkv_body,
        grid=(1,),
        in_specs=[full((M, D)), full((1, D)), layer((D, Wm)), layer((D, Wm)), full((1, MEM_HD))],
        out_specs=[full((M, Wm)), full((M, Wm))],
        out_shape=[jax.ShapeDtypeStruct((M, Wm), BF16)] * 2,
        compiler_params=_cparams(("arbitrary",)),
        name="mem_kv",
    )(mem, nw, wk, wv, kn)


def _memattn_body(x_ref, nw_ref, wq_ref, qn_ref, k_ref, v_ref, wo_ref, o_ref):
    x = x_ref[...]
    h = _rms(x, nw_ref[...]).astype(BF16)
    q = _dot(h, wq_ref[...])
    scale = MEM_HD ** -0.5
    outs = []
    for hh in range(MEM_HEADS):
        sl = slice(hh * MEM_HD, (hh + 1) * MEM_HD)
        qh = _rms(q[:, sl], qn_ref[...]).astype(BF16)
        s = _dot_nt(qh, k_ref[:, sl]) * scale
        e = jnp.exp(s - jnp.max(s, axis=-1, keepdims=True))
        p = e / jnp.sum(e, axis=-1, keepdims=True)
        outs.append(_dot(p.astype(BF16), v_ref[:, sl]))
    o = jnp.concatenate(outs, axis=1).astype(BF16)
    o_ref[...] = x + _dot(o, wo_ref[...])


def _mem_attn(x, nw, wq, qn, k, v, wo, l, *, seq, mem_len, tm=512):
    T, D = x.shape
    Wm = wq.shape[2]
    per_b = seq // tm
    return pl.pallas_call(
        _memattn_body,
        grid=(T // tm,),
        in_specs=[
            pl.BlockSpec((tm, D), lambda i: (i, 0)),
            pl.BlockSpec((1, D), lambda i: (0, 0)),
            pl.BlockSpec((None, D, Wm), lambda i: (l, 0, 0)),
            pl.BlockSpec((1, MEM_HD), lambda i: (0, 0)),
            pl.BlockSpec((mem_len, Wm), lambda i: (i // per_b, 0)),
            pl.BlockSpec((mem_len, Wm), lambda i: (i // per_b, 0)),
            pl.BlockSpec((None, Wm, D), lambda i: (l, 0, 0)),
        ],
        out_specs=pl.BlockSpec((tm, D), lambda i: (i, 0)),
        out_shape=jax.ShapeDtypeStruct((T, D), F32),
        compiler_params=_cparams(("parallel",)),
        name="mem_attn",
    )(x, nw, wq, qn, k, v, wo)


def _pad_lanes(v):
    return jnp.pad(v.astype(F32), (0, LANES - v.shape[0])).reshape(1, LANES)


def kernel(x, mem, ff1_norm, ff1_w_gu, ff1_w_down, mix_norm, w_in, q_norm, k_norm, conv_w, conv_b,
           dt_bias, a_log, d_skip, ssd_norm, w_out, xmem_norm, mem_norm, mem_wq, mem_wk, mem_wv,
           mem_q_norm, mem_k_norm, mem_wo, ff2_norm, ff2_w_gu, ff2_w_down):
    B, S, D = x.shape
    M = mem.shape[1]
    depth = w_in.shape[0]
    n_dt = SSD_GROUPS * SSD_HPG
    row = lambda v: v.reshape(1, -1)

    slopes = jnp.exp2(-8.0 * jnp.arange(1, ATTN_HEADS + 1, dtype=F32) / ATTN_HEADS)
    slopes = jnp.broadcast_to(slopes[:, None, None], (ATTN_HEADS, 1, LANES))
    bf = lambda w: w.astype(BF16)
    ff1_w_gu, ff1_w_down, ff2_w_gu, ff2_w_down = bf(ff1_w_gu), bf(ff1_w_down), bf(ff2_w_gu), bf(ff2_w_down)
    w_in = bf(jnp.pad(w_in, ((0, 0), (0, 0), (0, LANES - n_dt))))
    w_out = bf(w_out)
    mem_wq, mem_wk, mem_wv, mem_wo = bf(mem_wq), bf(mem_wk), bf(mem_wv), bf(mem_wo)

    xf = x.reshape(B * S, D)
    memf = mem.reshape(B * M, D)
    for l in range(depth):
        xf = _ffn(xf, row(ff1_norm[l]), ff1_w_gu, ff1_w_down, l)

        p_main, dt_raw = _in_proj(xf, row(mix_norm[l]), w_in, l)
        y_attn = _moba(p_main, row(q_norm[l]), row(k_norm[l]), slopes, batch=B, seq=S)
        dts, acss, acst = _ssd_dt(dt_raw, _pad_lanes(dt_bias[l]), _pad_lanes(a_log[l]))
        y_ssd = _ssd(p_main, dts, acss, acst, conv_w[l], row(conv_b[l]),
                     row(jnp.repeat(d_skip[l], SSD_HD)), row(ssd_norm[l]), batch=B, seq=S)
        xf = _out_proj(xf, y_attn, y_ssd, w_out, l)

        k_mem, v_mem = _mem_kv(memf, row(mem_norm[l]), mem_wk, mem_wv, row(mem_k_norm[l]), l)
        xf = _mem_attn(xf, row(xmem_norm[l]), mem_wq, row(mem_q_norm[l]), k_mem, v_mem, mem_wo, l,
                       seq=S, mem_len=M)

        xf = _ffn(xf, row(ff2_norm[l]), ff2_w_gu, ff2_w_down, l)
    return xf.reshape(B, S, D)
```

```python
import functools

import jax
import jax.numpy as jnp
from jax import lax
from jax.experimental import pallas as pl
from jax.experimental.pallas import tpu as pltpu

F32 = jnp.float32
BF16 = jnp.bfloat16
EPS = 1e-6
NEG = -1e30
LOG2E = 1.4426950408889634

LANES = 128
SUBLANES = 8
VMEM_LIMIT = 56 * 1024 * 1024

ATTN_HEADS = 8
ATTN_HD = 128
MOBA_BLOCK = 256
MOBA_TOPK = 3
SSD_HD = 64
SSD_GROUPS = 8
SSD_HPG = 6
SSD_STATE = 128
SSD_CONV = 4
SSD_CHUNK = 128
MEM_HEADS = 4
MEM_HD = 128


def _cparams(sem):
    return pltpu.CompilerParams(dimension_semantics=sem, vmem_limit_bytes=VMEM_LIMIT)


def _rms(x, w):
    return x * lax.rsqrt(jnp.mean(x * x, axis=-1, keepdims=True) + EPS) * w


def _dot(a, b):
    return jnp.dot(a, b, preferred_element_type=F32)


def _dot_nt(a, b, precision=None):
    return lax.dot_general(a, b, (((1,), (1,)), ((), ())), preferred_element_type=F32,
                           precision=precision)


CAST_BLOCK_BYTES = 12 * 1024 * 1024


def _cast_body(w_ref, o_ref):
    n = w_ref.shape[1]
    o_ref[:, 0:n] = w_ref[...].astype(BF16)
    if o_ref.shape[1] > n:
        o_ref[:, n:] = jnp.zeros((o_ref.shape[0], o_ref.shape[1] - n), BF16)


def _cast_bf16(w, pad_cols=0):
    nl, R, C = w.shape
    rows = nl * R
    tr = rows
    while tr * C * 4 > CAST_BLOCK_BYTES and tr % 32 == 0:
        tr //= 2
    out = pl.pallas_call(
        _cast_body,
        grid=(rows // tr,),
        in_specs=[pl.BlockSpec((tr, C), lambda i: (i, 0))],
        out_specs=pl.BlockSpec((tr, C + pad_cols), lambda i: (i, 0)),
        out_shape=jax.ShapeDtypeStruct((rows, C + pad_cols), BF16),
        compiler_params=_cparams(("parallel",)),
        name="cast_bf16",
    )(w.reshape(rows, C))
    return out.reshape(nl, R, C + pad_cols)


def _ffn_body(x_ref, nw_ref, wg_ref, wu_ref, wd_ref, o_ref, h_ref):
    j = pl.program_id(1)

    @pl.when(j == 0)
    def _():
        x = x_ref[...]
        h_ref[...] = _rms(x, nw_ref[...]).astype(BF16)
        o_ref[...] = x

    h = h_ref[...]
    g = _dot(h, wg_ref[...])
    u = _dot(h, wu_ref[...])
    a = (0.5 * (g * jax.nn.sigmoid(g)) * u).astype(BF16)
    o_ref[...] += _dot(a, wd_ref[...])


def _ffn(x, nw, w_gu, w_down, l, *, tm=1024, tf=512):
    T, D = x.shape
    F = w_down.shape[1]
    nf = F // tf
    return pl.pallas_call(
        _ffn_body,
        grid=(T // tm, nf),
        in_specs=[
            pl.BlockSpec((tm, D), lambda i, j: (i, 0)),
            pl.BlockSpec((1, D), lambda i, j: (0, 0)),
            pl.BlockSpec((None, D, tf), lambda i, j: (l, 0, j)),
            pl.BlockSpec((None, D, tf), lambda i, j: (l, 0, j + nf)),
            pl.BlockSpec((None, tf, D), lambda i, j: (l, j, 0)),
        ],
        out_specs=pl.BlockSpec((tm, D), lambda i, j: (i, 0)),
        out_shape=jax.ShapeDtypeStruct((T, D), F32),
        scratch_shapes=[pltpu.VMEM((tm, D), BF16)],
        compiler_params=_cparams(("parallel", "arbitrary")),
        name="ffn",
    )(x, nw, w_gu, w_gu, w_down)


def _inproj_body(x_ref, nw_ref, w_ref, wdt_ref, p_ref, dt_ref, h_ref):
    j = pl.program_id(1)

    @pl.when(j == 0)
    def _():
        h = _rms(x_ref[...], nw_ref[...]).astype(BF16)
        h_ref[...] = h
        dt_ref[...] = _dot(h, wdt_ref[...])

    p_ref[...] = _dot(h_ref[...], w_ref[...])


def _in_proj(x, nw, w_in, l, *, tm=1024, tn=1024):
    T, D = x.shape
    N = w_in.shape[2] - LANES
    assert N % tn == 0
    return pl.pallas_call(
        _inproj_body,
        grid=(T // tm, N // tn),
        in_specs=[
            pl.BlockSpec((tm, D), lambda i, j: (i, 0)),
            pl.BlockSpec((1, D), lambda i, j: (0, 0)),
            pl.BlockSpec((None, D, tn), lambda i, j: (l, 0, j)),
            pl.BlockSpec((None, D, LANES), lambda i, j: (l, 0, N // LANES)),
        ],
        out_specs=[
            pl.BlockSpec((tm, tn), lambda i, j: (i, j)),
            pl.BlockSpec((tm, LANES), lambda i, j: (i, 0)),
        ],
        out_shape=[jax.ShapeDtypeStruct((T, N), F32), jax.ShapeDtypeStruct((T, LANES), F32)],
        scratch_shapes=[pltpu.VMEM((tm, D), BF16)],
        compiler_params=_cparams(("parallel", "arbitrary")),
        name="in_proj",
    )(x, nw, w_in, w_in)


MOBA_EXT = LANES
MOBA_SEL0 = 16
MOBA_VARIANT = 4
MOBA_QBLOCKS = 2


def _split3(x):
    hi = x.astype(BF16).astype(F32)
    r = x - hi
    mid = r.astype(BF16).astype(F32)
    lo = (r - mid).astype(BF16).astype(F32)
    return hi, mid, lo


def _rows16(rows, width):
    r_i = lax.broadcasted_iota(jnp.int32, (2 * SUBLANES, width), 0)
    out = jnp.zeros((2 * SUBLANES, width), F32)
    for k, row in enumerate(rows):
        out = jnp.where(r_i == k, row, out)
    return out


def _moba_body(q_ref, k_ref, v_ref, qn_ref, kn_ref, slope_ref, o_ref,
               kx_ref, vt_ref, kmean_ref, *, nb):
    qi = pl.program_id(2)
    blk = MOBA_BLOCK
    seq = nb * blk
    scale = ATTN_HD ** -0.5
    slope = slope_ref[0:1, 0:1]

    @pl.when(qi == 0)
    def _():
        kf = _rms(k_ref[...], kn_ref[...])
        kx_ref[:, 0:ATTN_HD] = kf.astype(BF16)
        for n in range(nb):
            kmean_ref[n:n + 1, :] = jnp.mean(kf[n * blk:(n + 1) * blk], axis=0, keepdims=True)
        vt_ref[...] = v_ref[...].T.astype(BF16)
        pos = lax.broadcasted_iota(jnp.int32, (seq, MOBA_EXT), 0)
        c = lax.broadcasted_iota(jnp.int32, (seq, MOBA_EXT), 1)
        n_of = lax.shift_right_logical(pos, blk.bit_length() - 1)
        j_of = pos & (blk - 1)
        ext = jnp.where(c < 3, 1, jnp.where(c < 6, n_of, jnp.where(c < 9, j_of, 0)))
        ext = jnp.where(c - MOBA_SEL0 == n_of, 1, ext)
        kx_ref[:, ATTN_HD:] = ext.astype(F32).astype(BF16)

    qw = MOBA_QBLOCKS * blk
    qn = _rms(q_ref[...], qn_ref[...])

    gate = _dot_nt(kmean_ref[...], qn, precision=lax.Precision.HIGHEST)
    n_iota = lax.broadcasted_iota(jnp.int32, gate.shape, 0).astype(F32)
    lane_q = lax.broadcasted_iota(jnp.int32, (1, qw), 1)
    cur = (qi * MOBA_QBLOCKS + lax.shift_right_logical(lane_q, blk.bit_length() - 1)).astype(F32)
    past_blk = n_iota < cur
    gate = jnp.where(past_blk, gate, -jnp.inf)
    selb = jnp.full(gate.shape, NEG, F32)
    for _ in range(MOBA_TOPK):
        best = jnp.max(gate, axis=0, keepdims=True)
        first = jnp.min(jnp.where(gate == best, n_iota, float(nb)), axis=0, keepdims=True)
        pick = n_iota == first
        selb = jnp.where(pick, 0.0, selb)
        gate = jnp.where(pick, -jnp.inf, gate)
    selb = jnp.where(past_blk, selb, NEG)

    slope2 = slope * LOG2E
    t_row = (qi * qw + lane_q).astype(F32)
    ones = jnp.ones((1, qw), F32)
    bias_rows = (list(_split3(-slope2 * t_row)) + [v * ones for v in _split3(slope2 * float(blk))]
                 + [v * ones for v in _split3(slope2)])
    q_rows = [(qn * (scale * LOG2E)).T.astype(BF16), _rows16(bias_rows, qw).astype(BF16)]
    qx_past = jnp.concatenate(
        q_rows + [selb.astype(BF16), jnp.zeros((MOBA_EXT - MOBA_SEL0 - nb, qw), BF16)], axis=0)
    qx_own = jnp.concatenate(q_rows + [jnp.zeros((MOBA_EXT - MOBA_SEL0, qw), BF16)], axis=0)

    key_i = lax.broadcasted_iota(jnp.int32, (blk, blk), 0)
    qry_i = lax.broadcasted_iota(jnp.int32, (blk, blk), 1)
    lanes = lambda a, u: a[:, u * blk:(u + 1) * blk]
    row_cat = lambda parts: jnp.concatenate(parts, axis=1)

    def attend(n_past):
        n = n_past * blk
        owns = [pl.ds(pl.multiple_of((qi * MOBA_QBLOCKS + u) * blk, blk), blk) for u in range(MOBA_QBLOCKS)]
        s_own = [jnp.where(qry_i >= key_i, _dot(kx_ref[owns[u], :], lanes(qx_own, u)), NEG)
                 for u in range(MOBA_QBLOCKS)]
        s = _dot(kx_ref[0:n, :], qx_past)
        m = jnp.maximum(jnp.max(s, axis=0, keepdims=True),
                        row_cat([jnp.max(so, axis=0, keepdims=True) for so in s_own]))
        p = jnp.exp2(s - m)
        p_own = [jnp.exp2(s_own[u] - lanes(m, u)) for u in range(MOBA_QBLOCKS)]
        l = jnp.sum(p, axis=0, keepdims=True) + row_cat([jnp.sum(po, axis=0, keepdims=True) for po in p_own])
        acc = _dot(vt_ref[:, 0:n], p.astype(BF16)) + row_cat(
            [_dot(vt_ref[:, owns[u]], p_own[u].astype(BF16)) for u in range(MOBA_QBLOCKS)])
        o_ref[...] = (acc / l).T.astype(o_ref.dtype)

    per = MOBA_VARIANT // MOBA_QBLOCKS
    for v in range(nb // MOBA_VARIANT):
        pl.when((qi >= v * per) & (qi < (v + 1) * per))(
            functools.partial(attend, (v + 1) * MOBA_VARIANT - 1))


def _moba(p_main, q_norm, k_norm, slopes, *, batch, seq):
    T = p_main.shape[0]
    blk = MOBA_BLOCK
    nb = seq // blk
    assert seq % blk == 0 and nb % MOBA_VARIANT == 0 and nb <= MOBA_EXT - MOBA_SEL0
    assert MOBA_VARIANT % MOBA_QBLOCKS == 0
    H = ATTN_HEADS
    qw = MOBA_QBLOCKS * blk
    nq = seq // qw
    body = functools.partial(_moba_body, nb=nb)
    return pl.pallas_call(
        body,
        grid=(batch, H, nq),
        in_specs=[
            pl.BlockSpec((qw, ATTN_HD), lambda b, h, i: (b * nq + i, h)),
            pl.BlockSpec((seq, ATTN_HD), lambda b, h, i: (b, H + h)),
            pl.BlockSpec((seq, ATTN_HD), lambda b, h, i: (b, 2 * H + h)),
            pl.BlockSpec((1, ATTN_HD), lambda b, h, i: (0, 0)),
            pl.BlockSpec((1, ATTN_HD), lambda b, h, i: (0, 0)),
            pl.BlockSpec((None, 1, LANES), lambda b, h, i: (h, 0, 0)),
        ],
        out_specs=pl.BlockSpec((qw, ATTN_HD), lambda b, h, i: (b * nq + i, h)),
        out_shape=jax.ShapeDtypeStruct((T, H * ATTN_HD), BF16),
        scratch_shapes=[
            pltpu.VMEM((seq, ATTN_HD + MOBA_EXT), BF16),
            pltpu.VMEM((ATTN_HD, seq), BF16),
            pltpu.VMEM((nb, ATTN_HD), F32),
        ],
        compiler_params=_cparams(("parallel", "parallel", "arbitrary")),
        name="moba",
    )(p_main, p_main, p_main, q_norm, k_norm, slopes)


def _softplus(x):
    return jnp.maximum(x, 0.0) + jnp.log1p(jnp.exp(-jnp.abs(x)))


def _split_hi_lo(x):
    hi = x.astype(BF16)
    lo = (x - hi.astype(F32)).astype(BF16)
    return jnp.concatenate([hi, lo], axis=1)


def _ssd_dt_body(dt_ref, dtb_ref, alog_ref, dts_ref, acss_ref, acst_ref, *, rows):
    L = SSD_CHUNK
    li = lax.broadcasted_iota(jnp.int32, (L, L), 0)
    si = lax.broadcasted_iota(jnp.int32, (L, L), 1)
    tril = (li >= si).astype(F32)
    a2_row = -jnp.exp(alog_ref[...]) * LOG2E
    for c in range(rows // L):
        rs = slice(c * L, (c + 1) * L)
        dtv = _softplus(dt_ref[rs, :] + dtb_ref[...])
        acs = jnp.dot(tril, dtv * a2_row, preferred_element_type=F32,
                      precision=lax.Precision.HIGHEST)
        dts_ref[rs, :] = _split_hi_lo(dtv)
        acss_ref[rs, :] = _split_hi_lo(acs)
        acst_ref[c] = acs.T


def _ssd_dt(dt_raw, dt_bias_p, a_log_p, *, rows=1024):
    T = dt_raw.shape[0]
    L = SSD_CHUNK
    body = functools.partial(_ssd_dt_body, rows=rows)
    return pl.pallas_call(
        body,
        grid=(T // rows,),
        in_specs=[
            pl.BlockSpec((rows, LANES), lambda i: (i, 0)),
            pl.BlockSpec((1, LANES), lambda i: (0, 0)),
            pl.BlockSpec((1, LANES), lambda i: (0, 0)),
        ],
        out_specs=[
            pl.BlockSpec((rows, 2 * LANES), lambda i: (i, 0)),
            pl.BlockSpec((rows, 2 * LANES), lambda i: (i, 0)),
            pl.BlockSpec((rows // L, LANES, L), lambda i: (i, 0, 0)),
        ],
        out_shape=[jax.ShapeDtypeStruct((T, 2 * LANES), BF16), jax.ShapeDtypeStruct((T, 2 * LANES), BF16),
                   jax.ShapeDtypeStruct((T // L, LANES, L), F32)],
        compiler_params=_cparams(("parallel",)),
        name="ssd_dt",
    )(dt_raw, dt_bias_p, a_log_p)


def _ssd_body(xs_ref, bm_ref, cm_ref, xs_h_ref, bm_h_ref, cm_h_ref, z_ref, dts_ref, acss_ref, acst_ref,
              wx_ref, wb_ref, wc_ref, bx_ref, bb_ref, bc_ref, dsk_ref, nw_ref, o_ref,
              ux_ref, ub_ref, uc_ref, st_ref, *, rows):
    g = pl.program_id(1)
    r = pl.program_id(2)
    L = SSD_CHUNK
    P = SSD_HD
    W = SSD_HPG * P
    HALO = SUBLANES

    @pl.when(r == 0)
    def _():
        st_ref[...] = jnp.zeros_like(st_ref)

    first = r == 0

    def conv(u_ref, halo_ref, buf_ref, w_ref, b_ref):
        halo = halo_ref[...]
        buf_ref[0:HALO, :] = jnp.where(first, jnp.zeros_like(halo), halo)
        buf_ref[HALO:, :] = u_ref[...]
        acc = b_ref[...] + w_ref[SSD_CONV - 1:SSD_CONV, :] * u_ref[...]
        for k in range(SSD_CONV - 1):
            shift = SSD_CONV - 1 - k
            acc = acc + w_ref[k:k + 1, :] * buf_ref[HALO - shift:HALO - shift + rows, :]
        buf_ref[HALO:, :] = acc * jax.nn.sigmoid(acc)

    conv(xs_ref, xs_h_ref, ux_ref, wx_ref, bx_ref)
    conv(bm_ref, bm_h_ref, ub_ref, wb_ref, bb_ref)
    conv(cm_ref, cm_h_ref, uc_ref, wc_ref, bc_ref)

    k_i = lax.broadcasted_iota(jnp.int32, (2 * LANES, W), 0)
    c_i = lax.broadcasted_iota(jnp.int32, (2 * LANES, W), 1)
    head = g * SSD_HPG + lax.shift_right_logical(c_i, P.bit_length() - 1)
    expand = ((k_i == head) | (k_i == head + LANES)).astype(BF16)

    li = lax.broadcasted_iota(jnp.int32, (L, L), 0)
    si = lax.broadcasted_iota(jnp.int32, (L, L), 1)
    causal = li >= si
    lane = lax.broadcasted_iota(jnp.int32, (L, LANES), 1)

    for c in range(rows // L):
        r0 = c * L
        rs = slice(HALO + r0, HALO + r0 + L)
        dt_x = _dot(dts_ref[r0:r0 + L, :], expand)
        acs_x = _dot(acss_ref[r0:r0 + L, :], expand)
        acs_last = acs_x[L - 1:L, :]

        xs = ux_ref[rs, :]
        bmat = ub_ref[rs, :]
        cmat = uc_ref[rs, :]
        x = xs * dt_x
        xb = x.astype(BF16)
        cb16 = cmat.astype(BF16)
        cb = jnp.where(causal, _dot_nt(cb16, bmat.astype(BF16)), 0.0)

        tiles = []
        for m in range(W // LANES):
            res = []
            for jj in range(2):
                j = 2 * m + jj
                col = acs_x[:, j * P:j * P + 1]
                row = acst_ref[c, pl.ds(g * SSD_HPG + j, 1), :]
                dec = jnp.exp2(jnp.minimum(col - row, 0.0))
                sc = (cb * dec).astype(BF16)
                res.append(_dot(sc, xb[:, m * LANES:(m + 1) * LANES]))
            tiles.append(jnp.where(lane < P, res[0], res[1]))
        y = jnp.concatenate(tiles, axis=1)

        st = st_ref[...]
        y = y + _dot(cb16, st.astype(BF16)) * jnp.exp2(acs_x)
        xd = (x * jnp.exp2(acs_last - acs_x)).astype(BF16)
        st_ref[...] = st * jnp.exp2(acs_last) + _dot(bmat.T.astype(BF16), xd)

        y = y + dsk_ref[...] * xs
        z = z_ref[r0:r0 + L, :]
        gt = y * (z * jax.nn.sigmoid(z))
        gn = gt * lax.rsqrt(jnp.mean(gt * gt, axis=-1, keepdims=True) + EPS)
        o_ref[r0:r0 + L, :] = (gn * nw_ref[...]).astype(o_ref.dtype)


def _ssd(p_main, dts, acss, acst, conv_w, conv_b, d_skip_x, ssd_norm, *, batch, seq, rows=512):
    T = p_main.shape[0]
    G = SSD_GROUPS
    N = SSD_STATE
    W = SSD_HPG * SSD_HD
    nr = seq // rows
    hb = rows // SUBLANES
    z0, x0 = 3 * ATTN_HEADS * ATTN_HD // W, (3 * ATTN_HEADS * ATTN_HD + G * W) // W
    b0 = (3 * ATTN_HEADS * ATTN_HD + 2 * G * W) // N
    c0 = b0 + G
    cwb0, cwc0 = G * W // N, G * W // N + G

    def rowblk(b, g, r):
        return b * nr + r

    def halo(b, g, r):
        return jnp.maximum((b * nr + r) * hb - 1, 0)

    body = functools.partial(_ssd_body, rows=rows)
    return pl.pallas_call(
        body,
        grid=(batch, G, nr),
        in_specs=[
            pl.BlockSpec((rows, W), lambda b, g, r: (rowblk(b, g, r), x0 + g)),
            pl.BlockSpec((rows, N), lambda b, g, r: (rowblk(b, g, r), b0 + g)),
            pl.BlockSpec((rows, N), lambda b, g, r: (rowblk(b, g, r), c0 + g)),
            pl.BlockSpec((SUBLANES, W), lambda b, g, r: (halo(b, g, r), x0 + g)),
            pl.BlockSpec((SUBLANES, N), lambda b, g, r: (halo(b, g, r), b0 + g)),
            pl.BlockSpec((SUBLANES, N), lambda b, g, r: (halo(b, g, r), c0 + g)),
            pl.BlockSpec((rows, W), lambda b, g, r: (rowblk(b, g, r), z0 + g)),
            pl.BlockSpec((rows, 2 * LANES), lambda b, g, r: (rowblk(b, g, r), 0)),
            pl.BlockSpec((rows, 2 * LANES), lambda b, g, r: (rowblk(b, g, r), 0)),
            pl.BlockSpec((rows // SSD_CHUNK, LANES, SSD_CHUNK), lambda b, g, r: (rowblk(b, g, r), 0, 0)),
            pl.BlockSpec((SSD_CONV, W), lambda b, g, r: (0, g)),
            pl.BlockSpec((SSD_CONV, N), lambda b, g, r: (0, cwb0 + g)),
            pl.BlockSpec((SSD_CONV, N), lambda b, g, r: (0, cwc0 + g)),
            pl.BlockSpec((1, W), lambda b, g, r: (0, g)),
            pl.BlockSpec((1, N), lambda b, g, r: (0, cwb0 + g)),
            pl.BlockSpec((1, N), lambda b, g, r: (0, cwc0 + g)),
            pl.BlockSpec((1, W), lambda b, g, r: (0, g)),
            pl.BlockSpec((1, W), lambda b, g, r: (0, g)),
        ],
        out_specs=pl.BlockSpec((rows, W), lambda b, g, r: (rowblk(b, g, r), g)),
        out_shape=jax.ShapeDtypeStruct((T, G * W), BF16),
        scratch_shapes=[
            pltpu.VMEM((SUBLANES + rows, W), F32),
            pltpu.VMEM((SUBLANES + rows, N), F32),
            pltpu.VMEM((SUBLANES + rows, N), F32),
            pltpu.VMEM((N, W), F32),
        ],
        compiler_params=_cparams(("parallel", "parallel", "arbitrary")),
        name="ssd",
    )(p_main, p_main, p_main, p_main, p_main, p_main, p_main, dts, acss, acst,
      conv_w, conv_w, conv_w, conv_b, conv_b, conv_b, d_skip_x, ssd_norm)


def _outproj_body(x_ref, ya_ref, ys_ref, w_ref, o_ref):
    ka = ya_ref.shape[1]
    o_ref[...] = x_ref[...] + _dot(ya_ref[...], w_ref[0:ka, :]) + _dot(ys_ref[...], w_ref[ka:, :])


def _out_proj(x, ya, ys, w_out, l, *, tm=512, tn=1024):
    T, D = x.shape
    Ka, Ks = ya.shape[1], ys.shape[1]
    return pl.pallas_call(
        _outproj_body,
        grid=(T // tm, D // tn),
        in_specs=[
            pl.BlockSpec((tm, tn), lambda i, j: (i, j)),
            pl.BlockSpec((tm, Ka), lambda i, j: (i, 0)),
            pl.BlockSpec((tm, Ks), lambda i, j: (i, 0)),
            pl.BlockSpec((None, Ka + Ks, tn), lambda i, j: (l, 0, j)),
        ],
        out_specs=pl.BlockSpec((tm, tn), lambda i, j: (i, j)),
        out_shape=jax.ShapeDtypeStruct((T, D), F32),
        compiler_params=_cparams(("parallel", "arbitrary")),
        name="out_proj",
    )(x, ya, ys, w_out)


def _memkv_body(m_ref, nw_ref, wk_ref, wv_ref, kn_ref, k_ref, v_ref):
    h = _rms(m_ref[...], nw_ref[...]).astype(BF16)
    k = _dot(h, wk_ref[...])
    v_ref[...] = _dot(h, wv_ref[...]).astype(BF16)
    for hh in range(MEM_HEADS):
        sl = slice(hh * MEM_HD, (hh + 1) * MEM_HD)
        k_ref[:, sl] = _rms(k[:, sl], kn_ref[...]).astype(BF16)


def _mem_kv(mem, nw, wk, wv, kn, l):
    M, D = mem.shape
    Wm = wk.shape[2]
    full = lambda shape: pl.BlockSpec(shape, lambda i: (0,) * len(shape))
    layer = lambda shape: pl.BlockSpec((None,) + shape, lambda i: (l,) + (0,) * len(shape))
    return pl.pallas_call(
        _memkv_body,
        grid=(1,),
        in_specs=[full((M, D)), full((1, D)), layer((D, Wm)), layer((D, Wm)), full((1, MEM_HD))],
        out_specs=[full((M, Wm)), full((M, Wm))],
        out_shape=[jax.ShapeDtypeStruct((M, Wm), BF16)] * 2,
        compiler_params=_cparams(("arbitrary",)),
        name="mem_kv",
    )(mem, nw, wk, wv, kn)


def _memattn_body(x_ref, nw_ref, wq_ref, qn_ref, k_ref, v_ref, wo_ref, o_ref):
    x = x_ref[...]
    h = _rms(x, nw_ref[...]).astype(BF16)
    q = _dot(h, wq_ref[...])
    scale = MEM_HD ** -0.5
    outs = []
    for hh in range(MEM_HEADS):
        sl = slice(hh * MEM_HD, (hh + 1) * MEM_HD)
        qh = _rms(q[:, sl], qn_ref[...]).astype(BF16)
        s = _dot_nt(qh, k_ref[:, sl]) * scale
        e = jnp.exp(s - jnp.max(s, axis=-1, keepdims=True))
        p = e / jnp.sum(e, axis=-1, keepdims=True)
        outs.append(_dot(p.astype(BF16), v_ref[:, sl]))
    o = jnp.concatenate(outs, axis=1).astype(BF16)
    o_ref[...] = x + _dot(o, wo_ref[...])


def _mem_attn(x, nw, wq, qn, k, v, wo, l, *, seq, mem_len, tm=512):
    T, D = x.shape
    Wm = wq.shape[2]
    per_b = seq // tm
    return pl.pallas_call(
        _memattn_body,
        grid=(T // tm,),
        in_specs=[
            pl.BlockSpec((tm, D), lambda i: (i, 0)),
            pl.BlockSpec((1, D), lambda i: (0, 0)),
            pl.BlockSpec((None, D, Wm), lambda i: (l, 0, 0)),
            pl.BlockSpec((1, MEM_HD), lambda i: (0, 0)),
            pl.BlockSpec((mem_len, Wm), lambda i: (i // per_b, 0)),
            pl.BlockSpec((mem_len, Wm), lambda i: (i // per_b, 0)),
            pl.BlockSpec((None, Wm, D), lambda i: (l, 0, 0)),
        ],
        out_specs=pl.BlockSpec((tm, D), lambda i: (i, 0)),
        out_shape=jax.ShapeDtypeStruct((T, D), F32),
        compiler_params=_cparams(("parallel",)),
        name="mem_attn",
    )(x, nw, wq, qn, k, v, wo)


def _pad_lanes(v):
    return jnp.pad(v.astype(F32), (0, LANES - v.shape[0])).reshape(1, LANES)


def kernel(x, mem, ff1_norm, ff1_w_gu, ff1_w_down, mix_norm, w_in, q_norm, k_norm, conv_w, conv_b,
           dt_bias, a_log, d_skip, ssd_norm, w_out, xmem_norm, mem_norm, mem_wq, mem_wk, mem_wv,
           mem_q_norm, mem_k_norm, mem_wo, ff2_norm, ff2_w_gu, ff2_w_down):
    B, S, D = x.shape
    M = mem.shape[1]
    depth = w_in.shape[0]
    n_dt = SSD_GROUPS * SSD_HPG
    row = lambda v: v.reshape(1, -1)

    slopes = jnp.exp2(-8.0 * jnp.arange(1, ATTN_HEADS + 1, dtype=F32) / ATTN_HEADS)
    slopes = jnp.broadcast_to(slopes[:, None, None], (ATTN_HEADS, 1, LANES))
    bf = _cast_bf16
    ff1_w_gu, ff1_w_down, ff2_w_gu, ff2_w_down = bf(ff1_w_gu), bf(ff1_w_down), bf(ff2_w_gu), bf(ff2_w_down)
    w_in = bf(w_in, pad_cols=LANES - n_dt)
    w_out = bf(w_out)
    mem_wq, mem_wk, mem_wv, mem_wo = bf(mem_wq), bf(mem_wk), bf(mem_wv), bf(mem_wo)

    xf = x.reshape(B * S, D)
    memf = mem.reshape(B * M, D)
    for l in range(depth):
        xf = _ffn(xf, row(ff1_norm[l]), ff1_w_gu, ff1_w_down, l)

        p_main, dt_raw = _in_proj(xf, row(mix_norm[l]), w_in, l)
        y_attn = _moba(p_main, row(q_norm[l]), row(k_norm[l]), slopes, batch=B, seq=S)
        dts, acss, acst = _ssd_dt(dt_raw, _pad_lanes(dt_bias[l]), _pad_lanes(a_log[l]))
        y_ssd = _ssd(p_main, dts, acss, acst, conv_w[l], row(conv_b[l]),
                     row(jnp.repeat(d_skip[l], SSD_HD)), row(ssd_norm[l]), batch=B, seq=S)
        xf = _out_proj(xf, y_attn, y_ssd, w_out, l)

        k_mem, v_mem = _mem_kv(memf, row(mem_norm[l]), mem_wk, mem_wv, row(mem_k_norm[l]), l)
        xf = _mem_attn(xf, row(xmem_norm[l]), mem_wq, row(mem_q_norm[l]), k_mem, v_mem, mem_wo, l,
                       seq=S, mem_len=M)

        xf = _ffn(xf, row(ff2_norm[l]), ff2_w_gu, ff2_w_down, l)
    return xf.reshape(B, S, D)
```

```python
import functools

import jax
import jax.numpy as jnp
from jax import lax
from jax.experimental import pallas as pl
from jax.experimental.pallas import tpu as pltpu

F32 = jnp.float32
BF16 = jnp.bfloat16
EPS = 1e-6
NEG = -1e30
LOG2E = 1.4426950408889634

LANES = 128
SUBLANES = 8
VMEM_LIMIT = 56 * 1024 * 1024

ATTN_HEADS = 8
ATTN_HD = 128
MOBA_BLOCK = 256
MOBA_TOPK = 3
SSD_HD = 64
SSD_GROUPS = 8
SSD_HPG = 6
SSD_STATE = 128
SSD_CONV = 4
SSD_CHUNK = 128
MEM_HEADS = 4
MEM_HD = 128


def _cparams(sem):
    return pltpu.CompilerParams(dimension_semantics=sem, vmem_limit_bytes=VMEM_LIMIT)


def _rms(x, w):
    return x * lax.rsqrt(jnp.mean(x * x, axis=-1, keepdims=True) + EPS) * w


def _dot(a, b):
    return jnp.dot(a, b, preferred_element_type=F32)


def _dot_nt(a, b, precision=None):
    return lax.dot_general(a, b, (((1,), (1,)), ((), ())), preferred_element_type=F32,
                           precision=precision)


CAST_BLOCK_BYTES = 12 * 1024 * 1024


def _cast_body(w_ref, o_ref):
    n = w_ref.shape[1]
    o_ref[:, 0:n] = w_ref[...].astype(BF16)
    if o_ref.shape[1] > n:
        o_ref[:, n:] = jnp.zeros((o_ref.shape[0], o_ref.shape[1] - n), BF16)


def _cast_bf16(w, pad_cols=0):
    nl, R, C = w.shape
    tr = R
    while tr * C * 4 > CAST_BLOCK_BYTES and tr % 32 == 0:
        tr //= 2
    return pl.pallas_call(
        _cast_body,
        grid=(nl, R // tr),
        in_specs=[pl.BlockSpec((None, tr, C), lambda l, i: (l, i, 0))],
        out_specs=pl.BlockSpec((None, tr, C + pad_cols), lambda l, i: (l, i, 0)),
        out_shape=jax.ShapeDtypeStruct((nl, R, C + pad_cols), BF16),
        compiler_params=_cparams(("parallel", "parallel")),
        name="cast_bf16",
    )(w)


def _ffn_body(x_ref, nw_ref, wg_ref, wu_ref, wd_ref, o_ref, h_ref):
    j = pl.program_id(1)

    @pl.when(j == 0)
    def _():
        x = x_ref[...]
        h_ref[...] = _rms(x, nw_ref[...]).astype(BF16)
        o_ref[...] = x

    h = h_ref[...]
    g = _dot(h, wg_ref[...])
    u = _dot(h, wu_ref[...])
    a = (0.5 * (g * jax.nn.sigmoid(g)) * u).astype(BF16)
    o_ref[...] += _dot(a, wd_ref[...])


def _ffn(x, nw, w_gu, w_down, l, *, tm=1024, tf=512):
    T, D = x.shape
    F = w_down.shape[1]
    nf = F // tf
    return pl.pallas_call(
        _ffn_body,
        grid=(T // tm, nf),
        in_specs=[
            pl.BlockSpec((tm, D), lambda i, j: (i, 0)),
            pl.BlockSpec((1, D), lambda i, j: (0, 0)),
            pl.BlockSpec((None, D, tf), lambda i, j: (l, 0, j)),
            pl.BlockSpec((None, D, tf), lambda i, j: (l, 0, j + nf)),
            pl.BlockSpec((None, tf, D), lambda i, j: (l, j, 0)),
        ],
        out_specs=pl.BlockSpec((tm, D), lambda i, j: (i, 0)),
        out_shape=jax.ShapeDtypeStruct((T, D), F32),
        scratch_shapes=[pltpu.VMEM((tm, D), BF16)],
        compiler_params=_cparams(("parallel", "arbitrary")),
        name="ffn",
    )(x, nw, w_gu, w_gu, w_down)


def _inproj_body(x_ref, nw_ref, w_ref, wdt_ref, p_ref, dt_ref, h_ref):
    j = pl.program_id(1)

    @pl.when(j == 0)
    def _():
        h = _rms(x_ref[...], nw_ref[...]).astype(BF16)
        h_ref[...] = h
        dt_ref[...] = _dot(h, wdt_ref[...])

    p_ref[...] = _dot(h_ref[...], w_ref[...])


def _in_proj(x, nw, w_in, l, *, tm=1024, tn=1024):
    T, D = x.shape
    N = w_in.shape[2] - LANES
    assert N % tn == 0
    return pl.pallas_call(
        _inproj_body,
        grid=(T // tm, N // tn),
        in_specs=[
            pl.BlockSpec((tm, D), lambda i, j: (i, 0)),
            pl.BlockSpec((1, D), lambda i, j: (0, 0)),
            pl.BlockSpec((None, D, tn), lambda i, j: (l, 0, j)),
            pl.BlockSpec((None, D, LANES), lambda i, j: (l, 0, N // LANES)),
        ],
        out_specs=[
            pl.BlockSpec((tm, tn), lambda i, j: (i, j)),
            pl.BlockSpec((tm, LANES), lambda i, j: (i, 0)),
        ],
        out_shape=[jax.ShapeDtypeStruct((T, N), F32), jax.ShapeDtypeStruct((T, LANES), F32)],
        scratch_shapes=[pltpu.VMEM((tm, D), BF16)],
        compiler_params=_cparams(("parallel", "arbitrary")),
        name="in_proj",
    )(x, nw, w_in, w_in)


MOBA_EXT = LANES
MOBA_SEL0 = 16
MOBA_VARIANT = 4
MOBA_QBLOCKS = 2


def _split3(x):
    hi = x.astype(BF16).astype(F32)
    r = x - hi
    mid = r.astype(BF16).astype(F32)
    lo = (r - mid).astype(BF16).astype(F32)
    return hi, mid, lo


def _rows16(rows, width):
    r_i = lax.broadcasted_iota(jnp.int32, (2 * SUBLANES, width), 0)
    out = jnp.zeros((2 * SUBLANES, width), F32)
    for k, row in enumerate(rows):
        out = jnp.where(r_i == k, row, out)
    return out


def _moba_body(q_ref, k_ref, v_ref, qn_ref, kn_ref, slope_ref, o_ref,
               kx_ref, vt_ref, kmean_ref, qx_ref, *, nb):
    qi = pl.program_id(2)
    blk = MOBA_BLOCK
    seq = nb * blk
    scale = ATTN_HD ** -0.5
    slope = slope_ref[0:1, 0:1]

    @pl.when(qi == 0)
    def _():
        kf = _rms(k_ref[...], kn_ref[...])
        kx_ref[:, 0:ATTN_HD] = kf.astype(BF16)
        for n in range(nb):
            kmean_ref[n:n + 1, :] = jnp.mean(kf[n * blk:(n + 1) * blk], axis=0, keepdims=True)
        vt_ref[...] = v_ref[...].T.astype(BF16)
        pos = lax.broadcasted_iota(jnp.int32, (seq, MOBA_EXT), 0)
        c = lax.broadcasted_iota(jnp.int32, (seq, MOBA_EXT), 1)
        n_of = lax.shift_right_logical(pos, blk.bit_length() - 1)
        j_of = pos & (blk - 1)
        ext = jnp.where(c < 3, 1, jnp.where(c < 6, n_of, jnp.where(c < 9, j_of, 0)))
        ext = jnp.where(c - MOBA_SEL0 == n_of, 1, ext)
        kx_ref[:, ATTN_HD:] = ext.astype(F32).astype(BF16)

        qn = _rms(q_ref[...], qn_ref[...])
        gate = _dot_nt(kmean_ref[...], qn, precision=lax.Precision.HIGHEST)
        n_iota = lax.broadcasted_iota(jnp.int32, gate.shape, 0).astype(F32)
        t_i = lax.broadcasted_iota(jnp.int32, (1, seq), 1)
        past_blk = n_iota < lax.shift_right_logical(t_i, blk.bit_length() - 1).astype(F32)
        gate = jnp.where(past_blk, gate, -jnp.inf)
        selb = jnp.full(gate.shape, NEG, F32)
        for _ in range(MOBA_TOPK):
            best = jnp.max(gate, axis=0, keepdims=True)
            first = jnp.min(jnp.where(gate == best, n_iota, float(nb)), axis=0, keepdims=True)
            pick = n_iota == first
            selb = jnp.where(pick, 0.0, selb)
            gate = jnp.where(pick, -jnp.inf, gate)
        selb = jnp.where(past_blk, selb, NEG)
        slope2 = slope * LOG2E
        ones = jnp.ones((1, seq), F32)
        bias_rows = (list(_split3(-slope2 * t_i.astype(F32))) + [v * ones for v in _split3(slope2 * float(blk))]
                     + [v * ones for v in _split3(slope2)])
        qx_ref[0:ATTN_HD, :] = (qn * (scale * LOG2E)).T.astype(BF16)
        qx_ref[ATTN_HD:ATTN_HD + MOBA_SEL0, :] = _rows16(bias_rows, seq).astype(BF16)
        qx_ref[ATTN_HD + MOBA_SEL0:ATTN_HD + MOBA_SEL0 + nb, :] = selb.astype(BF16)
        qx_ref[ATTN_HD + MOBA_SEL0 + nb:, :] = jnp.zeros((MOBA_EXT - MOBA_SEL0 - nb, seq), BF16)

    qw = MOBA_QBLOCKS * blk
    qcols = pl.ds(pl.multiple_of(qi * qw, qw), qw)
    qx_past = qx_ref[:, qcols]
    qx_own = jnp.concatenate([qx_ref[0:ATTN_HD + MOBA_SEL0, qcols],
                              jnp.zeros((MOBA_EXT - MOBA_SEL0, qw), BF16)], axis=0)

    key_i = lax.broadcasted_iota(jnp.int32, (blk, blk), 0)
    qry_i = lax.broadcasted_iota(jnp.int32, (blk, blk), 1)
    lanes = lambda a, u: a[:, u * blk:(u + 1) * blk]
    row_cat = lambda parts: jnp.concatenate(parts, axis=1)

    def attend(n_past):
        n = n_past * blk
        owns = [pl.ds(pl.multiple_of((qi * MOBA_QBLOCKS + u) * blk, blk), blk) for u in range(MOBA_QBLOCKS)]
        s_own = [jnp.where(qry_i >= key_i, _dot(kx_ref[owns[u], :], lanes(qx_own, u)), NEG)
                 for u in range(MOBA_QBLOCKS)]
        s = _dot(kx_ref[0:n, :], qx_past)
        m = jnp.maximum(jnp.max(s, axis=0, keepdims=True),
                        row_cat([jnp.max(so, axis=0, keepdims=True) for so in s_own]))
        p = jnp.exp2(s - m)
        p_own = [jnp.exp2(s_own[u] - lanes(m, u)) for u in range(MOBA_QBLOCKS)]
        l = jnp.sum(p, axis=0, keepdims=True) + row_cat([jnp.sum(po, axis=0, keepdims=True) for po in p_own])
        acc = _dot(vt_ref[:, 0:n], p.astype(BF16)) + row_cat(
            [_dot(vt_ref[:, owns[u]], p_own[u].astype(BF16)) for u in range(MOBA_QBLOCKS)])
        o_ref[...] = (acc / l).T.astype(o_ref.dtype)

    per = MOBA_VARIANT // MOBA_QBLOCKS
    for v in range(nb // MOBA_VARIANT):
        pl.when((qi >= v * per) & (qi < (v + 1) * per))(
            functools.partial(attend, (v + 1) * MOBA_VARIANT - 1))


def _moba(p_main, q_norm, k_norm, slopes, *, batch, seq):
    T = p_main.shape[0]
    blk = MOBA_BLOCK
    nb = seq // blk
    assert seq % blk == 0 and nb % MOBA_VARIANT == 0 and nb <= MOBA_EXT - MOBA_SEL0
    assert MOBA_VARIANT % MOBA_QBLOCKS == 0
    H = ATTN_HEADS
    qw = MOBA_QBLOCKS * blk
    nq = seq // qw
    body = functools.partial(_moba_body, nb=nb)
    return pl.pallas_call(
        body,
        grid=(batch, H, nq),
        in_specs=[
            pl.BlockSpec((seq, ATTN_HD), lambda b, h, i: (b, h)),
            pl.BlockSpec((seq, ATTN_HD), lambda b, h, i: (b, H + h)),
            pl.BlockSpec((seq, ATTN_HD), lambda b, h, i: (b, 2 * H + h)),
            pl.BlockSpec((1, ATTN_HD), lambda b, h, i: (0, 0)),
            pl.BlockSpec((1, ATTN_HD), lambda b, h, i: (0, 0)),
            pl.BlockSpec((None, 1, LANES), lambda b, h, i: (h, 0, 0)),
        ],
        out_specs=pl.BlockSpec((qw, ATTN_HD), lambda b, h, i: (b * nq + i, h)),
        out_shape=jax.ShapeDtypeStruct((T, H * ATTN_HD), BF16),
        scratch_shapes=[
            pltpu.VMEM((seq, ATTN_HD + MOBA_EXT), BF16),
            pltpu.VMEM((ATTN_HD, seq), BF16),
            pltpu.VMEM((nb, ATTN_HD), F32),
            pltpu.VMEM((ATTN_HD + MOBA_EXT, seq), BF16),
        ],
        compiler_params=_cparams(("parallel", "parallel", "arbitrary")),
        name="moba",
    )(p_main, p_main, p_main, q_norm, k_norm, slopes)


def _softplus(x):
    return jnp.maximum(x, 0.0) + jnp.log1p(jnp.exp(-jnp.abs(x)))


def _split_hi_lo(x):
    hi = x.astype(BF16)
    lo = (x - hi.astype(F32)).astype(BF16)
    return jnp.concatenate([hi, lo], axis=1)


def _ssd_dt_body(dt_ref, dtb_ref, alog_ref, dts_ref, acss_ref, acst_ref, *, rows):
    L = SSD_CHUNK
    li = lax.broadcasted_iota(jnp.int32, (L, L), 0)
    si = lax.broadcasted_iota(jnp.int32, (L, L), 1)
    tril = (li >= si).astype(F32)
    a2_row = -jnp.exp(alog_ref[...]) * LOG2E
    for c in range(rows // L):
        rs = slice(c * L, (c + 1) * L)
        dtv = _softplus(dt_ref[rs, :] + dtb_ref[...])
        acs = jnp.dot(tril, dtv * a2_row, preferred_element_type=F32,
                      precision=lax.Precision.HIGHEST)
        dts_ref[rs, :] = _split_hi_lo(dtv)
        acss_ref[rs, :] = _split_hi_lo(acs)
        acst_ref[c] = acs.T


def _ssd_dt(dt_raw, dt_bias_p, a_log_p, *, rows=1024):
    T = dt_raw.shape[0]
    L = SSD_CHUNK
    body = functools.partial(_ssd_dt_body, rows=rows)
    return pl.pallas_call(
        body,
        grid=(T // rows,),
        in_specs=[
            pl.BlockSpec((rows, LANES), lambda i: (i, 0)),
            pl.BlockSpec((1, LANES), lambda i: (0, 0)),
            pl.BlockSpec((1, LANES), lambda i: (0, 0)),
        ],
        out_specs=[
            pl.BlockSpec((rows, 2 * LANES), lambda i: (i, 0)),
            pl.BlockSpec((rows, 2 * LANES), lambda i: (i, 0)),
            pl.BlockSpec((rows // L, LANES, L), lambda i: (i, 0, 0)),
        ],
        out_shape=[jax.ShapeDtypeStruct((T, 2 * LANES), BF16), jax.ShapeDtypeStruct((T, 2 * LANES), BF16),
                   jax.ShapeDtypeStruct((T // L, LANES, L), F32)],
        compiler_params=_cparams(("parallel",)),
        name="ssd_dt",
    )(dt_raw, dt_bias_p, a_log_p)


def _ssd_body(xs_ref, bm_ref, cm_ref, xs_h_ref, bm_h_ref, cm_h_ref, z_ref, dts_ref, acss_ref, acst_ref,
              wx_ref, wb_ref, wc_ref, bx_ref, bb_ref, bc_ref, dsk_ref, nw_ref, o_ref,
              ux_ref, ub_ref, uc_ref, st_ref, *, rows):
    g = pl.program_id(1)
    r = pl.program_id(2)
    L = SSD_CHUNK
    P = SSD_HD
    W = SSD_HPG * P
    HALO = SUBLANES

    @pl.when(r == 0)
    def _():
        st_ref[...] = jnp.zeros_like(st_ref)

    first = r == 0

    def conv(u_ref, halo_ref, buf_ref, w_ref, b_ref):
        halo = halo_ref[...]
        buf_ref[0:HALO, :] = jnp.where(first, jnp.zeros_like(halo), halo)
        buf_ref[HALO:, :] = u_ref[...]
        acc = b_ref[...] + w_ref[SSD_CONV - 1:SSD_CONV, :] * u_ref[...]
        for k in range(SSD_CONV - 1):
            shift = SSD_CONV - 1 - k
            acc = acc + w_ref[k:k + 1, :] * buf_ref[HALO - shift:HALO - shift + rows, :]
        buf_ref[HALO:, :] = acc * jax.nn.sigmoid(acc)

    conv(xs_ref, xs_h_ref, ux_ref, wx_ref, bx_ref)
    conv(bm_ref, bm_h_ref, ub_ref, wb_ref, bb_ref)
    conv(cm_ref, cm_h_ref, uc_ref, wc_ref, bc_ref)

    k_i = lax.broadcasted_iota(jnp.int32, (2 * LANES, W), 0)
    c_i = lax.broadcasted_iota(jnp.int32, (2 * LANES, W), 1)
    head = g * SSD_HPG + lax.shift_right_logical(c_i, P.bit_length() - 1)
    expand = ((k_i == head) | (k_i == head + LANES)).astype(BF16)

    li = lax.broadcasted_iota(jnp.int32, (L, L), 0)
    si = lax.broadcasted_iota(jnp.int32, (L, L), 1)
    causal = li >= si
    lane = lax.broadcasted_iota(jnp.int32, (L, LANES), 1)

    for c in range(rows // L):
        r0 = c * L
        rs = slice(HALO + r0, HALO + r0 + L)
        dt_x = _dot(dts_ref[r0:r0 + L, :], expand)
        acs_x = _dot(acss_ref[r0:r0 + L, :], expand)
        acs_last = acs_x[L - 1:L, :]

        xs = ux_ref[rs, :]
        bmat = ub_ref[rs, :]
        cmat = uc_ref[rs, :]
        x = xs * dt_x
        xb = x.astype(BF16)
        cb16 = cmat.astype(BF16)
        cb = jnp.where(causal, _dot_nt(cb16, bmat.astype(BF16)), 0.0)

        tiles = []
        for m in range(W // LANES):
            res = []
            for jj in range(2):
                j = 2 * m + jj
                col = acs_x[:, j * P:j * P + 1]
                row = acst_ref[c, pl.ds(g * SSD_HPG + j, 1), :]
                dec = jnp.exp2(jnp.minimum(col - row, 0.0))
                sc = (cb * dec).astype(BF16)
                res.append(_dot(sc, xb[:, m * LANES:(m + 1) * LANES]))
            tiles.append(jnp.where(lane < P, res[0], res[1]))
        y = jnp.concatenate(tiles, axis=1)

        st = st_ref[...]
        y = y + _dot(cb16, st.astype(BF16)) * jnp.exp2(acs_x)
        xd = (x * jnp.exp2(acs_last - acs_x)).astype(BF16)
        st_ref[...] = st * jnp.exp2(acs_last) + _dot(bmat.T.astype(BF16), xd)

        y = y + dsk_ref[...] * xs
        z = z_ref[r0:r0 + L, :]
        gt = y * (z * jax.nn.sigmoid(z))
        gn = gt * lax.rsqrt(jnp.mean(gt * gt, axis=-1, keepdims=True) + EPS)
        o_ref[r0:r0 + L, :] = (gn * nw_ref[...]).astype(o_ref.dtype)


def _ssd(p_main, dts, acss, acst, conv_w, conv_b, d_skip_x, ssd_norm, *, batch, seq, rows=512):
    T = p_main.shape[0]
    G = SSD_GROUPS
    N = SSD_STATE
    W = SSD_HPG * SSD_HD
    nr = seq // rows
    hb = rows // SUBLANES
    z0, x0 = 3 * ATTN_HEADS * ATTN_HD // W, (3 * ATTN_HEADS * ATTN_HD + G * W) // W
    b0 = (3 * ATTN_HEADS * ATTN_HD + 2 * G * W) // N
    c0 = b0 + G
    cwb0, cwc0 = G * W // N, G * W // N + G

    def rowblk(b, g, r):
        return b * nr + r

    def halo(b, g, r):
        return jnp.maximum((b * nr + r) * hb - 1, 0)

    body = functools.partial(_ssd_body, rows=rows)
    return pl.pallas_call(
        body,
        grid=(batch, G, nr),
        in_specs=[
            pl.BlockSpec((rows, W), lambda b, g, r: (rowblk(b, g, r), x0 + g)),
            pl.BlockSpec((rows, N), lambda b, g, r: (rowblk(b, g, r), b0 + g)),
            pl.BlockSpec((rows, N), lambda b, g, r: (rowblk(b, g, r), c0 + g)),
            pl.BlockSpec((SUBLANES, W), lambda b, g, r: (halo(b, g, r), x0 + g)),
            pl.BlockSpec((SUBLANES, N), lambda b, g, r: (halo(b, g, r), b0 + g)),
            pl.BlockSpec((SUBLANES, N), lambda b, g, r: (halo(b, g, r), c0 + g)),
            pl.BlockSpec((rows, W), lambda b, g, r: (rowblk(b, g, r), z0 + g)),
            pl.BlockSpec((rows, 2 * LANES), lambda b, g, r: (rowblk(b, g, r), 0)),
            pl.BlockSpec((rows, 2 * LANES), lambda b, g, r: (rowblk(b, g, r), 0)),
            pl.BlockSpec((rows // SSD_CHUNK, LANES, SSD_CHUNK), lambda b, g, r: (rowblk(b, g, r), 0, 0)),
            pl.BlockSpec((SSD_CONV, W), lambda b, g, r: (0, g)),
            pl.BlockSpec((SSD_CONV, N), lambda b, g, r: (0, cwb0 + g)),
            pl.BlockSpec((SSD_CONV, N), lambda b, g, r: (0, cwc0 + g)),
            pl.BlockSpec((1, W), lambda b, g, r: (0, g)),
            pl.BlockSpec((1, N), lambda b, g, r: (0, cwb0 + g)),
            pl.BlockSpec((1, N), lambda b, g, r: (0, cwc0 + g)),
            pl.BlockSpec((1, W), lambda b, g, r: (0, g)),
            pl.BlockSpec((1, W), lambda b, g, r: (0, g)),
        ],
        out_specs=pl.BlockSpec((rows, W), lambda b, g, r: (rowblk(b, g, r), g)),
        out_shape=jax.ShapeDtypeStruct((T, G * W), BF16),
        scratch_shapes=[
            pltpu.VMEM((SUBLANES + rows, W), F32),
            pltpu.VMEM((SUBLANES + rows, N), F32),
            pltpu.VMEM((SUBLANES + rows, N), F32),
            pltpu.VMEM((N, W), F32),
        ],
        compiler_params=_cparams(("parallel", "parallel", "arbitrary")),
        name="ssd",
    )(p_main, p_main, p_main, p_main, p_main, p_main, p_main, dts, acss, acst,
      conv_w, conv_w, conv_w, conv_b, conv_b, conv_b, d_skip_x, ssd_norm)


def _outproj_body(x_ref, ya_ref, ys_ref, w_ref, o_ref):
    ka = ya_ref.shape[1]
    o_ref[...] = x_ref[...] + _dot(ya_ref[...], w_ref[0:ka, :]) + _dot(ys_ref[...], w_ref[ka:, :])


def _out_proj(x, ya, ys, w_out, l, *, tm=512, tn=1024):
    T, D = x.shape
    Ka, Ks = ya.shape[1], ys.shape[1]
    return pl.pallas_call(
        _outproj_body,
        grid=(D // tn, T // tm),
        in_specs=[
            pl.BlockSpec((tm, tn), lambda j, i: (i, j)),
            pl.BlockSpec((tm, Ka), lambda j, i: (i, 0)),
            pl.BlockSpec((tm, Ks), lambda j, i: (i, 0)),
            pl.BlockSpec((None, Ka + Ks, tn), lambda j, i: (l, 0, j)),
        ],
        out_specs=pl.BlockSpec((tm, tn), lambda j, i: (i, j)),
        out_shape=jax.ShapeDtypeStruct((T, D), F32),
        compiler_params=_cparams(("parallel", "parallel")),
        name="out_proj",
    )(x, ya, ys, w_out)


def _memkv_body(m_ref, nw_ref, wk_ref, wv_ref, kn_ref, k_ref, v_ref):
    h = _rms(m_ref[...], nw_ref[...]).astype(BF16)
    k = _dot(h, wk_ref[...])
    v_ref[...] = _dot(h, wv_ref[...]).astype(BF16)
    for hh in range(MEM_HEADS):
        sl = slice(hh * MEM_HD, (hh + 1) * MEM_HD)
        k_ref[:, sl] = _rms(k[:, sl], kn_ref[...]).astype(BF16)


def _mem_kv(mem, nw, wk, wv, kn, l):
    M, D = mem.shape
    Wm = wk.shape[2]
    full = lambda shape: pl.BlockSpec(shape, lambda i: (0,) * len(shape))
    layer = lambda shape: pl.BlockSpec((None,) + shape, lambda i: (l,) + (0,) * len(shape))
    return pl.pallas_call(
        _memkv_body,
        grid=(1,),
        in_specs=[full((M, D)), full((1, D)), layer((D, Wm)), layer((D, Wm)), full((1, MEM_HD))],
        out_specs=[full((M, Wm)), full((M, Wm))],
        out_shape=[jax.ShapeDtypeStruct((M, Wm), BF16)] * 2,
        compiler_params=_cparams(("arbitrary",)),
        name="mem_kv",
    )(mem, nw, wk, wv, kn)


def _memattn_body(x_ref, nw_ref, wq_ref, qn_ref, k_ref, v_ref, wo_ref, o_ref):
    x = x_ref[...]
    h = _rms(x, nw_ref[...]).astype(BF16)
    q = _dot(h, wq_ref[...])
    scale = MEM_HD ** -0.5
    outs = []
    for hh in range(MEM_HEADS):
        sl = slice(hh * MEM_HD, (hh + 1) * MEM_HD)
        qh = _rms(q[:, sl], qn_ref[...]).astype(BF16)
        s = _dot_nt(qh, k_ref[:, sl]) * scale
        e = jnp.exp(s - jnp.max(s, axis=-1, keepdims=True))
        p = e / jnp.sum(e, axis=-1, keepdims=True)
        outs.append(_dot(p.astype(BF16), v_ref[:, sl]))
    o = jnp.concatenate(outs, axis=1).astype(BF16)
    o_ref[...] = x + _dot(o, wo_ref[...])


def _mem_attn(x, nw, wq, qn, k, v, wo, l, *, seq, mem_len, tm=512):
    T, D = x.shape
    Wm = wq.shape[2]
    per_b = seq // tm
    return pl.pallas_call(
        _memattn_body,
        grid=(T // tm,),
        in_specs=[
            pl.BlockSpec((tm, D), lambda i: (i, 0)),
            pl.BlockSpec((1, D), lambda i: (0, 0)),
            pl.BlockSpec((None, D, Wm), lambda i: (l, 0, 0)),
            pl.BlockSpec((1, MEM_HD), lambda i: (0, 0)),
            pl.BlockSpec((mem_len, Wm), lambda i: (i // per_b, 0)),
            pl.BlockSpec((mem_len, Wm), lambda i: (i // per_b, 0)),
            pl.BlockSpec((None, Wm, D), lambda i: (l, 0, 0)),
        ],
        out_specs=pl.BlockSpec((tm, D), lambda i: (i, 0)),
        out_shape=jax.ShapeDtypeStruct((T, D), F32),
        compiler_params=_cparams(("parallel",)),
        name="mem_attn",
    )(x, nw, wq, qn, k, v, wo)


def _pad_lanes(v):
    return jnp.pad(v.astype(F32), (0, LANES - v.shape[0])).reshape(1, LANES)


def kernel(x, mem, ff1_norm, ff1_w_gu, ff1_w_down, mix_norm, w_in, q_norm, k_norm, conv_w, conv_b,
           dt_bias, a_log, d_skip, ssd_norm, w_out, xmem_norm, mem_norm, mem_wq, mem_wk, mem_wv,
           mem_q_norm, mem_k_norm, mem_wo, ff2_norm, ff2_w_gu, ff2_w_down):
    B, S, D = x.shape
    M = mem.shape[1]
    depth = w_in.shape[0]
    n_dt = SSD_GROUPS * SSD_HPG
    row = lambda v: v.reshape(1, -1)

    slopes = jnp.exp2(-8.0 * jnp.arange(1, ATTN_HEADS + 1, dtype=F32) / ATTN_HEADS)
    slopes = jnp.broadcast_to(slopes[:, None, None], (ATTN_HEADS, 1, LANES))
    bf = _cast_bf16
    ff1_w_gu, ff1_w_down, ff2_w_gu, ff2_w_down = bf(ff1_w_gu), bf(ff1_w_down), bf(ff2_w_gu), bf(ff2_w_down)
    w_in = bf(w_in, pad_cols=LANES - n_dt)
    w_out = bf(w_out)
    mem_wq, mem_wk, mem_wv, mem_wo = bf(mem_wq), bf(mem_wk), bf(mem_wv), bf(mem_wo)

    xf = x.reshape(B * S, D)
    memf = mem.reshape(B * M, D)
    for l in range(depth):
        xf = _ffn(xf, row(ff1_norm[l]), ff1_w_gu, ff1_w_down, l)

        p_main, dt_raw = _in_proj(xf, row(mix_norm[l]), w_in, l)
        y_attn = _moba(p_main, row(q_norm[l]), row(k_norm[l]), slopes, batch=B, seq=S)
        dts, acss, acst = _ssd_dt(dt_raw, _pad_lanes(dt_bias[l]), _pad_lanes(a_log[l]))
        y_ssd = _ssd(p_main, dts, acss, acst, conv_w[l], row(conv_b[l]),
                     row(jnp.repeat(d_skip[l], SSD_HD)), row(ssd_norm[l]), batch=B, seq=S)
        xf = _out_proj(xf, y_attn, y_ssd, w_out, l)

        k_mem, v_mem = _mem_kv(memf, row(mem_norm[l]), mem_wk, mem_wv, row(mem_k_norm[l]), l)
        xf = _mem_attn(xf, row(xmem_norm[l]), mem_wq, row(mem_q_norm[l]), k_mem, v_mem, mem_wo, l,
                       seq=S, mem_len=M)

        xf = _ffn(xf, row(ff2_norm[l]), ff2_w_gu, ff2_w_down, l)
    return xf.reshape(B, S, D)
```

```python
import functools

import jax
import jax.numpy as jnp
from jax import lax
from jax.experimental import pallas as pl
from jax.experimental.pallas import tpu as pltpu

F32 = jnp.float32
BF16 = jnp.bfloat16
EPS = 1e-6
NEG = -1e30
LOG2E = 1.4426950408889634

LANES = 128
SUBLANES = 8
VMEM_LIMIT = 56 * 1024 * 1024

ATTN_HEADS = 8
ATTN_HD = 128
MOBA_BLOCK = 256
MOBA_TOPK = 3
SSD_HD = 64
SSD_GROUPS = 8
SSD_HPG = 6
SSD_STATE = 128
SSD_CONV = 4
SSD_CHUNK = 128
MEM_HEADS = 4
MEM_HD = 128


def _cparams(sem):
    return pltpu.CompilerParams(dimension_semantics=sem, vmem_limit_bytes=VMEM_LIMIT)


def _rms(x, w):
    return x * lax.rsqrt(jnp.mean(x * x, axis=-1, keepdims=True) + EPS) * w


def _dot(a, b):
    return jnp.dot(a, b, preferred_element_type=F32)


def _dot_nt(a, b, precision=None):
    return lax.dot_general(a, b, (((1,), (1,)), ((), ())), preferred_element_type=F32,
                           precision=precision)


CAST_BLOCK_BYTES = 12 * 1024 * 1024


def _cast_body(w_ref, o_ref):
    o_ref[...] = w_ref[...].astype(BF16)


def _cast_bf16(w):
    nl, R, C = w.shape
    tr = R
    while tr * C * 4 > CAST_BLOCK_BYTES and tr % 32 == 0:
        tr //= 2
    return pl.pallas_call(
        _cast_body,
        grid=(nl, R // tr),
        in_specs=[pl.BlockSpec((None, tr, C), lambda l, i: (l, i, 0))],
        out_specs=pl.BlockSpec((None, tr, C), lambda l, i: (l, i, 0)),
        out_shape=jax.ShapeDtypeStruct((nl, R, C), BF16),
        compiler_params=_cparams(("parallel", "parallel")),
        name="cast_bf16",
    )(w)


def _ffn_body(x_ref, nw_ref, wg_ref, wu_ref, wd_ref, o_ref, h_ref):
    j = pl.program_id(1)

    @pl.when(j == 0)
    def _():
        x = x_ref[...]
        h_ref[...] = _rms(x, nw_ref[...]).astype(BF16)
        o_ref[...] = x

    h = h_ref[...]
    g = _dot(h, wg_ref[...])
    u = _dot(h, wu_ref[...])
    a = (0.5 * (g * jax.nn.sigmoid(g)) * u).astype(BF16)
    o_ref[...] += _dot(a, wd_ref[...])


def _ffn(x, nw, w_gu, w_down, l, *, tm=1024, tf=512):
    T, D = x.shape
    F = w_down.shape[1]
    nf = F // tf
    return pl.pallas_call(
        _ffn_body,
        grid=(T // tm, nf),
        in_specs=[
            pl.BlockSpec((tm, D), lambda i, j: (i, 0)),
            pl.BlockSpec((1, D), lambda i, j: (0, 0)),
            pl.BlockSpec((None, D, tf), lambda i, j: (l, 0, j)),
            pl.BlockSpec((None, D, tf), lambda i, j: (l, 0, j + nf)),
            pl.BlockSpec((None, tf, D), lambda i, j: (l, j, 0)),
        ],
        out_specs=pl.BlockSpec((tm, D), lambda i, j: (i, 0)),
        out_shape=jax.ShapeDtypeStruct((T, D), F32),
        scratch_shapes=[pltpu.VMEM((tm, D), BF16)],
        compiler_params=_cparams(("parallel", "arbitrary")),
        name="ffn",
    )(x, nw, w_gu, w_gu, w_down)


def _inproj_body(x_ref, nw_ref, w_ref, wdt_ref, p_ref, dt_ref, h_ref):
    j = pl.program_id(1)

    @pl.when(j == 0)
    def _():
        h = _rms(x_ref[...], nw_ref[...]).astype(BF16)
        h_ref[...] = h
        dt_ref[...] = _dot(h, wdt_ref[...])

    p_ref[...] = _dot(h_ref[...], w_ref[...])


def _in_proj(x, nw, w_in, l, *, tm=1024, tn=1024):
    T, D = x.shape
    N = w_in.shape[2] - LANES
    assert N % tn == 0
    return pl.pallas_call(
        _inproj_body,
        grid=(T // tm, N // tn),
        in_specs=[
            pl.BlockSpec((tm, D), lambda i, j: (i, 0)),
            pl.BlockSpec((1, D), lambda i, j: (0, 0)),
            pl.BlockSpec((None, D, tn), lambda i, j: (l, 0, j)),
            pl.BlockSpec((None, D, LANES), lambda i, j: (l, 0, N // LANES)),
        ],
        out_specs=[
            pl.BlockSpec((tm, tn), lambda i, j: (i, j)),
            pl.BlockSpec((tm, LANES), lambda i, j: (i, 0)),
        ],
        out_shape=[jax.ShapeDtypeStruct((T, N), F32), jax.ShapeDtypeStruct((T, LANES), F32)],
        scratch_shapes=[pltpu.VMEM((tm, D), BF16)],
        compiler_params=_cparams(("parallel", "arbitrary")),
        name="in_proj",
    )(x, nw, w_in, w_in)


MOBA_EXT = LANES
MOBA_SEL0 = 16
MOBA_VARIANT = 2
MOBA_QBLOCKS = 2


def _split3(x):
    hi = x.astype(BF16).astype(F32)
    r = x - hi
    mid = r.astype(BF16).astype(F32)
    lo = (r - mid).astype(BF16).astype(F32)
    return hi, mid, lo


def _rows16(rows, width):
    r_i = lax.broadcasted_iota(jnp.int32, (2 * SUBLANES, width), 0)
    out = jnp.zeros((2 * SUBLANES, width), F32)
    for k, row in enumerate(rows):
        out = jnp.where(r_i == k, row, out)
    return out


def _moba_body(q_ref, k_ref, v_ref, qn_ref, kn_ref, slope_ref, o_ref,
               kx_ref, vt_ref, kmean_ref, qx_ref, *, nb):
    qi = pl.program_id(2)
    blk = MOBA_BLOCK
    seq = nb * blk
    scale = ATTN_HD ** -0.5
    slope = slope_ref[0:1, 0:1]

    @pl.when(qi == 0)
    def _():
        kf = _rms(k_ref[...], kn_ref[...])
        kx_ref[:, 0:ATTN_HD] = kf.astype(BF16)
        for n in range(nb):
            kmean_ref[n:n + 1, :] = jnp.mean(kf[n * blk:(n + 1) * blk], axis=0, keepdims=True)
        vt_ref[...] = v_ref[...].T.astype(BF16)
        pos = lax.broadcasted_iota(jnp.int32, (seq, MOBA_EXT), 0)
        c = lax.broadcasted_iota(jnp.int32, (seq, MOBA_EXT), 1)
        n_of = lax.shift_right_logical(pos, blk.bit_length() - 1)
        j_of = pos & (blk - 1)
        ext = jnp.where(c < 3, 1, jnp.where(c < 6, n_of, jnp.where(c < 9, j_of, 0)))
        ext = jnp.where(c - MOBA_SEL0 == n_of, 1, ext)
        kx_ref[:, ATTN_HD:] = ext.astype(F32).astype(BF16)

        qn = _rms(q_ref[...], qn_ref[...])
        gate = _dot_nt(kmean_ref[...], qn, precision=lax.Precision.HIGHEST)
        n_iota = lax.broadcasted_iota(jnp.int32, gate.shape, 0).astype(F32)
        t_i = lax.broadcasted_iota(jnp.int32, (1, seq), 1)
        past_blk = n_iota < lax.shift_right_logical(t_i, blk.bit_length() - 1).astype(F32)
        gate = jnp.where(past_blk, gate, -jnp.inf)
        selb = jnp.full(gate.shape, NEG, F32)
        for _ in range(MOBA_TOPK):
            best = jnp.max(gate, axis=0, keepdims=True)
            first = jnp.min(jnp.where(gate == best, n_iota, float(nb)), axis=0, keepdims=True)
            pick = n_iota == first
            selb = jnp.where(pick, 0.0, selb)
            gate = jnp.where(pick, -jnp.inf, gate)
        selb = jnp.where(past_blk, selb, NEG)
        slope2 = slope * LOG2E
        ones = jnp.ones((1, seq), F32)
        bias_rows = (list(_split3(-slope2 * t_i.astype(F32))) + [v * ones for v in _split3(slope2 * float(blk))]
                     + [v * ones for v in _split3(slope2)])
        qx_ref[0:ATTN_HD, :] = (qn * (scale * LOG2E)).T.astype(BF16)
        qx_ref[ATTN_HD:ATTN_HD + MOBA_SEL0, :] = _rows16(bias_rows, seq).astype(BF16)
        qx_ref[ATTN_HD + MOBA_SEL0:ATTN_HD + MOBA_SEL0 + nb, :] = selb.astype(BF16)
        qx_ref[ATTN_HD + MOBA_SEL0 + nb:, :] = jnp.zeros((MOBA_EXT - MOBA_SEL0 - nb, seq), BF16)

    qw = MOBA_QBLOCKS * blk
    qcols = pl.ds(pl.multiple_of(qi * qw, qw), qw)
    qx_past = qx_ref[:, qcols]
    qx_own = jnp.concatenate([qx_ref[0:ATTN_HD + MOBA_SEL0, qcols],
                              jnp.zeros((MOBA_EXT - MOBA_SEL0, qw), BF16)], axis=0)

    key_i = lax.broadcasted_iota(jnp.int32, (blk, blk), 0)
    qry_i = lax.broadcasted_iota(jnp.int32, (blk, blk), 1)
    lanes = lambda a, u: a[:, u * blk:(u + 1) * blk]
    row_cat = lambda parts: jnp.concatenate(parts, axis=1)

    def attend(n_past):
        n = n_past * blk
        owns = [pl.ds(pl.multiple_of((qi * MOBA_QBLOCKS + u) * blk, blk), blk) for u in range(MOBA_QBLOCKS)]
        s_own = [jnp.where(qry_i >= key_i, _dot(kx_ref[owns[u], :], lanes(qx_own, u)), NEG)
                 for u in range(MOBA_QBLOCKS)]
        s = _dot(kx_ref[0:n, :], qx_past)
        m = jnp.maximum(jnp.max(s, axis=0, keepdims=True),
                        row_cat([jnp.max(so, axis=0, keepdims=True) for so in s_own]))
        p = jnp.exp2(s - m)
        p_own = [jnp.exp2(s_own[u] - lanes(m, u)) for u in range(MOBA_QBLOCKS)]
        l = jnp.sum(p, axis=0, keepdims=True) + row_cat([jnp.sum(po, axis=0, keepdims=True) for po in p_own])
        acc = _dot(vt_ref[:, 0:n], p.astype(BF16)) + row_cat(
            [_dot(vt_ref[:, owns[u]], p_own[u].astype(BF16)) for u in range(MOBA_QBLOCKS)])
        o_ref[...] = (acc / l).T.astype(o_ref.dtype)

    per = MOBA_VARIANT // MOBA_QBLOCKS
    for v in range(nb // MOBA_VARIANT):
        pl.when((qi >= v * per) & (qi < (v + 1) * per))(
            functools.partial(attend, (v + 1) * MOBA_VARIANT - 1))


def _moba(p_main, q_norm, k_norm, slopes, *, batch, seq):
    T = p_main.shape[0]
    blk = MOBA_BLOCK
    nb = seq // blk
    assert seq % blk == 0 and nb % MOBA_VARIANT == 0 and nb <= MOBA_EXT - MOBA_SEL0
    assert MOBA_VARIANT % MOBA_QBLOCKS == 0
    H = ATTN_HEADS
    qw = MOBA_QBLOCKS * blk
    nq = seq // qw
    body = functools.partial(_moba_body, nb=nb)
    return pl.pallas_call(
        body,
        grid=(batch, H, nq),
        in_specs=[
            pl.BlockSpec((seq, ATTN_HD), lambda b, h, i: (b, h)),
            pl.BlockSpec((seq, ATTN_HD), lambda b, h, i: (b, H + h)),
            pl.BlockSpec((seq, ATTN_HD), lambda b, h, i: (b, 2 * H + h)),
            pl.BlockSpec((1, ATTN_HD), lambda b, h, i: (0, 0)),
            pl.BlockSpec((1, ATTN_HD), lambda b, h, i: (0, 0)),
            pl.BlockSpec((None, 1, LANES), lambda b, h, i: (h, 0, 0)),
        ],
        out_specs=pl.BlockSpec((qw, ATTN_HD), lambda b, h, i: (b * nq + i, h)),
        out_shape=jax.ShapeDtypeStruct((T, H * ATTN_HD), BF16),
        scratch_shapes=[
            pltpu.VMEM((seq, ATTN_HD + MOBA_EXT), BF16),
            pltpu.VMEM((ATTN_HD, seq), BF16),
            pltpu.VMEM((nb, ATTN_HD), F32),
            pltpu.VMEM((ATTN_HD + MOBA_EXT, seq), BF16),
        ],
        compiler_params=_cparams(("parallel", "parallel", "arbitrary")),
        name="moba",
    )(p_main, p_main, p_main, q_norm, k_norm, slopes)


def _softplus(x):
    return jnp.maximum(x, 0.0) + jnp.log1p(jnp.exp(-jnp.abs(x)))


def _split_hi_lo(x):
    hi = x.astype(BF16)
    lo = (x - hi.astype(F32)).astype(BF16)
    return jnp.concatenate([hi, lo], axis=1)


def _ssd_dt_body(dt_ref, dtb_ref, alog_ref, dts_ref, acss_ref, acst_ref, *, rows):
    L = SSD_CHUNK
    li = lax.broadcasted_iota(jnp.int32, (L, L), 0)
    si = lax.broadcasted_iota(jnp.int32, (L, L), 1)
    tril = (li >= si).astype(F32)
    a2_row = -jnp.exp(alog_ref[...]) * LOG2E
    for c in range(rows // L):
        rs = slice(c * L, (c + 1) * L)
        dtv = _softplus(dt_ref[rs, :] + dtb_ref[...])
        acs = jnp.dot(tril, dtv * a2_row, preferred_element_type=F32,
                      precision=lax.Precision.HIGHEST)
        dts_ref[rs, :] = _split_hi_lo(dtv)
        acss_ref[rs, :] = _split_hi_lo(acs)
        acst_ref[c] = acs.T


def _ssd_dt(dt_raw, dt_bias_p, a_log_p, *, rows=1024):
    T = dt_raw.shape[0]
    L = SSD_CHUNK
    body = functools.partial(_ssd_dt_body, rows=rows)
    return pl.pallas_call(
        body,
        grid=(T // rows,),
        in_specs=[
            pl.BlockSpec((rows, LANES), lambda i: (i, 0)),
            pl.BlockSpec((1, LANES), lambda i: (0, 0)),
            pl.BlockSpec((1, LANES), lambda i: (0, 0)),
        ],
        out_specs=[
            pl.BlockSpec((rows, 2 * LANES), lambda i: (i, 0)),
            pl.BlockSpec((rows, 2 * LANES), lambda i: (i, 0)),
            pl.BlockSpec((rows // L, LANES, L), lambda i: (i, 0, 0)),
        ],
        out_shape=[jax.ShapeDtypeStruct((T, 2 * LANES), BF16), jax.ShapeDtypeStruct((T, 2 * LANES), BF16),
                   jax.ShapeDtypeStruct((T // L, LANES, L), F32)],
        compiler_params=_cparams(("parallel",)),
        name="ssd_dt",
    )(dt_raw, dt_bias_p, a_log_p)


def _ssd_body(xs_ref, bm_ref, cm_ref, xs_h_ref, bm_h_ref, cm_h_ref, z_ref, dts_ref, acss_ref, acst_ref,
              wx_ref, wb_ref, wc_ref, bx_ref, bb_ref, bc_ref, dsk_ref, nw_ref, o_ref,
              ux_ref, ub_ref, uc_ref, st_ref, *, rows):
    g = pl.program_id(1)
    r = pl.program_id(2)
    L = SSD_CHUNK
    P = SSD_HD
    W = SSD_HPG * P
    HALO = SUBLANES

    @pl.when(r == 0)
    def _():
        st_ref[...] = jnp.zeros_like(st_ref)

    first = r == 0

    def conv(u_ref, halo_ref, buf_ref, w_ref, b_ref):
        halo = halo_ref[...]
        buf_ref[0:HALO, :] = jnp.where(first, jnp.zeros_like(halo), halo)
        buf_ref[HALO:, :] = u_ref[...]
        acc = b_ref[...] + w_ref[SSD_CONV - 1:SSD_CONV, :] * u_ref[...]
        for k in range(SSD_CONV - 1):
            shift = SSD_CONV - 1 - k
            acc = acc + w_ref[k:k + 1, :] * buf_ref[HALO - shift:HALO - shift + rows, :]
        buf_ref[HALO:, :] = acc * jax.nn.sigmoid(acc)

    conv(xs_ref, xs_h_ref, ux_ref, wx_ref, bx_ref)
    conv(bm_ref, bm_h_ref, ub_ref, wb_ref, bb_ref)
    conv(cm_ref, cm_h_ref, uc_ref, wc_ref, bc_ref)

    k_i = lax.broadcasted_iota(jnp.int32, (2 * LANES, W), 0)
    c_i = lax.broadcasted_iota(jnp.int32, (2 * LANES, W), 1)
    head = g * SSD_HPG + lax.shift_right_logical(c_i, P.bit_length() - 1)
    expand = ((k_i == head) | (k_i == head + LANES)).astype(BF16)

    li = lax.broadcasted_iota(jnp.int32, (L, L), 0)
    si = lax.broadcasted_iota(jnp.int32, (L, L), 1)
    causal = li >= si
    lane = lax.broadcasted_iota(jnp.int32, (L, LANES), 1)

    for c in range(rows // L):
        r0 = c * L
        rs = slice(HALO + r0, HALO + r0 + L)
        dt_x = _dot(dts_ref[r0:r0 + L, :], expand)
        acs_x = _dot(acss_ref[r0:r0 + L, :], expand)
        acs_last = acs_x[L - 1:L, :]

        xs = ux_ref[rs, :]
        bmat = ub_ref[rs, :]
        cmat = uc_ref[rs, :]
        x = xs * dt_x
        xb = x.astype(BF16)
        cb16 = cmat.astype(BF16)
        cb = jnp.where(causal, _dot_nt(cb16, bmat.astype(BF16)), 0.0)

        tiles = []
        for m in range(W // LANES):
            res = []
            for jj in range(2):
                j = 2 * m + jj
                col = acs_x[:, j * P:j * P + 1]
                row = acst_ref[c, pl.ds(g * SSD_HPG + j, 1), :]
                dec = jnp.exp2(jnp.minimum(col - row, 0.0))
                sc = (cb * dec).astype(BF16)
                res.append(_dot(sc, xb[:, m * LANES:(m + 1) * LANES]))
            tiles.append(jnp.where(lane < P, res[0], res[1]))
        y = jnp.concatenate(tiles, axis=1)

        st = st_ref[...]
        y = y + _dot(cb16, st.astype(BF16)) * jnp.exp2(acs_x)
        xd = (x * jnp.exp2(acs_last - acs_x)).astype(BF16)
        st_ref[...] = st * jnp.exp2(acs_last) + _dot(bmat.T.astype(BF16), xd)

        y = y + dsk_ref[...] * xs
        z = z_ref[r0:r0 + L, :]
        gt = y * (z * jax.nn.sigmoid(z))
        gn = gt * lax.rsqrt(jnp.mean(gt * gt, axis=-1, keepdims=True) + EPS)
        o_ref[r0:r0 + L, :] = (gn * nw_ref[...]).astype(o_ref.dtype)


def _ssd(p_main, dts, acss, acst, conv_w, conv_b, d_skip_x, ssd_norm, *, batch, seq, rows=1024):
    T = p_main.shape[0]
    G = SSD_GROUPS
    N = SSD_STATE
    W = SSD_HPG * SSD_HD
    nr = seq // rows
    hb = rows // SUBLANES
    z0, x0 = 3 * ATTN_HEADS * ATTN_HD // W, (3 * ATTN_HEADS * ATTN_HD + G * W) // W
    b0 = (3 * ATTN_HEADS * ATTN_HD + 2 * G * W) // N
    c0 = b0 + G
    cwb0, cwc0 = G * W // N, G * W // N + G

    def rowblk(b, g, r):
        return b * nr + r

    def halo(b, g, r):
        return jnp.maximum((b * nr + r) * hb - 1, 0)

    body = functools.partial(_ssd_body, rows=rows)
    return pl.pallas_call(
        body,
        grid=(batch, G, nr),
        in_specs=[
            pl.BlockSpec((rows, W), lambda b, g, r: (rowblk(b, g, r), x0 + g)),
            pl.BlockSpec((rows, N), lambda b, g, r: (rowblk(b, g, r), b0 + g)),
            pl.BlockSpec((rows, N), lambda b, g, r: (rowblk(b, g, r), c0 + g)),
            pl.BlockSpec((SUBLANES, W), lambda b, g, r: (halo(b, g, r), x0 + g)),
            pl.BlockSpec((SUBLANES, N), lambda b, g, r: (halo(b, g, r), b0 + g)),
            pl.BlockSpec((SUBLANES, N), lambda b, g, r: (halo(b, g, r), c0 + g)),
            pl.BlockSpec((rows, W), lambda b, g, r: (rowblk(b, g, r), z0 + g)),
            pl.BlockSpec((rows, 2 * LANES), lambda b, g, r: (rowblk(b, g, r), 0)),
            pl.BlockSpec((rows, 2 * LANES), lambda b, g, r: (rowblk(b, g, r), 0)),
            pl.BlockSpec((rows // SSD_CHUNK, LANES, SSD_CHUNK), lambda b, g, r: (rowblk(b, g, r), 0, 0)),
            pl.BlockSpec((SSD_CONV, W), lambda b, g, r: (0, g)),
            pl.BlockSpec((SSD_CONV, N), lambda b, g, r: (0, cwb0 + g)),
            pl.BlockSpec((SSD_CONV, N), lambda b, g, r: (0, cwc0 + g)),
            pl.BlockSpec((1, W), lambda b, g, r: (0, g)),
            pl.BlockSpec((1, N), lambda b, g, r: (0, cwb0 + g)),
            pl.BlockSpec((1, N), lambda b, g, r: (0, cwc0 + g)),
            pl.BlockSpec((1, W), lambda b, g, r: (0, g)),
            pl.BlockSpec((1, W), lambda b, g, r: (0, g)),
        ],
        out_specs=pl.BlockSpec((rows, W), lambda b, g, r: (rowblk(b, g, r), g)),
        out_shape=jax.ShapeDtypeStruct((T, G * W), BF16),
        scratch_shapes=[
            pltpu.VMEM((SUBLANES + rows, W), F32),
            pltpu.VMEM((SUBLANES + rows, N), F32),
            pltpu.VMEM((SUBLANES + rows, N), F32),
            pltpu.VMEM((N, W), F32),
        ],
        compiler_params=_cparams(("parallel", "parallel", "arbitrary")),
        name="ssd",
    )(p_main, p_main, p_main, p_main, p_main, p_main, p_main, dts, acss, acst,
      conv_w, conv_w, conv_w, conv_b, conv_b, conv_b, d_skip_x, ssd_norm)


def _outproj_body(x_ref, ya_ref, ys_ref, w_ref, o_ref):
    ka = ya_ref.shape[1]
    o_ref[...] = x_ref[...] + _dot(ya_ref[...], w_ref[0:ka, :]) + _dot(ys_ref[...], w_ref[ka:, :])


def _out_proj(x, ya, ys, w_out, l, *, tm=512, tn=1024):
    T, D = x.shape
    Ka, Ks = ya.shape[1], ys.shape[1]
    return pl.pallas_call(
        _outproj_body,
        grid=(D // tn, T // tm),
        in_specs=[
            pl.BlockSpec((tm, tn), lambda j, i: (i, j)),
            pl.BlockSpec((tm, Ka), lambda j, i: (i, 0)),
            pl.BlockSpec((tm, Ks), lambda j, i: (i, 0)),
            pl.BlockSpec((None, Ka + Ks, tn), lambda j, i: (l, 0, j)),
        ],
        out_specs=pl.BlockSpec((tm, tn), lambda j, i: (i, j)),
        out_shape=jax.ShapeDtypeStruct((T, D), F32),
        compiler_params=_cparams(("parallel", "parallel")),
        name="out_proj",
    )(x, ya, ys, w_out)


def _memkv_body(m_ref, nw_ref, wk_ref, wv_ref, kn_ref, k_ref, v_ref):
    h = _rms(m_ref[...], nw_ref[...]).astype(BF16)
    k = _dot(h, wk_ref[...])
    v_ref[...] = _dot(h, wv_ref[...]).astype(BF16)
    for hh in range(MEM_HEADS):
        sl = slice(hh * MEM_HD, (hh + 1) * MEM_HD)
        k_ref[:, sl] = _rms(k[:, sl], kn_ref[...]).astype(BF16)


def _mem_kv(mem, nw, wk, wv, kn, l):
    M, D = mem.shape
    Wm = wk.shape[2]
    full = lambda shape: pl.BlockSpec(shape, lambda i: (0,) * len(shape))
    layer = lambda shape: pl.BlockSpec((None,) + shape, lambda i: (l,) + (0,) * len(shape))
    return pl.pallas_call(
        _memkv_body,
        grid=(1,),
        in_specs=[full((M, D)), full((1, D)), layer((D, Wm)), layer((D, Wm)), full((1, MEM_HD))],
        out_specs=[full((M, Wm)), full((M, Wm))],
        out_shape=[jax.ShapeDtypeStruct((M, Wm), BF16)] * 2,
        compiler_params=_cparams(("arbitrary",)),
        name="mem_kv",
    )(mem, nw, wk, wv, kn)


def _memattn_body(x_ref, nw_ref, wq_ref, qn_ref, k_ref, v_ref, wo_ref, o_ref):
    x = x_ref[...]
    h = _rms(x, nw_ref[...]).astype(BF16)
    q = _dot(h, wq_ref[...])
    scale = MEM_HD ** -0.5
    outs = []
    for hh in range(MEM_HEADS):
        sl = slice(hh * MEM_HD, (hh + 1) * MEM_HD)
        qh = _rms(q[:, sl], qn_ref[...]).astype(BF16)
        s = _dot_nt(qh, k_ref[:, sl]) * scale
        e = jnp.exp(s - jnp.max(s, axis=-1, keepdims=True))
        p = e / jnp.sum(e, axis=-1, keepdims=True)
        outs.append(_dot(p.astype(BF16), v_ref[:, sl]))
    o = jnp.concatenate(outs, axis=1).astype(BF16)
    o_ref[...] = x + _dot(o, wo_ref[...])


def _mem_attn(x, nw, wq, qn, k, v, wo, l, *, seq, mem_len, tm=512):
    T, D = x.shape
    Wm = wq.shape[2]
    per_b = seq // tm
    return pl.pallas_call(
        _memattn_body,
        grid=(T // tm,),
        in_specs=[
            pl.BlockSpec((tm, D), lambda i: (i, 0)),
            pl.BlockSpec((1, D), lambda i: (0, 0)),
            pl.BlockSpec((None, D, Wm), lambda i: (l, 0, 0)),
            pl.BlockSpec((1, MEM_HD), lambda i: (0, 0)),
            pl.BlockSpec((mem_len, Wm), lambda i: (i // per_b, 0)),
            pl.BlockSpec((mem_len, Wm), lambda i: (i // per_b, 0)),
            pl.BlockSpec((None, Wm, D), lambda i: (l, 0, 0)),
        ],
        out_specs=pl.BlockSpec((tm, D), lambda i: (i, 0)),
        out_shape=jax.ShapeDtypeStruct((T, D), F32),
        compiler_params=_cparams(("parallel",)),
        name="mem_attn",
    )(x, nw, wq, qn, k, v, wo)


def _pad_lanes(v):
    return jnp.pad(v.astype(F32), (0, LANES - v.shape[0])).reshape(1, LANES)


def kernel(x, mem, ff1_norm, ff1_w_gu, ff1_w_down, mix_norm, w_in, q_norm, k_norm, conv_w, conv_b,
           dt_bias, a_log, d_skip, ssd_norm, w_out, xmem_norm, mem_norm, mem_wq, mem_wk, mem_wv,
           mem_q_norm, mem_k_norm, mem_wo, ff2_norm, ff2_w_gu, ff2_w_down):
    B, S, D = x.shape
    M = mem.shape[1]
    depth = w_in.shape[0]
    n_dt = SSD_GROUPS * SSD_HPG
    row = lambda v: v.reshape(1, -1)

    slopes = jnp.exp2(-8.0 * jnp.arange(1, ATTN_HEADS + 1, dtype=F32) / ATTN_HEADS)
    slopes = jnp.broadcast_to(slopes[:, None, None], (ATTN_HEADS, 1, LANES))
    bf = _cast_bf16
    ff1_w_gu, ff1_w_down, ff2_w_gu, ff2_w_down = bf(ff1_w_gu), bf(ff1_w_down), bf(ff2_w_gu), bf(ff2_w_down)
    w_in = jnp.pad(w_in, ((0, 0), (0, 0), (0, LANES - n_dt))).astype(BF16)
    w_out = bf(w_out)
    mem_wq, mem_wk, mem_wv, mem_wo = bf(mem_wq), bf(mem_wk), bf(mem_wv), bf(mem_wo)

    xf = x.reshape(B * S, D)
    memf = mem.reshape(B * M, D)
    for l in range(depth):
        xf = _ffn(xf, row(ff1_norm[l]), ff1_w_gu, ff1_w_down, l)

        p_main, dt_raw = _in_proj(xf, row(mix_norm[l]), w_in, l)
        y_attn = _moba(p_main, row(q_norm[l]), row(k_norm[l]), slopes, batch=B, seq=S)
        dts, acss, acst = _ssd_dt(dt_raw, _pad_lanes(dt_bias[l]), _pad_lanes(a_log[l]))
        y_ssd = _ssd(p_main, dts, acss, acst, conv_w[l], row(conv_b[l]),
                     row(jnp.repeat(d_skip[l], SSD_HD)), row(ssd_norm[l]), batch=B, seq=S)
        xf = _out_proj(xf, y_attn, y_ssd, w_out, l)

        k_mem, v_mem = _mem_kv(memf, row(mem_norm[l]), mem_wk, mem_wv, row(mem_k_norm[l]), l)
        xf = _mem_attn(xf, row(xmem_norm[l]), mem_wq, row(mem_q_norm[l]), k_mem, v_mem, mem_wo, l,
                       seq=S, mem_len=M)

        xf = _ffn(xf, row(ff2_norm[l]), ff2_w_gu, ff2_w_down, l)
    return xf.reshape(B, S, D)
```

```python
import functools

import jax
import jax.numpy as jnp
from jax import lax
from jax.experimental import pallas as pl
from jax.experimental.pallas import tpu as pltpu

F32 = jnp.float32
BF16 = jnp.bfloat16
EPS = 1e-6
NEG = -1e30
LOG2E = 1.4426950408889634

LANES = 128
SUBLANES = 8
VMEM_LIMIT = 56 * 1024 * 1024

ATTN_HEADS = 8
ATTN_HD = 128
MOBA_BLOCK = 256
MOBA_TOPK = 3
SSD_HD = 64
SSD_GROUPS = 8
SSD_HPG = 6
SSD_STATE = 128
SSD_CONV = 4
SSD_CHUNK = 128
MEM_HEADS = 4
MEM_HD = 128


def _cparams(sem):
    return pltpu.CompilerParams(dimension_semantics=sem, vmem_limit_bytes=VMEM_LIMIT)


def _rms(x, w):
    return x * lax.rsqrt(jnp.mean(x * x, axis=-1, keepdims=True) + EPS) * w


def _dot(a, b):
    return jnp.dot(a, b, preferred_element_type=F32)


def _dot_nt(a, b, precision=None):
    return lax.dot_general(a, b, (((1,), (1,)), ((), ())), preferred_element_type=F32,
                           precision=precision)


CAST_BLOCK_BYTES = 12 * 1024 * 1024


def _cast_body(w_ref, o_ref):
    o_ref[...] = w_ref[...].astype(BF16)


def _cast_bf16(w):
    nl, R, C = w.shape
    tr = R
    while tr * C * 4 > CAST_BLOCK_BYTES and tr % 32 == 0:
        tr //= 2
    return pl.pallas_call(
        _cast_body,
        grid=(nl, R // tr),
        in_specs=[pl.BlockSpec((None, tr, C), lambda l, i: (l, i, 0))],
        out_specs=pl.BlockSpec((None, tr, C), lambda l, i: (l, i, 0)),
        out_shape=jax.ShapeDtypeStruct((nl, R, C), BF16),
        compiler_params=_cparams(("parallel", "parallel")),
        name="cast_bf16",
    )(w)


def _ffn_body(x_ref, nw_ref, wg_ref, wu_ref, wd_ref, o_ref, h_ref):
    j = pl.program_id(1)

    @pl.when(j == 0)
    def _():
        x = x_ref[...]
        h_ref[...] = _rms(x, nw_ref[...]).astype(BF16)
        o_ref[...] = x

    h = h_ref[...]
    g = _dot(h, wg_ref[...])
    u = _dot(h, wu_ref[...])
    a = (0.5 * (g * jax.nn.sigmoid(g)) * u).astype(BF16)
    o_ref[...] += _dot(a, wd_ref[...])


def _ffn(x, nw, w_gu, w_down, l, *, tm=1024, tf=512):
    T, D = x.shape
    F = w_down.shape[1]
    nf = F // tf
    return pl.pallas_call(
        _ffn_body,
        grid=(T // tm, nf),
        in_specs=[
            pl.BlockSpec((tm, D), lambda i, j: (i, 0)),
            pl.BlockSpec((1, D), lambda i, j: (0, 0)),
            pl.BlockSpec((None, D, tf), lambda i, j: (l, 0, j)),
            pl.BlockSpec((None, D, tf), lambda i, j: (l, 0, j + nf)),
            pl.BlockSpec((None, tf, D), lambda i, j: (l, j, 0)),
        ],
        out_specs=pl.BlockSpec((tm, D), lambda i, j: (i, 0)),
        out_shape=jax.ShapeDtypeStruct((T, D), F32),
        scratch_shapes=[pltpu.VMEM((tm, D), BF16)],
        compiler_params=_cparams(("parallel", "arbitrary")),
        name="ffn",
    )(x, nw, w_gu, w_gu, w_down)


def _inproj_body(x_ref, nw_ref, w_ref, wdt_ref, p_ref, dt_ref, h_ref):
    j = pl.program_id(1)

    @pl.when(j == 0)
    def _():
        h = _rms(x_ref[...], nw_ref[...]).astype(BF16)
        h_ref[...] = h
        dt_ref[...] = _dot(h, wdt_ref[...])

    p_ref[...] = _dot(h_ref[...], w_ref[...])


def _in_proj(x, nw, w_main, w_dt, l, *, tm=1024, tn=1024):
    T, D = x.shape
    N = w_main.shape[2]
    assert N % tn == 0 and tn % LANES == 0
    return pl.pallas_call(
        _inproj_body,
        grid=(T // tm, N // tn),
        in_specs=[
            pl.BlockSpec((tm, D), lambda i, j: (i, 0)),
            pl.BlockSpec((1, D), lambda i, j: (0, 0)),
            pl.BlockSpec((None, D, tn), lambda i, j: (l, 0, j)),
            pl.BlockSpec((None, D, LANES), lambda i, j: (l, 0, 0)),
        ],
        out_specs=[
            pl.BlockSpec((tm, tn), lambda i, j: (i, j)),
            pl.BlockSpec((tm, LANES), lambda i, j: (i, 0)),
        ],
        out_shape=[jax.ShapeDtypeStruct((T, N), F32), jax.ShapeDtypeStruct((T, LANES), F32)],
        scratch_shapes=[pltpu.VMEM((tm, D), BF16)],
        compiler_params=_cparams(("parallel", "arbitrary")),
        name="in_proj",
    )(x, nw, w_main, w_dt)


MOBA_EXT = LANES
MOBA_SEL0 = 16
MOBA_VARIANT = 2
MOBA_QBLOCKS = 2


def _split3(x):
    hi = x.astype(BF16).astype(F32)
    r = x - hi
    mid = r.astype(BF16).astype(F32)
    lo = (r - mid).astype(BF16).astype(F32)
    return hi, mid, lo


def _rows16(rows, width):
    r_i = lax.broadcasted_iota(jnp.int32, (2 * SUBLANES, width), 0)
    out = jnp.zeros((2 * SUBLANES, width), F32)
    for k, row in enumerate(rows):
        out = jnp.where(r_i == k, row, out)
    return out


def _moba_body(q_ref, k_ref, v_ref, qn_ref, kn_ref, slope_ref, o_ref,
               kx_ref, vt_ref, kmean_ref, qx_ref, *, nb):
    qi = pl.program_id(2)
    blk = MOBA_BLOCK
    seq = nb * blk
    scale = ATTN_HD ** -0.5
    slope = slope_ref[0:1, 0:1]

    @pl.when((pl.program_id(0) == 0) & (pl.program_id(1) == 0) & (qi == 0))
    def _():
        pos = lax.broadcasted_iota(jnp.int32, (seq, MOBA_EXT), 0)
        c = lax.broadcasted_iota(jnp.int32, (seq, MOBA_EXT), 1)
        n_of = lax.shift_right_logical(pos, blk.bit_length() - 1)
        j_of = pos & (blk - 1)
        ext = jnp.where(c < 3, 1, jnp.where(c < 6, n_of, jnp.where(c < 9, j_of, 0)))
        ext = jnp.where(c - MOBA_SEL0 == n_of, 1, ext)
        kx_ref[:, ATTN_HD:] = ext.astype(F32).astype(BF16)

    @pl.when(qi == 0)
    def _():
        kf = _rms(k_ref[...], kn_ref[...])
        kx_ref[:, 0:ATTN_HD] = kf.astype(BF16)
        for n in range(nb):
            kmean_ref[n:n + 1, :] = jnp.mean(kf[n * blk:(n + 1) * blk], axis=0, keepdims=True)
        vt_ref[...] = v_ref[...].T.astype(BF16)

        qn = _rms(q_ref[...], qn_ref[...])
        gate = _dot_nt(kmean_ref[...], qn, precision=lax.Precision.HIGHEST)
        n_iota = lax.broadcasted_iota(jnp.int32, gate.shape, 0).astype(F32)
        t_i = lax.broadcasted_iota(jnp.int32, (1, seq), 1)
        past_blk = n_iota < lax.shift_right_logical(t_i, blk.bit_length() - 1).astype(F32)
        gate = jnp.where(past_blk, gate, -jnp.inf)
        selb = jnp.full(gate.shape, NEG, F32)
        for _ in range(MOBA_TOPK):
            best = jnp.max(gate, axis=0, keepdims=True)
            first = jnp.min(jnp.where(gate == best, n_iota, float(nb)), axis=0, keepdims=True)
            pick = n_iota == first
            selb = jnp.where(pick, 0.0, selb)
            gate = jnp.where(pick, -jnp.inf, gate)
        selb = jnp.where(past_blk, selb, NEG)
        slope2 = slope * LOG2E
        ones = jnp.ones((1, seq), F32)
        bias_rows = (list(_split3(-slope2 * t_i.astype(F32))) + [v * ones for v in _split3(slope2 * float(blk))]
                     + [v * ones for v in _split3(slope2)])
        qx_ref[0:ATTN_HD, :] = (qn * (scale * LOG2E)).T.astype(BF16)
        qx_ref[ATTN_HD:ATTN_HD + MOBA_SEL0, :] = _rows16(bias_rows, seq).astype(BF16)
        qx_ref[ATTN_HD + MOBA_SEL0:ATTN_HD + MOBA_SEL0 + nb, :] = selb.astype(BF16)
        qx_ref[ATTN_HD + MOBA_SEL0 + nb:, :] = jnp.zeros((MOBA_EXT - MOBA_SEL0 - nb, seq), BF16)

    qw = MOBA_QBLOCKS * blk
    qcols = pl.ds(pl.multiple_of(qi * qw, qw), qw)
    qx_past = qx_ref[:, qcols]
    qx_own = jnp.concatenate([qx_ref[0:ATTN_HD + MOBA_SEL0, qcols],
                              jnp.zeros((MOBA_EXT - MOBA_SEL0, qw), BF16)], axis=0)

    key_i = lax.broadcasted_iota(jnp.int32, (blk, blk), 0)
    qry_i = lax.broadcasted_iota(jnp.int32, (blk, blk), 1)
    lanes = lambda a, u: a[:, u * blk:(u + 1) * blk]
    row_cat = lambda parts: jnp.concatenate(parts, axis=1)

    def attend(n_past):
        n = n_past * blk
        owns = [pl.ds(pl.multiple_of((qi * MOBA_QBLOCKS + u) * blk, blk), blk) for u in range(MOBA_QBLOCKS)]
        s_own = [jnp.where(qry_i >= key_i, _dot(kx_ref[owns[u], :], lanes(qx_own, u)), NEG)
                 for u in range(MOBA_QBLOCKS)]
        h = (n_past // 2) * blk
        s_a = _dot(kx_ref[0:h, :], qx_past) if h else None
        s_b = _dot(kx_ref[h:n, :], qx_past)
        parts = []
        if h:
            m_a = jnp.max(s_a, axis=0, keepdims=True)
            p_a = jnp.exp2(s_a - m_a)
            parts.append((m_a, jnp.sum(p_a, axis=0, keepdims=True), _dot(vt_ref[:, 0:h], p_a.astype(BF16))))
        m_b = jnp.maximum(jnp.max(s_b, axis=0, keepdims=True),
                          row_cat([jnp.max(so, axis=0, keepdims=True) for so in s_own]))
        p_b = jnp.exp2(s_b - m_b)
        p_own = [jnp.exp2(s_own[u] - lanes(m_b, u)) for u in range(MOBA_QBLOCKS)]
        l_b = jnp.sum(p_b, axis=0, keepdims=True) + row_cat([jnp.sum(po, axis=0, keepdims=True) for po in p_own])
        acc_b = _dot(vt_ref[:, h:n], p_b.astype(BF16)) + row_cat(
            [_dot(vt_ref[:, owns[u]], p_own[u].astype(BF16)) for u in range(MOBA_QBLOCKS)])
        parts.append((m_b, l_b, acc_b))
        m = functools.reduce(jnp.maximum, [pm for pm, _, _ in parts])
        l = sum(jnp.exp2(pm - m) * pl_ for pm, pl_, _ in parts)
        acc = sum(jnp.exp2(pm - m) * pa for pm, _, pa in parts)
        o_ref[...] = (acc / l).T.astype(o_ref.dtype)

    per = MOBA_VARIANT // MOBA_QBLOCKS
    for v in range(nb // MOBA_VARIANT):
        pl.when((qi >= v * per) & (qi < (v + 1) * per))(
            functools.partial(attend, (v + 1) * MOBA_VARIANT - 1))


def _moba(p_main, q_norm, k_norm, slopes, *, batch, seq):
    T = p_main.shape[0]
    blk = MOBA_BLOCK
    nb = seq // blk
    assert seq % blk == 0 and nb % MOBA_VARIANT == 0 and nb <= MOBA_EXT - MOBA_SEL0
    assert MOBA_VARIANT % MOBA_QBLOCKS == 0
    H = ATTN_HEADS
    qw = MOBA_QBLOCKS * blk
    nq = seq // qw
    body = functools.partial(_moba_body, nb=nb)
    return pl.pallas_call(
        body,
        grid=(batch, H, nq),
        in_specs=[
            pl.BlockSpec((seq, ATTN_HD), lambda b, h, i: (b, h)),
            pl.BlockSpec((seq, ATTN_HD), lambda b, h, i: (b, H + h)),
            pl.BlockSpec((seq, ATTN_HD), lambda b, h, i: (b, 2 * H + h)),
            pl.BlockSpec((1, ATTN_HD), lambda b, h, i: (0, 0)),
            pl.BlockSpec((1, ATTN_HD), lambda b, h, i: (0, 0)),
            pl.BlockSpec((None, 1, LANES), lambda b, h, i: (h, 0, 0)),
        ],
        out_specs=pl.BlockSpec((qw, ATTN_HD), lambda b, h, i: (b * nq + i, h)),
        out_shape=jax.ShapeDtypeStruct((T, H * ATTN_HD), BF16),
        scratch_shapes=[
            pltpu.VMEM((seq, ATTN_HD + MOBA_EXT), BF16),
            pltpu.VMEM((ATTN_HD, seq), BF16),
            pltpu.VMEM((nb, ATTN_HD), F32),
            pltpu.VMEM((ATTN_HD + MOBA_EXT, seq), BF16),
        ],
        compiler_params=_cparams(("arbitrary", "arbitrary", "arbitrary")),
        name="moba",
    )(p_main, p_main, p_main, q_norm, k_norm, slopes)


def _softplus(x):
    return jnp.maximum(x, 0.0) + jnp.log1p(jnp.exp(-jnp.abs(x)))


def _split_hi_lo(x):
    hi = x.astype(BF16)
    lo = (x - hi.astype(F32)).astype(BF16)
    return jnp.concatenate([hi, lo], axis=1)


def _ssd_dt_body(dt_ref, dtb_ref, alog_ref, dts_ref, acss_ref, acst_ref, *, rows):
    L = SSD_CHUNK
    li = lax.broadcasted_iota(jnp.int32, (L, L), 0)
    si = lax.broadcasted_iota(jnp.int32, (L, L), 1)
    tril = (li >= si).astype(F32)
    a2_row = -jnp.exp(alog_ref[...]) * LOG2E
    for c in range(rows // L):
        rs = slice(c * L, (c + 1) * L)
        dtv = _softplus(dt_ref[rs, :] + dtb_ref[...])
        acs = jnp.dot(tril, dtv * a2_row, preferred_element_type=F32,
                      precision=lax.Precision.HIGHEST)
        dts_ref[rs, :] = _split_hi_lo(dtv)
        acss_ref[rs, :] = _split_hi_lo(acs)
        acst_ref[c] = acs.T


def _ssd_dt(dt_raw, dt_bias_p, a_log_p, *, rows=1024):
    T = dt_raw.shape[0]
    L = SSD_CHUNK
    body = functools.partial(_ssd_dt_body, rows=rows)
    return pl.pallas_call(
        body,
        grid=(T // rows,),
        in_specs=[
            pl.BlockSpec((rows, LANES), lambda i: (i, 0)),
            pl.BlockSpec((1, LANES), lambda i: (0, 0)),
            pl.BlockSpec((1, LANES), lambda i: (0, 0)),
        ],
        out_specs=[
            pl.BlockSpec((rows, 2 * LANES), lambda i: (i, 0)),
            pl.BlockSpec((rows, 2 * LANES), lambda i: (i, 0)),
            pl.BlockSpec((rows // L, LANES, L), lambda i: (i, 0, 0)),
        ],
        out_shape=[jax.ShapeDtypeStruct((T, 2 * LANES), BF16), jax.ShapeDtypeStruct((T, 2 * LANES), BF16),
                   jax.ShapeDtypeStruct((T // L, LANES, L), F32)],
        compiler_params=_cparams(("parallel",)),
        name="ssd_dt",
    )(dt_raw, dt_bias_p, a_log_p)


def _ssd_body(xs_ref, bm_ref, cm_ref, xs_h_ref, bm_h_ref, cm_h_ref, z_ref, dts_ref, acss_ref, acst_ref,
              wx_ref, wb_ref, wc_ref, bx_ref, bb_ref, bc_ref, dsk_ref, nw_ref, o_ref,
              ux_ref, ub_ref, uc_ref, st_ref, *, rows):
    g = pl.program_id(1)
    r = pl.program_id(2)
    L = SSD_CHUNK
    P = SSD_HD
    W = SSD_HPG * P
    HALO = SUBLANES

    @pl.when(r == 0)
    def _():
        st_ref[...] = jnp.zeros_like(st_ref)

    first = r == 0

    def conv(u_ref, halo_ref, buf_ref, w_ref, b_ref):
        halo = halo_ref[...]
        buf_ref[0:HALO, :] = jnp.where(first, jnp.zeros_like(halo), halo)
        buf_ref[HALO:, :] = u_ref[...]
        acc = b_ref[...] + w_ref[SSD_CONV - 1:SSD_CONV, :] * u_ref[...]
        for k in range(SSD_CONV - 1):
            shift = SSD_CONV - 1 - k
            acc = acc + w_ref[k:k + 1, :] * buf_ref[HALO - shift:HALO - shift + rows, :]
        buf_ref[HALO:, :] = acc * jax.nn.sigmoid(acc)

    conv(xs_ref, xs_h_ref, ux_ref, wx_ref, bx_ref)
    conv(bm_ref, bm_h_ref, ub_ref, wb_ref, bb_ref)
    conv(cm_ref, cm_h_ref, uc_ref, wc_ref, bc_ref)

    k_i = lax.broadcasted_iota(jnp.int32, (2 * LANES, W), 0)
    c_i = lax.broadcasted_iota(jnp.int32, (2 * LANES, W), 1)
    head = g * SSD_HPG + lax.shift_right_logical(c_i, P.bit_length() - 1)
    expand = ((k_i == head) | (k_i == head + LANES)).astype(BF16)

    li = lax.broadcasted_iota(jnp.int32, (L, L), 0)
    si = lax.broadcasted_iota(jnp.int32, (L, L), 1)
    causal = li >= si
    lane = lax.broadcasted_iota(jnp.int32, (L, LANES), 1)

    for c in range(rows // L):
        r0 = c * L
        rs = slice(HALO + r0, HALO + r0 + L)
        dt_x = _dot(dts_ref[r0:r0 + L, :], expand)
        acs_x = _dot(acss_ref[r0:r0 + L, :], expand)
        acs_last = acs_x[L - 1:L, :]

        xs = ux_ref[rs, :]
        bmat = ub_ref[rs, :]
        cmat = uc_ref[rs, :]
        x = xs * dt_x
        xb = x.astype(BF16)
        cb16 = cmat.astype(BF16)
        cb = jnp.where(causal, _dot_nt(cb16, bmat.astype(BF16)), 0.0)

        tiles = []
        for m in range(W // LANES):
            res = []
            for jj in range(2):
                j = 2 * m + jj
                col = acs_x[:, j * P:j * P + 1]
                row = acst_ref[c, pl.ds(g * SSD_HPG + j, 1), :]
                dec = jnp.exp2(jnp.minimum(col - row, 0.0))
                sc = (cb * dec).astype(BF16)
                res.append(_dot(sc, xb[:, m * LANES:(m + 1) * LANES]))
            tiles.append(jnp.where(lane < P, res[0], res[1]))
        y = jnp.concatenate(tiles, axis=1)

        st = st_ref[...]
        y = y + _dot(cb16, st.astype(BF16)) * jnp.exp2(acs_x)
        xd = (x * jnp.exp2(acs_last - acs_x)).astype(BF16)
        st_ref[...] = st * jnp.exp2(acs_last) + _dot(bmat.T.astype(BF16), xd)

        y = y + dsk_ref[...] * xs
        z = z_ref[r0:r0 + L, :]
        gt = y * (z * jax.nn.sigmoid(z))
        gn = gt * lax.rsqrt(jnp.mean(gt * gt, axis=-1, keepdims=True) + EPS)
        o_ref[r0:r0 + L, :] = (gn * nw_ref[...]).astype(o_ref.dtype)


def _ssd(p_main, dts, acss, acst, conv_w, conv_b, d_skip_x, ssd_norm, *, batch, seq, rows=1024):
    T = p_main.shape[0]
    G = SSD_GROUPS
    N = SSD_STATE
    W = SSD_HPG * SSD_HD
    nr = seq // rows
    hb = rows // SUBLANES
    z0, x0 = 3 * ATTN_HEADS * ATTN_HD // W, (3 * ATTN_HEADS * ATTN_HD + G * W) // W
    b0 = (3 * ATTN_HEADS * ATTN_HD + 2 * G * W) // N
    c0 = b0 + G
    cwb0, cwc0 = G * W // N, G * W // N + G

    def rowblk(b, g, r):
        return b * nr + r

    def halo(b, g, r):
        return jnp.maximum((b * nr + r) * hb - 1, 0)

    body = functools.partial(_ssd_body, rows=rows)
    return pl.pallas_call(
        body,
        grid=(batch, G, nr),
        in_specs=[
            pl.BlockSpec((rows, W), lambda b, g, r: (rowblk(b, g, r), x0 + g)),
            pl.BlockSpec((rows, N), lambda b, g, r: (rowblk(b, g, r), b0 + g)),
            pl.BlockSpec((rows, N), lambda b, g, r: (rowblk(b, g, r), c0 + g)),
            pl.BlockSpec((SUBLANES, W), lambda b, g, r: (halo(b, g, r), x0 + g)),
            pl.BlockSpec((SUBLANES, N), lambda b, g, r: (halo(b, g, r), b0 + g)),
            pl.BlockSpec((SUBLANES, N), lambda b, g, r: (halo(b, g, r), c0 + g)),
            pl.BlockSpec((rows, W), lambda b, g, r: (rowblk(b, g, r), z0 + g)),
            pl.BlockSpec((rows, 2 * LANES), lambda b, g, r: (rowblk(b, g, r), 0)),
            pl.BlockSpec((rows, 2 * LANES), lambda b, g, r: (rowblk(b, g, r), 0)),
            pl.BlockSpec((rows // SSD_CHUNK, LANES, SSD_CHUNK), lambda b, g, r: (rowblk(b, g, r), 0, 0)),
            pl.BlockSpec((SSD_CONV, W), lambda b, g, r: (0, g)),
            pl.BlockSpec((SSD_CONV, N), lambda b, g, r: (0, cwb0 + g)),
            pl.BlockSpec((SSD_CONV, N), lambda b, g, r: (0, cwc0 + g)),
            pl.BlockSpec((1, W), lambda b, g, r: (0, g)),
            pl.BlockSpec((1, N), lambda b, g, r: (0, cwb0 + g)),
            pl.BlockSpec((1, N), lambda b, g, r: (0, cwc0 + g)),
            pl.BlockSpec((1, W), lambda b, g, r: (0, g)),
            pl.BlockSpec((1, W), lambda b, g, r: (0, g)),
        ],
        out_specs=pl.BlockSpec((rows, W), lambda b, g, r: (rowblk(b, g, r), g)),
        out_shape=jax.ShapeDtypeStruct((T, G * W), BF16),
        scratch_shapes=[
            pltpu.VMEM((SUBLANES + rows, W), F32),
            pltpu.VMEM((SUBLANES + rows, N), F32),
            pltpu.VMEM((SUBLANES + rows, N), F32),
            pltpu.VMEM((N, W), F32),
        ],
        compiler_params=_cparams(("parallel", "parallel", "arbitrary")),
        name="ssd",
    )(p_main, p_main, p_main, p_main, p_main, p_main, p_main, dts, acss, acst,
      conv_w, conv_w, conv_w, conv_b, conv_b, conv_b, d_skip_x, ssd_norm)


def _outproj_body(x_ref, ya_ref, ys_ref, w_ref, o_ref):
    ka = ya_ref.shape[1]
    o_ref[...] = x_ref[...] + _dot(ya_ref[...], w_ref[0:ka, :]) + _dot(ys_ref[...], w_ref[ka:, :])


def _out_proj(x, ya, ys, w_out, l, *, tm=512, tn=1024):
    T, D = x.shape
    Ka, Ks = ya.shape[1], ys.shape[1]
    return pl.pallas_call(
        _outproj_body,
        grid=(D // tn, T // tm),
        in_specs=[
            pl.BlockSpec((tm, tn), lambda j, i: (i, j)),
            pl.BlockSpec((tm, Ka), lambda j, i: (i, 0)),
            pl.BlockSpec((tm, Ks), lambda j, i: (i, 0)),
            pl.BlockSpec((None, Ka + Ks, tn), lambda j, i: (l, 0, j)),
        ],
        out_specs=pl.BlockSpec((tm, tn), lambda j, i: (i, j)),
        out_shape=jax.ShapeDtypeStruct((T, D), F32),
        compiler_params=_cparams(("parallel", "parallel")),
        name="out_proj",
    )(x, ya, ys, w_out)


def _memkv_body(m_ref, nw_ref, wk_ref, wv_ref, kn_ref, k_ref, v_ref):
    h = _rms(m_ref[...], nw_ref[...]).astype(BF16)
    k = _dot(h, wk_ref[...])
    v_ref[...] = _dot(h, wv_ref[...]).astype(BF16)
    for hh in range(MEM_HEADS):
        sl = slice(hh * MEM_HD, (hh + 1) * MEM_HD)
        k_ref[:, sl] = _rms(k[:, sl], kn_ref[...]).astype(BF16)


def _mem_kv(mem, nw, wk, wv, kn, l):
    M, D = mem.shape
    Wm = wk.shape[2]
    full = lambda shape: pl.BlockSpec(shape, lambda i: (0,) * len(shape))
    layer = lambda shape: pl.BlockSpec((None,) + shape, lambda i: (l,) + (0,) * len(shape))
    return pl.pallas_call(
        _memkv_body,
        grid=(1,),
        in_specs=[full((M, D)), full((1, D)), layer((D, Wm)), layer((D, Wm)), full((1, MEM_HD))],
        out_specs=[full((M, Wm)), full((M, Wm))],
        out_shape=[jax.ShapeDtypeStruct((M, Wm), BF16)] * 2,
        compiler_params=_cparams(("arbitrary",)),
        name="mem_kv",
    )(mem, nw, wk, wv, kn)


def _memattn_body(x_ref, nw_ref, wq_ref, qn_ref, k_ref, v_ref, wo_ref, o_ref):
    x = x_ref[...]
    h = _rms(x, nw_ref[...]).astype(BF16)
    q = _dot(h, wq_ref[...])
    scale = MEM_HD ** -0.5
    outs = []
    for hh in range(MEM_HEADS):
        sl = slice(hh * MEM_HD, (hh + 1) * MEM_HD)
        qh = _rms(q[:, sl], qn_ref[...]).astype(BF16)
        s = _dot_nt(qh, k_ref[:, sl]) * scale
        e = jnp.exp(s - jnp.max(s, axis=-1, keepdims=True))
        p = e / jnp.sum(e, axis=-1, keepdims=True)
        outs.append(_dot(p.astype(BF16), v_ref[:, sl]))
    o = jnp.concatenate(outs, axis=1).astype(BF16)
    o_ref[...] = x + _dot(o, wo_ref[...])


def _mem_attn(x, nw, wq, qn, k, v, wo, l, *, seq, mem_len, tm=512):
    T, D = x.shape
    Wm = wq.shape[2]
    per_b = seq // tm
    return pl.pallas_call(
        _memattn_body,
        grid=(T // tm,),
        in_specs=[
            pl.BlockSpec((tm, D), lambda i: (i, 0)),
            pl.BlockSpec((1, D), lambda i: (0, 0)),
            pl.BlockSpec((None, D, Wm), lambda i: (l, 0, 0)),
            pl.BlockSpec((1, MEM_HD), lambda i: (0, 0)),
            pl.BlockSpec((mem_len, Wm), lambda i: (i // per_b, 0)),
            pl.BlockSpec((mem_len, Wm), lambda i: (i // per_b, 0)),
            pl.BlockSpec((None, Wm, D), lambda i: (l, 0, 0)),
        ],
        out_specs=pl.BlockSpec((tm, D), lambda i: (i, 0)),
        out_shape=jax.ShapeDtypeStruct((T, D), F32),
        compiler_params=_cparams(("parallel",)),
        name="mem_attn",
    )(x, nw, wq, qn, k, v, wo)


def _pad_lanes(v):
    return jnp.pad(v.astype(F32), (0, LANES - v.shape[0])).reshape(1, LANES)


def kernel(x, mem, ff1_norm, ff1_w_gu, ff1_w_down, mix_norm, w_in, q_norm, k_norm, conv_w, conv_b,
           dt_bias, a_log, d_skip, ssd_norm, w_out, xmem_norm, mem_norm, mem_wq, mem_wk, mem_wv,
           mem_q_norm, mem_k_norm, mem_wo, ff2_norm, ff2_w_gu, ff2_w_down):
    B, S, D = x.shape
    M = mem.shape[1]
    depth = w_in.shape[0]
    n_dt = SSD_GROUPS * SSD_HPG
    row = lambda v: v.reshape(1, -1)

    slopes = jnp.exp2(-8.0 * jnp.arange(1, ATTN_HEADS + 1, dtype=F32) / ATTN_HEADS)
    slopes = jnp.broadcast_to(slopes[:, None, None], (ATTN_HEADS, 1, LANES))
    bf = _cast_bf16
    ff1_w_gu, ff1_w_down, ff2_w_gu, ff2_w_down = bf(ff1_w_gu), bf(ff1_w_down), bf(ff2_w_gu), bf(ff2_w_down)
    n_main = w_in.shape[2] - n_dt
    w_main = w_in[:, :, :n_main].astype(BF16)
    w_dt = jnp.pad(w_in[:, :, n_main:], ((0, 0), (0, 0), (0, LANES - n_dt))).astype(BF16)
    w_out = bf(w_out)
    mem_wq, mem_wk, mem_wv, mem_wo = bf(mem_wq), bf(mem_wk), bf(mem_wv), bf(mem_wo)

    xf = x.reshape(B * S, D)
    memf = mem.reshape(B * M, D)
    for l in range(depth):
        xf = _ffn(xf, row(ff1_norm[l]), ff1_w_gu, ff1_w_down, l)

        p_main, dt_raw = _in_proj(xf, row(mix_norm[l]), w_main, w_dt, l)
        y_attn = _moba(p_main, row(q_norm[l]), row(k_norm[l]), slopes, batch=B, seq=S)
        dts, acss, acst = _ssd_dt(dt_raw, _pad_lanes(dt_bias[l]), _pad_lanes(a_log[l]))
        y_ssd = _ssd(p_main, dts, acss, acst, conv_w[l], row(conv_b[l]),
                     row(jnp.repeat(d_skip[l], SSD_HD)), row(ssd_norm[l]), batch=B, seq=S)
        xf = _out_proj(xf, y_attn, y_ssd, w_out, l)

        k_mem, v_mem = _mem_kv(memf, row(mem_norm[l]), mem_wk, mem_wv, row(mem_k_norm[l]), l)
        xf = _mem_attn(xf, row(xmem_norm[l]), mem_wq, row(mem_q_norm[l]), k_mem, v_mem, mem_wo, l,
                       seq=S, mem_len=M)

        xf = _ffn(xf, row(ff2_norm[l]), ff2_w_gu, ff2_w_down, l)
    return xf.reshape(B, S, D)
```

```python
import functools

import jax
import jax.numpy as jnp
from jax import lax
from jax.experimental import pallas as pl
from jax.experimental.pallas import tpu as pltpu

F32 = jnp.float32
BF16 = jnp.bfloat16
EPS = 1e-6
NEG = -1e30
LOG2E = 1.4426950408889634

LANES = 128
SUBLANES = 8
VMEM_LIMIT = 56 * 1024 * 1024

ATTN_HEADS = 8
ATTN_HD = 128
MOBA_BLOCK = 256
MOBA_TOPK = 3
SSD_HD = 64
SSD_GROUPS = 8
SSD_HPG = 6
SSD_STATE = 128
SSD_CONV = 4
SSD_CHUNK = 128
MEM_HEADS = 4
MEM_HD = 128


def _cparams(sem):
    return pltpu.CompilerParams(dimension_semantics=sem, vmem_limit_bytes=VMEM_LIMIT)


def _rms(x, w):
    return x * lax.rsqrt(jnp.mean(x * x, axis=-1, keepdims=True) + EPS) * w


def _dot(a, b):
    return jnp.dot(a, b, preferred_element_type=F32)


def _dot_nt(a, b, precision=None):
    return lax.dot_general(a, b, (((1,), (1,)), ((), ())), preferred_element_type=F32,
                           precision=precision)


CAST_BLOCK_BYTES = 12 * 1024 * 1024


def _cast_body(w_ref, o_ref):
    o_ref[...] = w_ref[...].astype(BF16)


def _cast_bf16(w):
    nl, R, C = w.shape
    tr = R
    while tr * C * 4 > CAST_BLOCK_BYTES and tr % 32 == 0:
        tr //= 2
    return pl.pallas_call(
        _cast_body,
        grid=(nl, R // tr),
        in_specs=[pl.BlockSpec((None, tr, C), lambda l, i: (l, i, 0))],
        out_specs=pl.BlockSpec((None, tr, C), lambda l, i: (l, i, 0)),
        out_shape=jax.ShapeDtypeStruct((nl, R, C), BF16),
        compiler_params=_cparams(("parallel", "parallel")),
        name="cast_bf16",
    )(w)


def _ffn_body(x_ref, nw_ref, wg_ref, wu_ref, wd_ref, o_ref, h_ref):
    j = pl.program_id(1)

    @pl.when(j == 0)
    def _():
        x = x_ref[...]
        h_ref[...] = _rms(x, nw_ref[...]).astype(BF16)
        o_ref[...] = x

    h = h_ref[...]
    g = _dot(h, wg_ref[...])
    u = _dot(h, wu_ref[...])
    a = (0.5 * (g * jax.nn.sigmoid(g)) * u).astype(BF16)
    o_ref[...] += _dot(a, wd_ref[...])


def _ffn(x, nw, w_gu, w_down, l, *, tm=1024, tf=512):
    T, D = x.shape
    F = w_down.shape[1]
    nf = F // tf
    return pl.pallas_call(
        _ffn_body,
        grid=(T // tm, nf),
        in_specs=[
            pl.BlockSpec((tm, D), lambda i, j: (i, 0)),
            pl.BlockSpec((1, D), lambda i, j: (0, 0)),
            pl.BlockSpec((None, D, tf), lambda i, j: (l, 0, j)),
            pl.BlockSpec((None, D, tf), lambda i, j: (l, 0, j + nf)),
            pl.BlockSpec((None, tf, D), lambda i, j: (l, j, 0)),
        ],
        out_specs=pl.BlockSpec((tm, D), lambda i, j: (i, 0)),
        out_shape=jax.ShapeDtypeStruct((T, D), F32),
        scratch_shapes=[pltpu.VMEM((tm, D), BF16)],
        compiler_params=_cparams(("parallel", "arbitrary")),
        name="ffn",
    )(x, nw, w_gu, w_gu, w_down)


def _inproj_body(x_ref, nw_ref, w_ref, wdt_ref, p_ref, dt_ref, h_ref):
    j = pl.program_id(1)

    @pl.when(j == 0)
    def _():
        h = _rms(x_ref[...], nw_ref[...]).astype(BF16)
        h_ref[...] = h
        dt_ref[...] = _dot(h, wdt_ref[...])

    p_ref[...] = _dot(h_ref[...], w_ref[...])


def _in_proj(x, nw, w_in, l, *, tm=1024, tn=1024):
    T, D = x.shape
    N = w_in.shape[2] - LANES
    assert N % tn == 0 and tn % LANES == 0
    return pl.pallas_call(
        _inproj_body,
        grid=(T // tm, N // tn),
        in_specs=[
            pl.BlockSpec((tm, D), lambda i, j: (i, 0)),
            pl.BlockSpec((1, D), lambda i, j: (0, 0)),
            pl.BlockSpec((None, D, tn), lambda i, j: (l, 0, j)),
            pl.BlockSpec((None, D, LANES), lambda i, j: (l, 0, N // LANES)),
        ],
        out_specs=[
            pl.BlockSpec((tm, tn), lambda i, j: (i, j)),
            pl.BlockSpec((tm, LANES), lambda i, j: (i, 0)),
        ],
        out_shape=[jax.ShapeDtypeStruct((T, N), F32), jax.ShapeDtypeStruct((T, LANES), F32)],
        scratch_shapes=[pltpu.VMEM((tm, D), BF16)],
        compiler_params=_cparams(("parallel", "arbitrary")),
        name="in_proj",
    )(x, nw, w_in, w_in)


MOBA_EXT = LANES
MOBA_SEL0 = 16
MOBA_VARIANT = 2
MOBA_QBLOCKS = 2


def _split3(x):
    hi = x.astype(BF16).astype(F32)
    r = x - hi
    mid = r.astype(BF16).astype(F32)
    lo = (r - mid).astype(BF16).astype(F32)
    return hi, mid, lo


def _rows16(rows, width):
    r_i = lax.broadcasted_iota(jnp.int32, (2 * SUBLANES, width), 0)
    out = jnp.zeros((2 * SUBLANES, width), F32)
    for k, row in enumerate(rows):
        out = jnp.where(r_i == k, row, out)
    return out


def _moba_body(q_ref, k_ref, v_ref, qn_ref, kn_ref, slope_ref, o_ref,
               kx_ref, vt_ref, kmean_ref, qx_ref, *, nb):
    qi = pl.program_id(2)
    blk = MOBA_BLOCK
    seq = nb * blk
    scale = ATTN_HD ** -0.5
    slope = slope_ref[0:1, 0:1]

    @pl.when((pl.program_id(0) == 0) & (pl.program_id(1) == 0) & (qi == 0))
    def _():
        pos = lax.broadcasted_iota(jnp.int32, (seq, MOBA_EXT), 0)
        c = lax.broadcasted_iota(jnp.int32, (seq, MOBA_EXT), 1)
        n_of = lax.shift_right_logical(pos, blk.bit_length() - 1)
        j_of = pos & (blk - 1)
        ext = jnp.where(c < 3, 1, jnp.where(c < 6, n_of, jnp.where(c < 9, j_of, 0)))
        ext = jnp.where(c - MOBA_SEL0 == n_of, 1, ext)
        kx_ref[:, ATTN_HD:] = ext.astype(F32).astype(BF16)

    @pl.when(qi == 0)
    def _():
        kf = _rms(k_ref[...], kn_ref[...])
        kx_ref[:, 0:ATTN_HD] = kf.astype(BF16)
        for n in range(nb):
            kmean_ref[n:n + 1, :] = jnp.mean(kf[n * blk:(n + 1) * blk], axis=0, keepdims=True)
        vt_ref[...] = v_ref[...].T.astype(BF16)

        qn = _rms(q_ref[...], qn_ref[...])
        gate = _dot_nt(kmean_ref[...], qn, precision=lax.Precision.HIGHEST)
        n_iota = lax.broadcasted_iota(jnp.int32, gate.shape, 0).astype(F32)
        t_i = lax.broadcasted_iota(jnp.int32, (1, seq), 1)
        past_blk = n_iota < lax.shift_right_logical(t_i, blk.bit_length() - 1).astype(F32)
        gate = jnp.where(past_blk, gate, -jnp.inf)
        selb = jnp.full(gate.shape, NEG, F32)
        for _ in range(MOBA_TOPK):
            best = jnp.max(gate, axis=0, keepdims=True)
            first = jnp.min(jnp.where(gate == best, n_iota, float(nb)), axis=0, keepdims=True)
            pick = n_iota == first
            selb = jnp.where(pick, 0.0, selb)
            gate = jnp.where(pick, -jnp.inf, gate)
        selb = jnp.where(past_blk, selb, NEG)
        slope2 = slope * LOG2E
        ones = jnp.ones((1, seq), F32)
        bias_rows = (list(_split3(-slope2 * t_i.astype(F32))) + [v * ones for v in _split3(slope2 * float(blk))]
                     + [v * ones for v in _split3(slope2)])
        qx_ref[0:ATTN_HD, :] = (qn * (scale * LOG2E)).T.astype(BF16)
        qx_ref[ATTN_HD:ATTN_HD + MOBA_SEL0, :] = _rows16(bias_rows, seq).astype(BF16)
        qx_ref[ATTN_HD + MOBA_SEL0:ATTN_HD + MOBA_SEL0 + nb, :] = selb.astype(BF16)
        qx_ref[ATTN_HD + MOBA_SEL0 + nb:, :] = jnp.zeros((MOBA_EXT - MOBA_SEL0 - nb, seq), BF16)

    qw = MOBA_QBLOCKS * blk
    qcols = pl.ds(pl.multiple_of(qi * qw, qw), qw)
    qx_past = qx_ref[:, qcols]
    qx_own = jnp.concatenate([qx_ref[0:ATTN_HD + MOBA_SEL0, qcols],
                              jnp.zeros((MOBA_EXT - MOBA_SEL0, qw), BF16)], axis=0)

    key_i = lax.broadcasted_iota(jnp.int32, (blk, blk), 0)
    qry_i = lax.broadcasted_iota(jnp.int32, (blk, blk), 1)
    lanes = lambda a, u: a[:, u * blk:(u + 1) * blk]
    row_cat = lambda parts: jnp.concatenate(parts, axis=1)

    def attend(n_past):
        n = n_past * blk
        owns = [pl.ds(pl.multiple_of((qi * MOBA_QBLOCKS + u) * blk, blk), blk) for u in range(MOBA_QBLOCKS)]
        s_own = [jnp.where(qry_i >= key_i, _dot(kx_ref[owns[u], :], lanes(qx_own, u)), NEG)
                 for u in range(MOBA_QBLOCKS)]
        h = (n_past // 2) * blk
        s_a = _dot(kx_ref[0:h, :], qx_past) if h else None
        s_b = _dot(kx_ref[h:n, :], qx_past)
        parts = []
        if h:
            m_a = jnp.max(s_a, axis=0, keepdims=True)
            p_a = jnp.exp2(s_a - m_a)
            parts.append((m_a, jnp.sum(p_a, axis=0, keepdims=True), _dot(vt_ref[:, 0:h], p_a.astype(BF16))))
        m_b = jnp.maximum(jnp.max(s_b, axis=0, keepdims=True),
                          row_cat([jnp.max(so, axis=0, keepdims=True) for so in s_own]))
        p_b = jnp.exp2(s_b - m_b)
        p_own = [jnp.exp2(s_own[u] - lanes(m_b, u)) for u in range(MOBA_QBLOCKS)]
        l_b = jnp.sum(p_b, axis=0, keepdims=True) + row_cat([jnp.sum(po, axis=0, keepdims=True) for po in p_own])
        acc_b = _dot(vt_ref[:, h:n], p_b.astype(BF16)) + row_cat(
            [_dot(vt_ref[:, owns[u]], p_own[u].astype(BF16)) for u in range(MOBA_QBLOCKS)])
        parts.append((m_b, l_b, acc_b))
        m = functools.reduce(jnp.maximum, [pm for pm, _, _ in parts])
        l = sum(jnp.exp2(pm - m) * pl_ for pm, pl_, _ in parts)
        acc = sum(jnp.exp2(pm - m) * pa for pm, _, pa in parts)
        o_ref[...] = (acc / l).T.astype(o_ref.dtype)

    per = MOBA_VARIANT // MOBA_QBLOCKS
    for v in range(nb // MOBA_VARIANT):
        pl.when((qi >= v * per) & (qi < (v + 1) * per))(
            functools.partial(attend, (v + 1) * MOBA_VARIANT - 1))


def _moba(p_main, q_norm, k_norm, slopes, *, batch, seq):
    T = p_main.shape[0]
    blk = MOBA_BLOCK
    nb = seq // blk
    assert seq % blk == 0 and nb % MOBA_VARIANT == 0 and nb <= MOBA_EXT - MOBA_SEL0
    assert MOBA_VARIANT % MOBA_QBLOCKS == 0
    H = ATTN_HEADS
    qw = MOBA_QBLOCKS * blk
    nq = seq // qw
    body = functools.partial(_moba_body, nb=nb)
    return pl.pallas_call(
        body,
        grid=(batch, H, nq),
        in_specs=[
            pl.BlockSpec((seq, ATTN_HD), lambda b, h, i: (b, h)),
            pl.BlockSpec((seq, ATTN_HD), lambda b, h, i: (b, H + h)),
            pl.BlockSpec((seq, ATTN_HD), lambda b, h, i: (b, 2 * H + h)),
            pl.BlockSpec((1, ATTN_HD), lambda b, h, i: (0, 0)),
            pl.BlockSpec((1, ATTN_HD), lambda b, h, i: (0, 0)),
            pl.BlockSpec((None, 1, LANES), lambda b, h, i: (h, 0, 0)),
        ],
        out_specs=pl.BlockSpec((qw, ATTN_HD), lambda b, h, i: (b * nq + i, h)),
        out_shape=jax.ShapeDtypeStruct((T, H * ATTN_HD), BF16),
        scratch_shapes=[
            pltpu.VMEM((seq, ATTN_HD + MOBA_EXT), BF16),
            pltpu.VMEM((ATTN_HD, seq), BF16),
            pltpu.VMEM((nb, ATTN_HD), F32),
            pltpu.VMEM((ATTN_HD + MOBA_EXT, seq), BF16),
        ],
        compiler_params=_cparams(("arbitrary", "arbitrary", "arbitrary")),
        name="moba",
    )(p_main, p_main, p_main, q_norm, k_norm, slopes)


def _softplus(x):
    return jnp.maximum(x, 0.0) + jnp.log1p(jnp.exp(-jnp.abs(x)))


def _split_hi_lo(x):
    hi = x.astype(BF16)
    lo = (x - hi.astype(F32)).astype(BF16)
    return jnp.concatenate([hi, lo], axis=1)


def _ssd_dt_body(dt_ref, dtb_ref, alog_ref, dts_ref, acss_ref, acst_ref, *, rows):
    L = SSD_CHUNK
    li = lax.broadcasted_iota(jnp.int32, (L, L), 0)
    si = lax.broadcasted_iota(jnp.int32, (L, L), 1)
    tril = (li >= si).astype(F32)
    a2_row = -jnp.exp(alog_ref[...]) * LOG2E
    for c in range(rows // L):
        rs = slice(c * L, (c + 1) * L)
        dtv = _softplus(dt_ref[rs, :] + dtb_ref[...])
        acs = jnp.dot(tril, dtv * a2_row, preferred_element_type=F32,
                      precision=lax.Precision.HIGHEST)
        dts_ref[rs, :] = _split_hi_lo(dtv)
        acss_ref[rs, :] = _split_hi_lo(acs)
        acst_ref[c] = acs.T


def _ssd_dt(dt_raw, dt_bias_p, a_log_p, *, rows=1024):
    T = dt_raw.shape[0]
    L = SSD_CHUNK
    body = functools.partial(_ssd_dt_body, rows=rows)
    return pl.pallas_call(
        body,
        grid=(T // rows,),
        in_specs=[
            pl.BlockSpec((rows, LANES), lambda i: (i, 0)),
            pl.BlockSpec((1, LANES), lambda i: (0, 0)),
            pl.BlockSpec((1, LANES), lambda i: (0, 0)),
        ],
        out_specs=[
            pl.BlockSpec((rows, 2 * LANES), lambda i: (i, 0)),
            pl.BlockSpec((rows, 2 * LANES), lambda i: (i, 0)),
            pl.BlockSpec((rows // L, LANES, L), lambda i: (i, 0, 0)),
        ],
        out_shape=[jax.ShapeDtypeStruct((T, 2 * LANES), BF16), jax.ShapeDtypeStruct((T, 2 * LANES), BF16),
                   jax.ShapeDtypeStruct((T // L, LANES, L), F32)],
        compiler_params=_cparams(("parallel",)),
        name="ssd_dt",
    )(dt_raw, dt_bias_p, a_log_p)


def _ssd_body(xs_ref, bm_ref, cm_ref, xs_h_ref, bm_h_ref, cm_h_ref, z_ref, dts_ref, acss_ref, acst_ref,
              wx_ref, wb_ref, wc_ref, bx_ref, bb_ref, bc_ref, dsk_ref, nw_ref, o_ref,
              ux_ref, ub_ref, uc_ref, st_ref, *, rows):
    g = pl.program_id(1)
    r = pl.program_id(2)
    L = SSD_CHUNK
    P = SSD_HD
    W = SSD_HPG * P
    HALO = SUBLANES

    @pl.when(r == 0)
    def _():
        st_ref[...] = jnp.zeros_like(st_ref)

    first = r == 0

    def conv(u_ref, halo_ref, buf_ref, w_ref, b_ref):
        halo = halo_ref[...]
        buf_ref[0:HALO, :] = jnp.where(first, jnp.zeros_like(halo), halo)
        buf_ref[HALO:, :] = u_ref[...]
        acc = b_ref[...] + w_ref[SSD_CONV - 1:SSD_CONV, :] * u_ref[...]
        for k in range(SSD_CONV - 1):
            shift = SSD_CONV - 1 - k
            acc = acc + w_ref[k:k + 1, :] * buf_ref[HALO - shift:HALO - shift + rows, :]
        buf_ref[HALO:, :] = acc * jax.nn.sigmoid(acc)

    conv(xs_ref, xs_h_ref, ux_ref, wx_ref, bx_ref)
    conv(bm_ref, bm_h_ref, ub_ref, wb_ref, bb_ref)
    conv(cm_ref, cm_h_ref, uc_ref, wc_ref, bc_ref)

    k_i = lax.broadcasted_iota(jnp.int32, (2 * LANES, W), 0)
    c_i = lax.broadcasted_iota(jnp.int32, (2 * LANES, W), 1)
    head = g * SSD_HPG + lax.shift_right_logical(c_i, P.bit_length() - 1)
    expand = ((k_i == head) | (k_i == head + LANES)).astype(BF16)

    li = lax.broadcasted_iota(jnp.int32, (L, L), 0)
    si = lax.broadcasted_iota(jnp.int32, (L, L), 1)
    causal = li >= si
    lane = lax.broadcasted_iota(jnp.int32, (L, LANES), 1)

    for c in range(rows // L):
        r0 = c * L
        rs = slice(HALO + r0, HALO + r0 + L)
        dt_x = _dot(dts_ref[r0:r0 + L, :], expand)
        acs_x = _dot(acss_ref[r0:r0 + L, :], expand)
        acs_last = acs_x[L - 1:L, :]

        xs = ux_ref[rs, :]
        bmat = ub_ref[rs, :]
        cmat = uc_ref[rs, :]
        x = xs * dt_x
        xb = x.astype(BF16)
        cb16 = cmat.astype(BF16)
        cb = jnp.where(causal, _dot_nt(cb16, bmat.astype(BF16)), 0.0)

        tiles = []
        for m in range(W // LANES):
            res = []
            for jj in range(2):
                j = 2 * m + jj
                col = acs_x[:, j * P:j * P + 1]
                row = acst_ref[c, pl.ds(g * SSD_HPG + j, 1), :]
                dec = jnp.exp2(jnp.minimum(col - row, 0.0))
                sc = (cb * dec).astype(BF16)
                res.append(_dot(sc, xb[:, m * LANES:(m + 1) * LANES]))
            tiles.append(jnp.where(lane < P, res[0], res[1]))
        y = jnp.concatenate(tiles, axis=1)

        st = st_ref[...]
        y = y + _dot(cb16, st.astype(BF16)) * jnp.exp2(acs_x)
        xd = (x * jnp.exp2(acs_last - acs_x)).astype(BF16)
        st_ref[...] = st * jnp.exp2(acs_last) + _dot(bmat.T.astype(BF16), xd)

        y = y + dsk_ref[...] * xs
        z = z_ref[r0:r0 + L, :]
        gt = y * (z * jax.nn.sigmoid(z))
        gn = gt * lax.rsqrt(jnp.mean(gt * gt, axis=-1, keepdims=True) + EPS)
        o_ref[r0:r0 + L, :] = (gn * nw_ref[...]).astype(o_ref.dtype)


def _ssd(p_main, dts, acss, acst, conv_w, conv_b, d_skip_x, ssd_norm, *, batch, seq, rows=1024):
    T = p_main.shape[0]
    G = SSD_GROUPS
    N = SSD_STATE
    W = SSD_HPG * SSD_HD
    nr = seq // rows
    hb = rows // SUBLANES
    z0, x0 = 3 * ATTN_HEADS * ATTN_HD // W, (3 * ATTN_HEADS * ATTN_HD + G * W) // W
    b0 = (3 * ATTN_HEADS * ATTN_HD + 2 * G * W) // N
    c0 = b0 + G
    cwb0, cwc0 = G * W // N, G * W // N + G

    def rowblk(b, g, r):
        return b * nr + r

    def halo(b, g, r):
        return jnp.maximum((b * nr + r) * hb - 1, 0)

    body = functools.partial(_ssd_body, rows=rows)
    return pl.pallas_call(
        body,
        grid=(batch, G, nr),
        in_specs=[
            pl.BlockSpec((rows, W), lambda b, g, r: (rowblk(b, g, r), x0 + g)),
            pl.BlockSpec((rows, N), lambda b, g, r: (rowblk(b, g, r), b0 + g)),
            pl.BlockSpec((rows, N), lambda b, g, r: (rowblk(b, g, r), c0 + g)),
            pl.BlockSpec((SUBLANES, W), lambda b, g, r: (halo(b, g, r), x0 + g)),
            pl.BlockSpec((SUBLANES, N), lambda b, g, r: (halo(b, g, r), b0 + g)),
            pl.BlockSpec((SUBLANES, N), lambda b, g, r: (halo(b, g, r), c0 + g)),
            pl.BlockSpec((rows, W), lambda b, g, r: (rowblk(b, g, r), z0 + g)),
            pl.BlockSpec((rows, 2 * LANES), lambda b, g, r: (rowblk(b, g, r), 0)),
            pl.BlockSpec((rows, 2 * LANES), lambda b, g, r: (rowblk(b, g, r), 0)),
            pl.BlockSpec((rows // SSD_CHUNK, LANES, SSD_CHUNK), lambda b, g, r: (rowblk(b, g, r), 0, 0)),
            pl.BlockSpec((SSD_CONV, W), lambda b, g, r: (0, g)),
            pl.BlockSpec((SSD_CONV, N), lambda b, g, r: (0, cwb0 + g)),
            pl.BlockSpec((SSD_CONV, N), lambda b, g, r: (0, cwc0 + g)),
            pl.BlockSpec((1, W), lambda b, g, r: (0, g)),
            pl.BlockSpec((1, N), lambda b, g, r: (0, cwb0 + g)),
            pl.BlockSpec((1, N), lambda b, g, r: (0, cwc0 + g)),
            pl.BlockSpec((1, W), lambda b, g, r: (0, g)),
            pl.BlockSpec((1, W), lambda b, g, r: (0, g)),
        ],
        out_specs=pl.BlockSpec((rows, W), lambda b, g, r: (rowblk(b, g, r), g)),
        out_shape=jax.ShapeDtypeStruct((T, G * W), BF16),
        scratch_shapes=[
            pltpu.VMEM((SUBLANES + rows, W), F32),
            pltpu.VMEM((SUBLANES + rows, N), F32),
            pltpu.VMEM((SUBLANES + rows, N), F32),
            pltpu.VMEM((N, W), F32),
        ],
        compiler_params=_cparams(("parallel", "parallel", "arbitrary")),
        name="ssd",
    )(p_main, p_main, p_main, p_main, p_main, p_main, p_main, dts, acss, acst,
      conv_w, conv_w, conv_w, conv_b, conv_b, conv_b, d_skip_x, ssd_norm)


def _outproj_body(x_ref, ya_ref, ys_ref, w_ref, o_ref):
    ka = ya_ref.shape[1]
    o_ref[...] = x_ref[...] + _dot(ya_ref[...], w_ref[0:ka, :]) + _dot(ys_ref[...], w_ref[ka:, :])


def _out_proj(x, ya, ys, w_out, l, *, tm=512, tn=1024):
    T, D = x.shape
    Ka, Ks = ya.shape[1], ys.shape[1]
    return pl.pallas_call(
        _outproj_body,
        grid=(D // tn, T // tm),
        in_specs=[
            pl.BlockSpec((tm, tn), lambda j, i: (i, j)),
            pl.BlockSpec((tm, Ka), lambda j, i: (i, 0)),
            pl.BlockSpec((tm, Ks), lambda j, i: (i, 0)),
            pl.BlockSpec((None, Ka + Ks, tn), lambda j, i: (l, 0, j)),
        ],
        out_specs=pl.BlockSpec((tm, tn), lambda j, i: (i, j)),
        out_shape=jax.ShapeDtypeStruct((T, D), F32),
        compiler_params=_cparams(("parallel", "parallel")),
        name="out_proj",
    )(x, ya, ys, w_out)


def _memkv_body(m_ref, nw_ref, wk_ref, wv_ref, kn_ref, k_ref, v_ref):
    h = _rms(m_ref[...], nw_ref[...]).astype(BF16)
    k = _dot(h, wk_ref[...])
    v_ref[...] = _dot(h, wv_ref[...]).astype(BF16)
    for hh in range(MEM_HEADS):
        sl = slice(hh * MEM_HD, (hh + 1) * MEM_HD)
        k_ref[:, sl] = _rms(k[:, sl], kn_ref[...]).astype(BF16)


def _mem_kv(mem, nw, wk, wv, kn, l):
    M, D = mem.shape
    Wm = wk.shape[2]
    full = lambda shape: pl.BlockSpec(shape, lambda i: (0,) * len(shape))
    layer = lambda shape: pl.BlockSpec((None,) + shape, lambda i: (l,) + (0,) * len(shape))
    return pl.pallas_call(
        _memkv_body,
        grid=(1,),
        in_specs=[full((M, D)), full((1, D)), layer((D, Wm)), layer((D, Wm)), full((1, MEM_HD))],
        out_specs=[full((M, Wm)), full((M, Wm))],
        out_shape=[jax.ShapeDtypeStruct((M, Wm), BF16)] * 2,
        compiler_params=_cparams(("arbitrary",)),
        name="mem_kv",
    )(mem, nw, wk, wv, kn)


def _memattn_body(x_ref, nw_ref, wq_ref, qn_ref, k_ref, v_ref, wo_ref, o_ref):
    x = x_ref[...]
    h = _rms(x, nw_ref[...]).astype(BF16)
    q = _dot(h, wq_ref[...])
    scale = MEM_HD ** -0.5
    outs = []
    for hh in range(MEM_HEADS):
        sl = slice(hh * MEM_HD, (hh + 1) * MEM_HD)
        qh = _rms(q[:, sl], qn_ref[...]).astype(BF16)
        s = _dot_nt(qh, k_ref[:, sl]) * scale
        e = jnp.exp(s - jnp.max(s, axis=-1, keepdims=True))
        p = e / jnp.sum(e, axis=-1, keepdims=True)
        outs.append(_dot(p.astype(BF16), v_ref[:, sl]))
    o = jnp.concatenate(outs, axis=1).astype(BF16)
    o_ref[...] = x + _dot(o, wo_ref[...])


def _mem_attn(x, nw, wq, qn, k, v, wo, l, *, seq, mem_len, tm=512):
    T, D = x.shape
    Wm = wq.shape[2]
    per_b = seq // tm
    return pl.pallas_call(
        _memattn_body,
        grid=(T // tm,),
        in_specs=[
            pl.BlockSpec((tm, D), lambda i: (i, 0)),
            pl.BlockSpec((1, D), lambda i: (0, 0)),
            pl.BlockSpec((None, D, Wm), lambda i: (l, 0, 0)),
            pl.BlockSpec((1, MEM_HD), lambda i: (0, 0)),
            pl.BlockSpec((mem_len, Wm), lambda i: (i // per_b, 0)),
            pl.BlockSpec((mem_len, Wm), lambda i: (i // per_b, 0)),
            pl.BlockSpec((None, Wm, D), lambda i: (l, 0, 0)),
        ],
        out_specs=pl.BlockSpec((tm, D), lambda i: (i, 0)),
        out_shape=jax.ShapeDtypeStruct((T, D), F32),
        compiler_params=_cparams(("parallel",)),
        name="mem_attn",
    )(x, nw, wq, qn, k, v, wo)


def _pad_lanes(v):
    return jnp.pad(v.astype(F32), (0, LANES - v.shape[0])).reshape(1, LANES)


def kernel(x, mem, ff1_norm, ff1_w_gu, ff1_w_down, mix_norm, w_in, q_norm, k_norm, conv_w, conv_b,
           dt_bias, a_log, d_skip, ssd_norm, w_out, xmem_norm, mem_norm, mem_wq, mem_wk, mem_wv,
           mem_q_norm, mem_k_norm, mem_wo, ff2_norm, ff2_w_gu, ff2_w_down):
    B, S, D = x.shape
    M = mem.shape[1]
    depth = w_in.shape[0]
    n_dt = SSD_GROUPS * SSD_HPG
    row = lambda v: v.reshape(1, -1)

    slopes = jnp.exp2(-8.0 * jnp.arange(1, ATTN_HEADS + 1, dtype=F32) / ATTN_HEADS)
    slopes = jnp.broadcast_to(slopes[:, None, None], (ATTN_HEADS, 1, LANES))
    bf = _cast_bf16
    ff1_w_gu, ff1_w_down, ff2_w_gu, ff2_w_down = bf(ff1_w_gu), bf(ff1_w_down), bf(ff2_w_gu), bf(ff2_w_down)
    w_in = jnp.pad(w_in, ((0, 0), (0, 0), (0, LANES - n_dt))).astype(BF16)
    w_out = bf(w_out)
    mem_wq, mem_wk, mem_wv, mem_wo = bf(mem_wq), bf(mem_wk), bf(mem_wv), bf(mem_wo)

    xf = x.reshape(B * S, D)
    memf = mem.reshape(B * M, D)
    for l in range(depth):
        xf = _ffn(xf, row(ff1_norm[l]), ff1_w_gu, ff1_w_down, l)

        p_main, dt_raw = _in_proj(xf, row(mix_norm[l]), w_in, l)
        y_attn = _moba(p_main, row(q_norm[l]), row(k_norm[l]), slopes, batch=B, seq=S)
        dts, acss, acst = _ssd_dt(dt_raw, _pad_lanes(dt_bias[l]), _pad_lanes(a_log[l]))
        y_ssd = _ssd(p_main, dts, acss, acst, conv_w[l], row(conv_b[l]),
                     row(jnp.repeat(d_skip[l], SSD_HD)), row(ssd_norm[l]), batch=B, seq=S)
        xf = _out_proj(xf, y_attn, y_ssd, w_out, l)

        k_mem, v_mem = _mem_kv(memf, row(mem_norm[l]), mem_wk, mem_wv, row(mem_k_norm[l]), l)
        xf = _mem_attn(xf, row(xmem_norm[l]), mem_wq, row(mem_q_norm[l]), k_mem, v_mem, mem_wo, l,
                       seq=S, mem_len=M)

        xf = _ffn(xf, row(ff2_norm[l]), ff2_w_gu, ff2_w_down, l)
    return xf.reshape(B, S, D)
```

```python
import functools

import jax
import jax.numpy as jnp
from jax import lax
from jax.experimental import pallas as pl
from jax.experimental.pallas import tpu as pltpu

F32 = jnp.float32
BF16 = jnp.bfloat16
EPS = 1e-6
NEG = -1e30
LOG2E = 1.4426950408889634

LANES = 128
SUBLANES = 8
VMEM_LIMIT = 56 * 1024 * 1024

ATTN_HEADS = 8
ATTN_HD = 128
MOBA_BLOCK = 256
MOBA_TOPK = 3
SSD_HD = 64
SSD_GROUPS = 8
SSD_HPG = 6
SSD_STATE = 128
SSD_CONV = 4
SSD_CHUNK = 128
MEM_HEADS = 4
MEM_HD = 128


def _cparams(sem):
    return pltpu.CompilerParams(dimension_semantics=sem, vmem_limit_bytes=VMEM_LIMIT)


def _rms(x, w):
    return x * lax.rsqrt(jnp.mean(x * x, axis=-1, keepdims=True) + EPS) * w


def _dot(a, b):
    return jnp.dot(a, b, preferred_element_type=F32)


def _dot_nt(a, b, precision=None):
    return lax.dot_general(a, b, (((1,), (1,)), ((), ())), preferred_element_type=F32,
                           precision=precision)


CAST_BLOCK_BYTES = 12 * 1024 * 1024


def _cast_body(w_ref, o_ref):
    o_ref[...] = w_ref[...].astype(BF16)


def _cast_bf16(w, layer=None):
    nl, R, C = w.shape
    first = 0
    if layer is not None:
        nl, first = 1, layer
    tr = R
    while tr * C * 4 > CAST_BLOCK_BYTES and tr % 32 == 0:
        tr //= 2
    return pl.pallas_call(
        _cast_body,
        grid=(nl, R // tr),
        in_specs=[pl.BlockSpec((None, tr, C), lambda l, i: (first + l, i, 0))],
        out_specs=pl.BlockSpec((None, tr, C), lambda l, i: (l, i, 0)),
        out_shape=jax.ShapeDtypeStruct((nl, R, C), BF16),
        compiler_params=_cparams(("parallel", "parallel")),
        name="cast_bf16",
    )(w)


def _ffn_body(x_ref, nw_ref, wg_ref, wu_ref, wd_ref, o_ref, h_ref):
    j = pl.program_id(1)

    @pl.when(j == 0)
    def _():
        x = x_ref[...]
        h_ref[...] = _rms(x, nw_ref[...]).astype(BF16)
        o_ref[...] = x

    h = h_ref[...]
    g = _dot(h, wg_ref[...])
    u = _dot(h, wu_ref[...])
    a = (0.5 * (g * jax.nn.sigmoid(g)) * u).astype(BF16)
    o_ref[...] += _dot(a, wd_ref[...])


def _ffn(x, nw, w_gu, w_down, l, *, tm=1024, tf=512):
    T, D = x.shape
    F = w_down.shape[1]
    nf = F // tf
    return pl.pallas_call(
        _ffn_body,
        grid=(T // tm, nf),
        in_specs=[
            pl.BlockSpec((tm, D), lambda i, j: (i, 0)),
            pl.BlockSpec((1, D), lambda i, j: (0, 0)),
            pl.BlockSpec((None, D, tf), lambda i, j: (l, 0, j)),
            pl.BlockSpec((None, D, tf), lambda i, j: (l, 0, j + nf)),
            pl.BlockSpec((None, tf, D), lambda i, j: (l, j, 0)),
        ],
        out_specs=pl.BlockSpec((tm, D), lambda i, j: (i, 0)),
        out_shape=jax.ShapeDtypeStruct((T, D), F32),
        scratch_shapes=[pltpu.VMEM((tm, D), BF16)],
        compiler_params=_cparams(("parallel", "arbitrary")),
        name="ffn",
    )(x, nw, w_gu, w_gu, w_down)


def _inproj_body(x_ref, nw_ref, w_ref, wdt_ref, fa_ref, fb_ref, p_ref, dt_ref, fa_o, fb_o, h_ref):
    j = pl.program_id(1)

    @pl.when(j == 0)
    def _():
        h = _rms(x_ref[...], nw_ref[...]).astype(BF16)
        h_ref[...] = h
        dt_ref[...] = _dot(h, wdt_ref[...])

    p_ref[...] = _dot(h_ref[...], w_ref[...])
    fa_o[...] = fa_ref[...].astype(BF16)
    fb_o[...] = fb_ref[...].astype(BF16)


def _in_proj(x, nw, w_main, w_dt, l, cast_a, cast_b, *, tm=1024, tn=1024):
    T, D = x.shape
    N = w_main.shape[2]
    assert N % tn == 0 and tn % LANES == 0
    ni, nj = T // tm, N // tn
    (_, Ra, Ca), (_, Rb, Cb) = cast_a.shape, cast_b.shape
    assert Ra % ni == 0 and Ca % nj == 0 and Rb % nj == 0 and Cb % ni == 0
    blk_a, blk_b = (Ra // ni, Ca // nj), (Rb // nj, Cb // ni)
    return pl.pallas_call(
        _inproj_body,
        grid=(ni, nj),
        in_specs=[
            pl.BlockSpec((tm, D), lambda i, j: (i, 0)),
            pl.BlockSpec((1, D), lambda i, j: (0, 0)),
            pl.BlockSpec((None, D, tn), lambda i, j: (l, 0, j)),
            pl.BlockSpec((None, D, LANES), lambda i, j: (l, 0, 0)),
            pl.BlockSpec((None,) + blk_a, lambda i, j: (l, i, j)),
            pl.BlockSpec((None,) + blk_b, lambda i, j: (l, j, i)),
        ],
        out_specs=[
            pl.BlockSpec((tm, tn), lambda i, j: (i, j)),
            pl.BlockSpec((tm, LANES), lambda i, j: (i, 0)),
            pl.BlockSpec((None,) + blk_a, lambda i, j: (0, i, j)),
            pl.BlockSpec((None,) + blk_b, lambda i, j: (0, j, i)),
        ],
        out_shape=[jax.ShapeDtypeStruct((T, N), F32), jax.ShapeDtypeStruct((T, LANES), F32),
                   jax.ShapeDtypeStruct((1, Ra, Ca), BF16), jax.ShapeDtypeStruct((1, Rb, Cb), BF16)],
        scratch_shapes=[pltpu.VMEM((tm, D), BF16)],
        compiler_params=_cparams(("parallel", "arbitrary")),
        name="in_proj",
    )(x, nw, w_main, w_dt, cast_a, cast_b)


MOBA_EXT = LANES
MOBA_SEL0 = 16
MOBA_VARIANT = 2
MOBA_QBLOCKS = 2


def _split3(x):
    hi = x.astype(BF16).astype(F32)
    r = x - hi
    mid = r.astype(BF16).astype(F32)
    lo = (r - mid).astype(BF16).astype(F32)
    return hi, mid, lo


def _rows16(rows, width):
    r_i = lax.broadcasted_iota(jnp.int32, (2 * SUBLANES, width), 0)
    out = jnp.zeros((2 * SUBLANES, width), F32)
    for k, row in enumerate(rows):
        out = jnp.where(r_i == k, row, out)
    return out


def _moba_body(q_ref, k_ref, v_ref, qn_ref, kn_ref, slope_ref, o_ref,
               kx_ref, vt_ref, kmean_ref, qx_ref, *, nb):
    qi = pl.program_id(2)
    blk = MOBA_BLOCK
    seq = nb * blk
    scale = ATTN_HD ** -0.5
    slope = slope_ref[0:1, 0:1]

    @pl.when((pl.program_id(0) == 0) & (pl.program_id(1) == 0) & (qi == 0))
    def _():
        pos = lax.broadcasted_iota(jnp.int32, (seq, MOBA_EXT), 0)
        c = lax.broadcasted_iota(jnp.int32, (seq, MOBA_EXT), 1)
        n_of = lax.shift_right_logical(pos, blk.bit_length() - 1)
        j_of = pos & (blk - 1)
        ext = jnp.where(c < 3, 1, jnp.where(c < 6, n_of, jnp.where(c < 9, j_of, 0)))
        ext = jnp.where(c - MOBA_SEL0 == n_of, 1, ext)
        kx_ref[:, ATTN_HD:] = ext.astype(F32).astype(BF16)

    @pl.when(qi == 0)
    def _():
        kf = _rms(k_ref[...], kn_ref[...])
        kx_ref[:, 0:ATTN_HD] = kf.astype(BF16)
        for n in range(nb):
            kmean_ref[n:n + 1, :] = jnp.mean(kf[n * blk:(n + 1) * blk], axis=0, keepdims=True)
        vt_ref[...] = v_ref[...].T.astype(BF16)

        qn = _rms(q_ref[...], qn_ref[...])
        gate = _dot_nt(kmean_ref[...], qn, precision=lax.Precision.HIGHEST)
        n_iota = lax.broadcasted_iota(jnp.int32, gate.shape, 0).astype(F32)
        t_i = lax.broadcasted_iota(jnp.int32, (1, seq), 1)
        past_blk = n_iota < lax.shift_right_logical(t_i, blk.bit_length() - 1).astype(F32)
        gate = jnp.where(past_blk, gate, -jnp.inf)
        selb = jnp.full(gate.shape, NEG, F32)
        for _ in range(MOBA_TOPK):
            best = jnp.max(gate, axis=0, keepdims=True)
            first = jnp.min(jnp.where(gate == best, n_iota, float(nb)), axis=0, keepdims=True)
            pick = n_iota == first
            selb = jnp.where(pick, 0.0, selb)
            gate = jnp.where(pick, -jnp.inf, gate)
        selb = jnp.where(past_blk, selb, NEG)
        slope2 = slope * LOG2E
        ones = jnp.ones((1, seq), F32)
        bias_rows = (list(_split3(-slope2 * t_i.astype(F32))) + [v * ones for v in _split3(slope2 * float(blk))]
                     + [v * ones for v in _split3(slope2)])
        qx_ref[0:ATTN_HD, :] = (qn * (scale * LOG2E)).T.astype(BF16)
        qx_ref[ATTN_HD:ATTN_HD + MOBA_SEL0, :] = _rows16(bias_rows, seq).astype(BF16)
        qx_ref[ATTN_HD + MOBA_SEL0:ATTN_HD + MOBA_SEL0 + nb, :] = selb.astype(BF16)
        qx_ref[ATTN_HD + MOBA_SEL0 + nb:, :] = jnp.zeros((MOBA_EXT - MOBA_SEL0 - nb, seq), BF16)

    qw = MOBA_QBLOCKS * blk
    qcols = pl.ds(pl.multiple_of(qi * qw, qw), qw)
    qx_past = qx_ref[:, qcols]
    qx_own = jnp.concatenate([qx_ref[0:ATTN_HD + MOBA_SEL0, qcols],
                              jnp.zeros((MOBA_EXT - MOBA_SEL0, qw), BF16)], axis=0)

    key_i = lax.broadcasted_iota(jnp.int32, (blk, blk), 0)
    qry_i = lax.broadcasted_iota(jnp.int32, (blk, blk), 1)
    lanes = lambda a, u: a[:, u * blk:(u + 1) * blk]
    row_cat = lambda parts: jnp.concatenate(parts, axis=1)

    def attend(n_past):
        n = n_past * blk
        owns = [pl.ds(pl.multiple_of((qi * MOBA_QBLOCKS + u) * blk, blk), blk) for u in range(MOBA_QBLOCKS)]
        s_own = [jnp.where(qry_i >= key_i, _dot(kx_ref[owns[u], :], lanes(qx_own, u)), NEG)
                 for u in range(MOBA_QBLOCKS)]
        h = (n_past // 2) * blk
        s_a = _dot(kx_ref[0:h, :], qx_past) if h else None
        s_b = _dot(kx_ref[h:n, :], qx_past)
        parts = []
        if h:
            m_a = jnp.max(s_a, axis=0, keepdims=True)
            p_a = jnp.exp2(s_a - m_a)
            parts.append((m_a, jnp.sum(p_a, axis=0, keepdims=True), _dot(vt_ref[:, 0:h], p_a.astype(BF16))))
        m_b = jnp.maximum(jnp.max(s_b, axis=0, keepdims=True),
                          row_cat([jnp.max(so, axis=0, keepdims=True) for so in s_own]))
        p_b = jnp.exp2(s_b - m_b)
        p_own = [jnp.exp2(s_own[u] - lanes(m_b, u)) for u in range(MOBA_QBLOCKS)]
        l_b = jnp.sum(p_b, axis=0, keepdims=True) + row_cat([jnp.sum(po, axis=0, keepdims=True) for po in p_own])
        acc_b = _dot(vt_ref[:, h:n], p_b.astype(BF16)) + row_cat(
            [_dot(vt_ref[:, owns[u]], p_own[u].astype(BF16)) for u in range(MOBA_QBLOCKS)])
        parts.append((m_b, l_b, acc_b))
        m = functools.reduce(jnp.maximum, [pm for pm, _, _ in parts])
        l = sum(jnp.exp2(pm - m) * pl_ for pm, pl_, _ in parts)
        acc = sum(jnp.exp2(pm - m) * pa for pm, _, pa in parts)
        o_ref[...] = (acc / l).T.astype(o_ref.dtype)

    per = MOBA_VARIANT // MOBA_QBLOCKS
    for v in range(nb // MOBA_VARIANT):
        pl.when((qi >= v * per) & (qi < (v + 1) * per))(
            functools.partial(attend, (v + 1) * MOBA_VARIANT - 1))


def _moba(p_main, q_norm, k_norm, slopes, *, batch, seq):
    T = p_main.shape[0]
    blk = MOBA_BLOCK
    nb = seq // blk
    assert seq % blk == 0 and nb % MOBA_VARIANT == 0 and nb <= MOBA_EXT - MOBA_SEL0
    assert MOBA_VARIANT % MOBA_QBLOCKS == 0
    H = ATTN_HEADS
    qw = MOBA_QBLOCKS * blk
    nq = seq // qw
    body = functools.partial(_moba_body, nb=nb)
    return pl.pallas_call(
        body,
        grid=(batch, H, nq),
        in_specs=[
            pl.BlockSpec((seq, ATTN_HD), lambda b, h, i: (b, h)),
            pl.BlockSpec((seq, ATTN_HD), lambda b, h, i: (b, H + h)),
            pl.BlockSpec((seq, ATTN_HD), lambda b, h, i: (b, 2 * H + h)),
            pl.BlockSpec((1, ATTN_HD), lambda b, h, i: (0, 0)),
            pl.BlockSpec((1, ATTN_HD), lambda b, h, i: (0, 0)),
            pl.BlockSpec((None, 1, LANES), lambda b, h, i: (h, 0, 0)),
        ],
        out_specs=pl.BlockSpec((qw, ATTN_HD), lambda b, h, i: (b * nq + i, h)),
        out_shape=jax.ShapeDtypeStruct((T, H * ATTN_HD), BF16),
        scratch_shapes=[
            pltpu.VMEM((seq, ATTN_HD + MOBA_EXT), BF16),
            pltpu.VMEM((ATTN_HD, seq), BF16),
            pltpu.VMEM((nb, ATTN_HD), F32),
            pltpu.VMEM((ATTN_HD + MOBA_EXT, seq), BF16),
        ],
        compiler_params=_cparams(("arbitrary", "arbitrary", "arbitrary")),
        name="moba",
    )(p_main, p_main, p_main, q_norm, k_norm, slopes)


def _softplus(x):
    return jnp.maximum(x, 0.0) + jnp.log1p(jnp.exp(-jnp.abs(x)))


def _split_hi_lo(x):
    hi = x.astype(BF16)
    lo = (x - hi.astype(F32)).astype(BF16)
    return jnp.concatenate([hi, lo], axis=1)


def _ssd_dt_body(dt_ref, dtb_ref, alog_ref, dts_ref, acss_ref, acst_ref, *, rows):
    L = SSD_CHUNK
    li = lax.broadcasted_iota(jnp.int32, (L, L), 0)
    si = lax.broadcasted_iota(jnp.int32, (L, L), 1)
    tril = (li >= si).astype(F32)
    a2_row = -jnp.exp(alog_ref[...]) * LOG2E
    for c in range(rows // L):
        rs = slice(c * L, (c + 1) * L)
        dtv = _softplus(dt_ref[rs, :] + dtb_ref[...])
        acs = jnp.dot(tril, dtv * a2_row, preferred_element_type=F32,
                      precision=lax.Precision.HIGHEST)
        dts_ref[rs, :] = _split_hi_lo(dtv)
        acss_ref[rs, :] = _split_hi_lo(acs)
        acst_ref[c] = acs.T


def _ssd_dt(dt_raw, dt_bias_p, a_log_p, *, rows=1024):
    T = dt_raw.shape[0]
    L = SSD_CHUNK
    body = functools.partial(_ssd_dt_body, rows=rows)
    return pl.pallas_call(
        body,
        grid=(T // rows,),
        in_specs=[
            pl.BlockSpec((rows, LANES), lambda i: (i, 0)),
            pl.BlockSpec((1, LANES), lambda i: (0, 0)),
            pl.BlockSpec((1, LANES), lambda i: (0, 0)),
        ],
        out_specs=[
            pl.BlockSpec((rows, 2 * LANES), lambda i: (i, 0)),
            pl.BlockSpec((rows, 2 * LANES), lambda i: (i, 0)),
            pl.BlockSpec((rows // L, LANES, L), lambda i: (i, 0, 0)),
        ],
        out_shape=[jax.ShapeDtypeStruct((T, 2 * LANES), BF16), jax.ShapeDtypeStruct((T, 2 * LANES), BF16),
                   jax.ShapeDtypeStruct((T // L, LANES, L), F32)],
        compiler_params=_cparams(("parallel",)),
        name="ssd_dt",
    )(dt_raw, dt_bias_p, a_log_p)


def _ssd_body(xs_ref, bm_ref, cm_ref, xs_h_ref, bm_h_ref, cm_h_ref, z_ref, dts_ref, acss_ref, acst_ref,
              wx_ref, wb_ref, wc_ref, bx_ref, bb_ref, bc_ref, dsk_ref, nw_ref, *rest, rows, with_cast):
    if with_cast:
        fa_ref, fb_ref, o_ref, fa_o, fb_o, ux_ref, ub_ref, uc_ref, st_ref = rest
        fa_o[...] = fa_ref[...].astype(BF16)
        fb_o[...] = fb_ref[...].astype(BF16)
    else:
        o_ref, ux_ref, ub_ref, uc_ref, st_ref = rest
    g = pl.program_id(1)
    r = pl.program_id(2)
    L = SSD_CHUNK
    P = SSD_HD
    W = SSD_HPG * P
    HALO = SUBLANES

    @pl.when(r == 0)
    def _():
        st_ref[...] = jnp.zeros_like(st_ref)

    first = r == 0

    def conv(u_ref, halo_ref, buf_ref, w_ref, b_ref):
        halo = halo_ref[...]
        buf_ref[0:HALO, :] = jnp.where(first, jnp.zeros_like(halo), halo)
        buf_ref[HALO:, :] = u_ref[...]
        acc = b_ref[...] + w_ref[SSD_CONV - 1:SSD_CONV, :] * u_ref[...]
        for k in range(SSD_CONV - 1):
            shift = SSD_CONV - 1 - k
            acc = acc + w_ref[k:k + 1, :] * buf_ref[HALO - shift:HALO - shift + rows, :]
        buf_ref[HALO:, :] = acc * jax.nn.sigmoid(acc)

    conv(xs_ref, xs_h_ref, ux_ref, wx_ref, bx_ref)
    conv(bm_ref, bm_h_ref, ub_ref, wb_ref, bb_ref)
    conv(cm_ref, cm_h_ref, uc_ref, wc_ref, bc_ref)

    k_i = lax.broadcasted_iota(jnp.int32, (2 * LANES, W), 0)
    c_i = lax.broadcasted_iota(jnp.int32, (2 * LANES, W), 1)
    head = g * SSD_HPG + lax.shift_right_logical(c_i, P.bit_length() - 1)
    expand = ((k_i == head) | (k_i == head + LANES)).astype(BF16)

    li = lax.broadcasted_iota(jnp.int32, (L, L), 0)
    si = lax.broadcasted_iota(jnp.int32, (L, L), 1)
    causal = li >= si
    lane = lax.broadcasted_iota(jnp.int32, (L, LANES), 1)

    for c in range(rows // L):
        r0 = c * L
        rs = slice(HALO + r0, HALO + r0 + L)
        dt_x = _dot(dts_ref[r0:r0 + L, :], expand)
        acs_x = _dot(acss_ref[r0:r0 + L, :], expand)
        acs_last = acs_x[L - 1:L, :]

        xs = ux_ref[rs, :]
        bmat = ub_ref[rs, :]
        cmat = uc_ref[rs, :]
        x = xs * dt_x
        xb = x.astype(BF16)
        cb16 = cmat.astype(BF16)
        cb = jnp.where(causal, _dot_nt(cb16, bmat.astype(BF16)), 0.0)

        tiles = []
        for m in range(W // LANES):
            res = []
            for jj in range(2):
                j = 2 * m + jj
                col = acs_x[:, j * P:j * P + 1]
                row = acst_ref[c, pl.ds(g * SSD_HPG + j, 1), :]
                dec = jnp.exp2(jnp.minimum(col - row, 0.0))
                sc = (cb * dec).astype(BF16)
                res.append(_dot(sc, xb[:, m * LANES:(m + 1) * LANES]))
            tiles.append(jnp.where(lane < P, res[0], res[1]))
        y = jnp.concatenate(tiles, axis=1)

        st = st_ref[...]
        y = y + _dot(cb16, st.astype(BF16)) * jnp.exp2(acs_x)
        xd = (x * jnp.exp2(acs_last - acs_x)).astype(BF16)
        st_ref[...] = st * jnp.exp2(acs_last) + _dot(bmat.T.astype(BF16), xd)

        y = y + dsk_ref[...] * xs
        z = z_ref[r0:r0 + L, :]
        gt = y * (z * jax.nn.sigmoid(z))
        gn = gt * lax.rsqrt(jnp.mean(gt * gt, axis=-1, keepdims=True) + EPS)
        o_ref[r0:r0 + L, :] = (gn * nw_ref[...]).astype(o_ref.dtype)


def _ssd(p_main, dts, acss, acst, conv_w, conv_b, d_skip_x, ssd_norm, cast=None, *, batch, seq, rows=1024):
    T = p_main.shape[0]
    G = SSD_GROUPS
    N = SSD_STATE
    W = SSD_HPG * SSD_HD
    nr = seq // rows
    hb = rows // SUBLANES
    z0, x0 = 3 * ATTN_HEADS * ATTN_HD // W, (3 * ATTN_HEADS * ATTN_HD + G * W) // W
    b0 = (3 * ATTN_HEADS * ATTN_HD + 2 * G * W) // N
    c0 = b0 + G
    cwb0, cwc0 = G * W // N, G * W // N + G

    def rowblk(b, g, r):
        return b * nr + r

    def halo(b, g, r):
        return jnp.maximum((b * nr + r) * hb - 1, 0)

    cast_in, cast_specs, cast_shapes = [], [], []
    if cast is not None:
        cast_a, cast_b, lc = cast
        (_, Ra, Ca), (_, Rb, Cb) = cast_a.shape, cast_b.shape
        n_bg = batch * G
        assert Ra % n_bg == 0 and Ca % nr == 0 and Rb % nr == 0 and Cb % n_bg == 0
        blk_a, blk_b = (Ra // n_bg, Ca // nr), (Rb // nr, Cb // n_bg)
        cast_in = [pl.BlockSpec((None,) + blk_a, lambda b, g, r: (lc, b * G + g, r)),
                   pl.BlockSpec((None,) + blk_b, lambda b, g, r: (lc, r, b * G + g))]
        cast_specs = [pl.BlockSpec((None,) + blk_a, lambda b, g, r: (0, b * G + g, r)),
                      pl.BlockSpec((None,) + blk_b, lambda b, g, r: (0, r, b * G + g))]
        cast_shapes = [jax.ShapeDtypeStruct((1, Ra, Ca), BF16), jax.ShapeDtypeStruct((1, Rb, Cb), BF16)]

    body = functools.partial(_ssd_body, rows=rows, with_cast=cast is not None)
    outs = pl.pallas_call(
        body,
        grid=(batch, G, nr),
        in_specs=[
            pl.BlockSpec((rows, W), lambda b, g, r: (rowblk(b, g, r), x0 + g)),
            pl.BlockSpec((rows, N), lambda b, g, r: (rowblk(b, g, r), b0 + g)),
            pl.BlockSpec((rows, N), lambda b, g, r: (rowblk(b, g, r), c0 + g)),
            pl.BlockSpec((SUBLANES, W), lambda b, g, r: (halo(b, g, r), x0 + g)),
            pl.BlockSpec((SUBLANES, N), lambda b, g, r: (halo(b, g, r), b0 + g)),
            pl.BlockSpec((SUBLANES, N), lambda b, g, r: (halo(b, g, r), c0 + g)),
            pl.BlockSpec((rows, W), lambda b, g, r: (rowblk(b, g, r), z0 + g)),
            pl.BlockSpec((rows, 2 * LANES), lambda b, g, r: (rowblk(b, g, r), 0)),
            pl.BlockSpec((rows, 2 * LANES), lambda b, g, r: (rowblk(b, g, r), 0)),
            pl.BlockSpec((rows // SSD_CHUNK, LANES, SSD_CHUNK), lambda b, g, r: (rowblk(b, g, r), 0, 0)),
            pl.BlockSpec((SSD_CONV, W), lambda b, g, r: (0, g)),
            pl.BlockSpec((SSD_CONV, N), lambda b, g, r: (0, cwb0 + g)),
            pl.BlockSpec((SSD_CONV, N), lambda b, g, r: (0, cwc0 + g)),
            pl.BlockSpec((1, W), lambda b, g, r: (0, g)),
            pl.BlockSpec((1, N), lambda b, g, r: (0, cwb0 + g)),
            pl.BlockSpec((1, N), lambda b, g, r: (0, cwc0 + g)),
            pl.BlockSpec((1, W), lambda b, g, r: (0, g)),
            pl.BlockSpec((1, W), lambda b, g, r: (0, g)),
        ] + cast_in,
        out_specs=[pl.BlockSpec((rows, W), lambda b, g, r: (rowblk(b, g, r), g))] + cast_specs,
        out_shape=[jax.ShapeDtypeStruct((T, G * W), BF16)] + cast_shapes,
        scratch_shapes=[
            pltpu.VMEM((SUBLANES + rows, W), F32),
            pltpu.VMEM((SUBLANES + rows, N), F32),
            pltpu.VMEM((SUBLANES + rows, N), F32),
            pltpu.VMEM((N, W), F32),
        ],
        compiler_params=_cparams(("parallel", "parallel", "arbitrary")),
        name="ssd",
    )(p_main, p_main, p_main, p_main, p_main, p_main, p_main, dts, acss, acst,
      conv_w, conv_w, conv_w, conv_b, conv_b, conv_b, d_skip_x, ssd_norm,
      *([cast[0], cast[1]] if cast is not None else []))
    return outs[0], outs[1:]


def _outproj_body(x_ref, ya_ref, ys_ref, w_ref, o_ref):
    ka = ya_ref.shape[1]
    o_ref[...] = x_ref[...] + _dot(ya_ref[...], w_ref[0:ka, :]) + _dot(ys_ref[...], w_ref[ka:, :])


def _out_proj(x, ya, ys, w_out, l, *, tm=512, tn=1024):
    T, D = x.shape
    Ka, Ks = ya.shape[1], ys.shape[1]
    return pl.pallas_call(
        _outproj_body,
        grid=(D // tn, T // tm),
        in_specs=[
            pl.BlockSpec((tm, tn), lambda j, i: (i, j)),
            pl.BlockSpec((tm, Ka), lambda j, i: (i, 0)),
            pl.BlockSpec((tm, Ks), lambda j, i: (i, 0)),
            pl.BlockSpec((None, Ka + Ks, tn), lambda j, i: (l, 0, j)),
        ],
        out_specs=pl.BlockSpec((tm, tn), lambda j, i: (i, j)),
        out_shape=jax.ShapeDtypeStruct((T, D), F32),
        compiler_params=_cparams(("parallel", "parallel")),
        name="out_proj",
    )(x, ya, ys, w_out)


def _memkv_body(m_ref, nw_ref, wk_ref, wv_ref, kn_ref, k_ref, v_ref):
    h = _rms(m_ref[...], nw_ref[...]).astype(BF16)
    k = _dot(h, wk_ref[...])
    v_ref[...] = _dot(h, wv_ref[...]).astype(BF16)
    for hh in range(MEM_HEADS):
        sl = slice(hh * MEM_HD, (hh + 1) * MEM_HD)
        k_ref[:, sl] = _rms(k[:, sl], kn_ref[...]).astype(BF16)


def _mem_kv(mem, nw, wk, wv, kn, l):
    M, D = mem.shape
    Wm = wk.shape[2]
    full = lambda shape: pl.BlockSpec(shape, lambda i: (0,) * len(shape))
    layer = lambda shape: pl.BlockSpec((None,) + shape, lambda i: (l,) + (0,) * len(shape))
    return pl.pallas_call(
        _memkv_body,
        grid=(1,),
        in_specs=[full((M, D)), full((1, D)), layer((D, Wm)), layer((D, Wm)), full((1, MEM_HD))],
        out_specs=[full((M, Wm)), full((M, Wm))],
        out_shape=[jax.ShapeDtypeStruct((M, Wm), BF16)] * 2,
        compiler_params=_cparams(("arbitrary",)),
        name="mem_kv",
    )(mem, nw, wk, wv, kn)


def _memattn_body(x_ref, nw_ref, wq_ref, qn_ref, k_ref, v_ref, wo_ref, o_ref):
    x = x_ref[...]
    h = _rms(x, nw_ref[...]).astype(BF16)
    q = _dot(h, wq_ref[...])
    scale = MEM_HD ** -0.5
    outs = []
    for hh in range(MEM_HEADS):
        sl = slice(hh * MEM_HD, (hh + 1) * MEM_HD)
        qh = _rms(q[:, sl], qn_ref[...]).astype(BF16)
        s = _dot_nt(qh, k_ref[:, sl]) * scale
        e = jnp.exp(s - jnp.max(s, axis=-1, keepdims=True))
        p = e / jnp.sum(e, axis=-1, keepdims=True)
        outs.append(_dot(p.astype(BF16), v_ref[:, sl]))
    o = jnp.concatenate(outs, axis=1).astype(BF16)
    o_ref[...] = x + _dot(o, wo_ref[...])


def _mem_attn(x, nw, wq, qn, k, v, wo, l, *, seq, mem_len, tm=512):
    T, D = x.shape
    Wm = wq.shape[2]
    per_b = seq // tm
    return pl.pallas_call(
        _memattn_body,
        grid=(T // tm,),
        in_specs=[
            pl.BlockSpec((tm, D), lambda i: (i, 0)),
            pl.BlockSpec((1, D), lambda i: (0, 0)),
            pl.BlockSpec((None, D, Wm), lambda i: (l, 0, 0)),
            pl.BlockSpec((1, MEM_HD), lambda i: (0, 0)),
            pl.BlockSpec((mem_len, Wm), lambda i: (i // per_b, 0)),
            pl.BlockSpec((mem_len, Wm), lambda i: (i // per_b, 0)),
            pl.BlockSpec((None, Wm, D), lambda i: (l, 0, 0)),
        ],
        out_specs=pl.BlockSpec((tm, D), lambda i: (i, 0)),
        out_shape=jax.ShapeDtypeStruct((T, D), F32),
        compiler_params=_cparams(("parallel",)),
        name="mem_attn",
    )(x, nw, wq, qn, k, v, wo)


def _pad_lanes(v):
    return jnp.pad(v.astype(F32), (0, LANES - v.shape[0])).reshape(1, LANES)


def kernel(x, mem, ff1_norm, ff1_w_gu, ff1_w_down, mix_norm, w_in, q_norm, k_norm, conv_w, conv_b,
           dt_bias, a_log, d_skip, ssd_norm, w_out, xmem_norm, mem_norm, mem_wq, mem_wk, mem_wv,
           mem_q_norm, mem_k_norm, mem_wo, ff2_norm, ff2_w_gu, ff2_w_down):
    B, S, D = x.shape
    M = mem.shape[1]
    depth = w_in.shape[0]
    n_dt = SSD_GROUPS * SSD_HPG
    row = lambda v: v.reshape(1, -1)

    slopes = jnp.exp2(-8.0 * jnp.arange(1, ATTN_HEADS + 1, dtype=F32) / ATTN_HEADS)
    slopes = jnp.broadcast_to(slopes[:, None, None], (ATTN_HEADS, 1, LANES))
    bf = _cast_bf16
    ff1_bf = (bf(ff1_w_gu, layer=0), bf(ff1_w_down, layer=0))
    n_main = w_in.shape[2] - n_dt
    w_main = w_in[:, :, :n_main].astype(BF16)
    w_dt = jnp.pad(w_in[:, :, n_main:], ((0, 0), (0, 0), (0, LANES - n_dt))).astype(BF16)
    w_out = bf(w_out)
    mem_wq, mem_wk, mem_wv, mem_wo = bf(mem_wq), bf(mem_wk), bf(mem_wv), bf(mem_wo)

    xf = x.reshape(B * S, D)
    memf = mem.reshape(B * M, D)
    for l in range(depth):
        xf = _ffn(xf, row(ff1_norm[l]), ff1_bf[0], ff1_bf[1], 0)

        p_main, dt_raw, w2_gu, w2_down = _in_proj(xf, row(mix_norm[l]), w_main, w_dt, l, ff2_w_gu, ff2_w_down)
        y_attn = _moba(p_main, row(q_norm[l]), row(k_norm[l]), slopes, batch=B, seq=S)
        dts, acss, acst = _ssd_dt(dt_raw, _pad_lanes(dt_bias[l]), _pad_lanes(a_log[l]))
        nxt = (ff1_w_gu, ff1_w_down, l + 1) if l + 1 < depth else None
        y_ssd, ff1_next = _ssd(p_main, dts, acss, acst, conv_w[l], row(conv_b[l]),
                               row(jnp.repeat(d_skip[l], SSD_HD)), row(ssd_norm[l]), nxt, batch=B, seq=S)
        ff1_bf = ff1_next or ff1_bf
        xf = _out_proj(xf, y_attn, y_ssd, w_out, l)

        k_mem, v_mem = _mem_kv(memf, row(mem_norm[l]), mem_wk, mem_wv, row(mem_k_norm[l]), l)
        xf = _mem_attn(xf, row(xmem_norm[l]), mem_wq, row(mem_q_norm[l]), k_mem, v_mem, mem_wo, l,
                       seq=S, mem_len=M)

        xf = _ffn(xf, row(ff2_norm[l]), w2_gu, w2_down, 0)
    return xf.reshape(B, S, D)
```

```python
import functools

import jax
import jax.numpy as jnp
from jax import lax
from jax.experimental import pallas as pl
from jax.experimental.pallas import tpu as pltpu

F32 = jnp.float32
BF16 = jnp.bfloat16
EPS = 1e-6
NEG = -1e30
LOG2E = 1.4426950408889634

LANES = 128
SUBLANES = 8
VMEM_LIMIT = 56 * 1024 * 1024

ATTN_HEADS = 8
ATTN_HD = 128
MOBA_BLOCK = 256
MOBA_TOPK = 3
SSD_HD = 64
SSD_GROUPS = 8
SSD_HPG = 6
SSD_STATE = 128
SSD_CONV = 4
SSD_CHUNK = 128
MEM_HEADS = 4
MEM_HD = 128


def _cparams(sem):
    return pltpu.CompilerParams(dimension_semantics=sem, vmem_limit_bytes=VMEM_LIMIT)


def _rms(x, w):
    return x * lax.rsqrt(jnp.mean(x * x, axis=-1, keepdims=True) + EPS) * w


def _dot(a, b):
    return jnp.dot(a, b, preferred_element_type=F32)


def _dot_nt(a, b, precision=None):
    return lax.dot_general(a, b, (((1,), (1,)), ((), ())), preferred_element_type=F32,
                           precision=precision)


CAST_BLOCK_BYTES = 12 * 1024 * 1024


def _cast_body(w_ref, o_ref):
    o_ref[...] = w_ref[...].astype(BF16)


def _cast_bf16(w, layer=None):
    nl, R, C = w.shape
    first = 0
    if layer is not None:
        nl, first = 1, layer
    tr = R
    while tr * C * 4 > CAST_BLOCK_BYTES and tr % 32 == 0:
        tr //= 2
    return pl.pallas_call(
        _cast_body,
        grid=(nl, R // tr),
        in_specs=[pl.BlockSpec((None, tr, C), lambda l, i: (first + l, i, 0))],
        out_specs=pl.BlockSpec((None, tr, C), lambda l, i: (l, i, 0)),
        out_shape=jax.ShapeDtypeStruct((nl, R, C), BF16),
        compiler_params=_cparams(("parallel", "parallel")),
        name="cast_bf16",
    )(w)


def _ffn_body(x_ref, nw_ref, wg_ref, wu_ref, wd_ref, o_ref, h_ref):
    j = pl.program_id(1)

    @pl.when(j == 0)
    def _():
        x = x_ref[...]
        h_ref[...] = _rms(x, nw_ref[...]).astype(BF16)
        o_ref[...] = x

    h = h_ref[...]
    g = _dot(h, wg_ref[...])
    u = _dot(h, wu_ref[...])
    a = (0.5 * (g * jax.nn.sigmoid(g)) * u).astype(BF16)
    o_ref[...] += _dot(a, wd_ref[...])


def _ffn(x, nw, w_gu, w_down, l, *, tm=1024, tf=512):
    T, D = x.shape
    F = w_down.shape[1]
    nf = F // tf
    return pl.pallas_call(
        _ffn_body,
        grid=(T // tm, nf),
        in_specs=[
            pl.BlockSpec((tm, D), lambda i, j: (i, 0)),
            pl.BlockSpec((1, D), lambda i, j: (0, 0)),
            pl.BlockSpec((None, D, tf), lambda i, j: (l, 0, j)),
            pl.BlockSpec((None, D, tf), lambda i, j: (l, 0, j + nf)),
            pl.BlockSpec((None, tf, D), lambda i, j: (l, j, 0)),
        ],
        out_specs=pl.BlockSpec((tm, D), lambda i, j: (i, 0)),
        out_shape=jax.ShapeDtypeStruct((T, D), F32),
        scratch_shapes=[pltpu.VMEM((tm, D), BF16)],
        compiler_params=_cparams(("parallel", "arbitrary")),
        name="ffn",
    )(x, nw, w_gu, w_gu, w_down)


def _inproj_body(x_ref, nw_ref, w_ref, wdt_ref, fa_ref, fb_ref, p_ref, dt_ref, fa_o, fb_o, h_ref):
    j = pl.program_id(1)

    @pl.when(j == 0)
    def _():
        h = _rms(x_ref[...], nw_ref[...]).astype(BF16)
        h_ref[...] = h
        dt_ref[...] = _dot(h, wdt_ref[...])

    p_ref[...] = _dot(h_ref[...], w_ref[...])
    fa_o[...] = fa_ref[...].astype(BF16)
    fb_o[...] = fb_ref[...].astype(BF16)


def _in_proj(x, nw, w_main, w_dt, l, cast_a, cast_b, lc, *, tm=1024, tn=1024):
    T, D = x.shape
    N = w_main.shape[2]
    assert N % tn == 0 and tn % LANES == 0
    ni, nj = T // tm, N // tn
    (_, Ra, Ca), (_, Rb, Cb) = cast_a.shape, cast_b.shape
    assert Ra % ni == 0 and Ca % nj == 0 and Rb % nj == 0 and Cb % ni == 0
    blk_a, blk_b = (Ra // ni, Ca // nj), (Rb // nj, Cb // ni)
    return pl.pallas_call(
        _inproj_body,
        grid=(ni, nj),
        in_specs=[
            pl.BlockSpec((tm, D), lambda i, j: (i, 0)),
            pl.BlockSpec((1, D), lambda i, j: (0, 0)),
            pl.BlockSpec((None, D, tn), lambda i, j: (l, 0, j)),
            pl.BlockSpec((None, D, LANES), lambda i, j: (l, 0, 0)),
            pl.BlockSpec((None,) + blk_a, lambda i, j: (lc, i, j)),
            pl.BlockSpec((None,) + blk_b, lambda i, j: (lc, j, i)),
        ],
        out_specs=[
            pl.BlockSpec((tm, tn), lambda i, j: (i, j)),
            pl.BlockSpec((tm, LANES), lambda i, j: (i, 0)),
            pl.BlockSpec((None,) + blk_a, lambda i, j: (0, i, j)),
            pl.BlockSpec((None,) + blk_b, lambda i, j: (0, j, i)),
        ],
        out_shape=[jax.ShapeDtypeStruct((T, N), F32), jax.ShapeDtypeStruct((T, LANES), F32),
                   jax.ShapeDtypeStruct((1, Ra, Ca), BF16), jax.ShapeDtypeStruct((1, Rb, Cb), BF16)],
        scratch_shapes=[pltpu.VMEM((tm, D), BF16)],
        compiler_params=_cparams(("parallel", "arbitrary")),
        name="in_proj",
    )(x, nw, w_main, w_dt, cast_a, cast_b)


MOBA_EXT = LANES
MOBA_SEL0 = 16
MOBA_VARIANT = 2
MOBA_QBLOCKS = 2


def _split3(x):
    hi = x.astype(BF16).astype(F32)
    r = x - hi
    mid = r.astype(BF16).astype(F32)
    lo = (r - mid).astype(BF16).astype(F32)
    return hi, mid, lo


def _rows16(rows, width):
    r_i = lax.broadcasted_iota(jnp.int32, (2 * SUBLANES, width), 0)
    out = jnp.zeros((2 * SUBLANES, width), F32)
    for k, row in enumerate(rows):
        out = jnp.where(r_i == k, row, out)
    return out


def _moba_body(q_ref, k_ref, v_ref, qn_ref, kn_ref, slope_ref, *rest, nb, n_cast):
    cast_in, rest = rest[:n_cast], rest[n_cast:]
    o_ref, cast_out, rest = rest[0], rest[1:1 + n_cast], rest[1 + n_cast:]
    for src, dst in zip(cast_in, cast_out):
        dst[...] = src[...].astype(BF16)
    _moba_attend(q_ref, k_ref, v_ref, qn_ref, kn_ref, slope_ref, o_ref, *rest, nb=nb)


def _moba_attend(q_ref, k_ref, v_ref, qn_ref, kn_ref, slope_ref, o_ref,
                 kx_ref, vt_ref, kmean_ref, qx_ref, *, nb):
    qi = pl.program_id(2)
    blk = MOBA_BLOCK
    seq = nb * blk
    scale = ATTN_HD ** -0.5
    slope = slope_ref[0:1, 0:1]

    @pl.when((pl.program_id(0) == 0) & (pl.program_id(1) == 0) & (qi == 0))
    def _():
        pos = lax.broadcasted_iota(jnp.int32, (seq, MOBA_EXT), 0)
        c = lax.broadcasted_iota(jnp.int32, (seq, MOBA_EXT), 1)
        n_of = lax.shift_right_logical(pos, blk.bit_length() - 1)
        j_of = pos & (blk - 1)
        ext = jnp.where(c < 3, 1, jnp.where(c < 6, n_of, jnp.where(c < 9, j_of, 0)))
        ext = jnp.where(c - MOBA_SEL0 == n_of, 1, ext)
        kx_ref[:, ATTN_HD:] = ext.astype(F32).astype(BF16)

    @pl.when(qi == 0)
    def _():
        kf = _rms(k_ref[...], kn_ref[...])
        kx_ref[:, 0:ATTN_HD] = kf.astype(BF16)
        for n in range(nb):
            kmean_ref[n:n + 1, :] = jnp.mean(kf[n * blk:(n + 1) * blk], axis=0, keepdims=True)
        vt_ref[...] = v_ref[...].T.astype(BF16)

        qn = _rms(q_ref[...], qn_ref[...])
        gate = _dot_nt(kmean_ref[...], qn, precision=lax.Precision.HIGHEST)
        n_iota = lax.broadcasted_iota(jnp.int32, gate.shape, 0).astype(F32)
        t_i = lax.broadcasted_iota(jnp.int32, (1, seq), 1)
        past_blk = n_iota < lax.shift_right_logical(t_i, blk.bit_length() - 1).astype(F32)
        gate = jnp.where(past_blk, gate, -jnp.inf)
        selb = jnp.full(gate.shape, NEG, F32)
        for _ in range(MOBA_TOPK):
            best = jnp.max(gate, axis=0, keepdims=True)
            first = jnp.min(jnp.where(gate == best, n_iota, float(nb)), axis=0, keepdims=True)
            pick = n_iota == first
            selb = jnp.where(pick, 0.0, selb)
            gate = jnp.where(pick, -jnp.inf, gate)
        selb = jnp.where(past_blk, selb, NEG)
        slope2 = slope * LOG2E
        ones = jnp.ones((1, seq), F32)
        bias_rows = (list(_split3(-slope2 * t_i.astype(F32))) + [v * ones for v in _split3(slope2 * float(blk))]
                     + [v * ones for v in _split3(slope2)])
        qx_ref[0:ATTN_HD, :] = (qn * (scale * LOG2E)).T.astype(BF16)
        qx_ref[ATTN_HD:ATTN_HD + MOBA_SEL0, :] = _rows16(bias_rows, seq).astype(BF16)
        qx_ref[ATTN_HD + MOBA_SEL0:ATTN_HD + MOBA_SEL0 + nb, :] = selb.astype(BF16)
        qx_ref[ATTN_HD + MOBA_SEL0 + nb:, :] = jnp.zeros((MOBA_EXT - MOBA_SEL0 - nb, seq), BF16)

    qw = MOBA_QBLOCKS * blk
    qcols = pl.ds(pl.multiple_of(qi * qw, qw), qw)
    qx_past = qx_ref[:, qcols]
    qx_own = jnp.concatenate([qx_ref[0:ATTN_HD + MOBA_SEL0, qcols],
                              jnp.zeros((MOBA_EXT - MOBA_SEL0, qw), BF16)], axis=0)

    key_i = lax.broadcasted_iota(jnp.int32, (blk, blk), 0)
    qry_i = lax.broadcasted_iota(jnp.int32, (blk, blk), 1)
    lanes = lambda a, u: a[:, u * blk:(u + 1) * blk]
    row_cat = lambda parts: jnp.concatenate(parts, axis=1)

    def attend(n_past):
        n = n_past * blk
        owns = [pl.ds(pl.multiple_of((qi * MOBA_QBLOCKS + u) * blk, blk), blk) for u in range(MOBA_QBLOCKS)]
        s_own = [jnp.where(qry_i >= key_i, _dot(kx_ref[owns[u], :], lanes(qx_own, u)), NEG)
                 for u in range(MOBA_QBLOCKS)]
        h = (n_past // 2) * blk
        s_a = _dot(kx_ref[0:h, :], qx_past) if h else None
        s_b = _dot(kx_ref[h:n, :], qx_past)
        parts = []
        if h:
            m_a = jnp.max(s_a, axis=0, keepdims=True)
            p_a = jnp.exp2(s_a - m_a)
            parts.append((m_a, jnp.sum(p_a, axis=0, keepdims=True), _dot(vt_ref[:, 0:h], p_a.astype(BF16))))
        m_b = jnp.maximum(jnp.max(s_b, axis=0, keepdims=True),
                          row_cat([jnp.max(so, axis=0, keepdims=True) for so in s_own]))
        p_b = jnp.exp2(s_b - m_b)
        p_own = [jnp.exp2(s_own[u] - lanes(m_b, u)) for u in range(MOBA_QBLOCKS)]
        l_b = jnp.sum(p_b, axis=0, keepdims=True) + row_cat([jnp.sum(po, axis=0, keepdims=True) for po in p_own])
        acc_b = _dot(vt_ref[:, h:n], p_b.astype(BF16)) + row_cat(
            [_dot(vt_ref[:, owns[u]], p_own[u].astype(BF16)) for u in range(MOBA_QBLOCKS)])
        parts.append((m_b, l_b, acc_b))
        m = functools.reduce(jnp.maximum, [pm for pm, _, _ in parts])
        l = sum(jnp.exp2(pm - m) * pl_ for pm, pl_, _ in parts)
        acc = sum(jnp.exp2(pm - m) * pa for pm, _, pa in parts)
        o_ref[...] = (acc / l).T.astype(o_ref.dtype)

    per = MOBA_VARIANT // MOBA_QBLOCKS
    for v in range(nb // MOBA_VARIANT):
        pl.when((qi >= v * per) & (qi < (v + 1) * per))(
            functools.partial(attend, (v + 1) * MOBA_VARIANT - 1))


def _moba(p_main, q_norm, k_norm, slopes, casts=(), *, batch, seq):
    T = p_main.shape[0]
    blk = MOBA_BLOCK
    nb = seq // blk
    assert seq % blk == 0 and nb % MOBA_VARIANT == 0 and nb <= MOBA_EXT - MOBA_SEL0
    assert MOBA_VARIANT % MOBA_QBLOCKS == 0
    H = ATTN_HEADS
    qw = MOBA_QBLOCKS * blk
    nq = seq // qw
    cast_in, cast_out, cast_shapes = [], [], []
    for w, lc, C in casts:
        R = w.shape[1]
        assert R % (batch * H) == 0 and C % nq == 0 and (C // nq) % LANES == 0
        cblk = (None, R // (batch * H), C // nq)
        cast_in.append(pl.BlockSpec(cblk, lambda b, h, i, lc=lc: (lc, b * H + h, i)))
        cast_out.append(pl.BlockSpec(cblk, lambda b, h, i: (0, b * H + h, i)))
        cast_shapes.append(jax.ShapeDtypeStruct((1, R, C), BF16))
    body = functools.partial(_moba_body, nb=nb, n_cast=len(casts))
    outs = pl.pallas_call(
        body,
        grid=(batch, H, nq),
        in_specs=[
            pl.BlockSpec((seq, ATTN_HD), lambda b, h, i: (b, h)),
            pl.BlockSpec((seq, ATTN_HD), lambda b, h, i: (b, H + h)),
            pl.BlockSpec((seq, ATTN_HD), lambda b, h, i: (b, 2 * H + h)),
            pl.BlockSpec((1, ATTN_HD), lambda b, h, i: (0, 0)),
            pl.BlockSpec((1, ATTN_HD), lambda b, h, i: (0, 0)),
            pl.BlockSpec((None, 1, LANES), lambda b, h, i: (h, 0, 0)),
        ] + cast_in,
        out_specs=[pl.BlockSpec((qw, ATTN_HD), lambda b, h, i: (b * nq + i, h))] + cast_out,
        out_shape=[jax.ShapeDtypeStruct((T, H * ATTN_HD), BF16)] + cast_shapes,
        scratch_shapes=[
            pltpu.VMEM((seq, ATTN_HD + MOBA_EXT), BF16),
            pltpu.VMEM((ATTN_HD, seq), BF16),
            pltpu.VMEM((nb, ATTN_HD), F32),
            pltpu.VMEM((ATTN_HD + MOBA_EXT, seq), BF16),
        ],
        compiler_params=_cparams(("arbitrary", "arbitrary", "arbitrary")),
        name="moba",
    )(p_main, p_main, p_main, q_norm, k_norm, slopes, *[w for w, _, _ in casts])
    return outs[0], outs[1:]


def _softplus(x):
    return jnp.maximum(x, 0.0) + jnp.log1p(jnp.exp(-jnp.abs(x)))


def _split_hi_lo(x):
    hi = x.astype(BF16)
    lo = (x - hi.astype(F32)).astype(BF16)
    return jnp.concatenate([hi, lo], axis=1)


def _ssd_dt_body(dt_ref, dtb_ref, alog_ref, dts_ref, acss_ref, acst_ref, *, rows):
    L = SSD_CHUNK
    li = lax.broadcasted_iota(jnp.int32, (L, L), 0)
    si = lax.broadcasted_iota(jnp.int32, (L, L), 1)
    tril = (li >= si).astype(F32)
    a2_row = -jnp.exp(alog_ref[...]) * LOG2E
    for c in range(rows // L):
        rs = slice(c * L, (c + 1) * L)
        dtv = _softplus(dt_ref[rs, :] + dtb_ref[...])
        acs = jnp.dot(tril, dtv * a2_row, preferred_element_type=F32,
                      precision=lax.Precision.HIGHEST)
        dts_ref[rs, :] = _split_hi_lo(dtv)
        acss_ref[rs, :] = _split_hi_lo(acs)
        acst_ref[c] = acs.T


def _ssd_dt(dt_raw, dt_bias_p, a_log_p, *, rows=1024):
    T = dt_raw.shape[0]
    L = SSD_CHUNK
    body = functools.partial(_ssd_dt_body, rows=rows)
    return pl.pallas_call(
        body,
        grid=(T // rows,),
        in_specs=[
            pl.BlockSpec((rows, LANES), lambda i: (i, 0)),
            pl.BlockSpec((1, LANES), lambda i: (0, 0)),
            pl.BlockSpec((1, LANES), lambda i: (0, 0)),
        ],
        out_specs=[
            pl.BlockSpec((rows, 2 * LANES), lambda i: (i, 0)),
            pl.BlockSpec((rows, 2 * LANES), lambda i: (i, 0)),
            pl.BlockSpec((rows // L, LANES, L), lambda i: (i, 0, 0)),
        ],
        out_shape=[jax.ShapeDtypeStruct((T, 2 * LANES), BF16), jax.ShapeDtypeStruct((T, 2 * LANES), BF16),
                   jax.ShapeDtypeStruct((T // L, LANES, L), F32)],
        compiler_params=_cparams(("parallel",)),
        name="ssd_dt",
    )(dt_raw, dt_bias_p, a_log_p)


def _ssd_body(xs_ref, bm_ref, cm_ref, xs_h_ref, bm_h_ref, cm_h_ref, z_ref, dts_ref, acss_ref, acst_ref,
              wx_ref, wb_ref, wc_ref, bx_ref, bb_ref, bc_ref, dsk_ref, nw_ref, *rest, rows, with_cast):
    if with_cast:
        fa_ref, fb_ref, o_ref, fa_o, fb_o, ux_ref, ub_ref, uc_ref, st_ref = rest
        fa_o[...] = fa_ref[...].astype(BF16)
        fb_o[...] = fb_ref[...].astype(BF16)
    else:
        o_ref, ux_ref, ub_ref, uc_ref, st_ref = rest
    g = pl.program_id(1)
    r = pl.program_id(2)
    L = SSD_CHUNK
    P = SSD_HD
    W = SSD_HPG * P
    HALO = SUBLANES

    @pl.when(r == 0)
    def _():
        st_ref[...] = jnp.zeros_like(st_ref)

    first = r == 0

    def conv(u_ref, halo_ref, buf_ref, w_ref, b_ref):
        halo = halo_ref[...]
        buf_ref[0:HALO, :] = jnp.where(first, jnp.zeros_like(halo), halo)
        buf_ref[HALO:, :] = u_ref[...]
        acc = b_ref[...] + w_ref[SSD_CONV - 1:SSD_CONV, :] * u_ref[...]
        for k in range(SSD_CONV - 1):
            shift = SSD_CONV - 1 - k
            acc = acc + w_ref[k:k + 1, :] * buf_ref[HALO - shift:HALO - shift + rows, :]
        buf_ref[HALO:, :] = acc * jax.nn.sigmoid(acc)

    conv(xs_ref, xs_h_ref, ux_ref, wx_ref, bx_ref)
    conv(bm_ref, bm_h_ref, ub_ref, wb_ref, bb_ref)
    conv(cm_ref, cm_h_ref, uc_ref, wc_ref, bc_ref)

    k_i = lax.broadcasted_iota(jnp.int32, (2 * LANES, W), 0)
    c_i = lax.broadcasted_iota(jnp.int32, (2 * LANES, W), 1)
    head = g * SSD_HPG + lax.shift_right_logical(c_i, P.bit_length() - 1)
    expand = ((k_i == head) | (k_i == head + LANES)).astype(BF16)

    li = lax.broadcasted_iota(jnp.int32, (L, L), 0)
    si = lax.broadcasted_iota(jnp.int32, (L, L), 1)
    causal = li >= si
    lane = lax.broadcasted_iota(jnp.int32, (L, LANES), 1)

    for c in range(rows // L):
        r0 = c * L
        rs = slice(HALO + r0, HALO + r0 + L)
        dt_x = _dot(dts_ref[r0:r0 + L, :], expand)
        acs_x = _dot(acss_ref[r0:r0 + L, :], expand)
        acs_last = acs_x[L - 1:L, :]

        xs = ux_ref[rs, :]
        bmat = ub_ref[rs, :]
        cmat = uc_ref[rs, :]
        x = xs * dt_x
        xb = x.astype(BF16)
        cb16 = cmat.astype(BF16)
        cb = jnp.where(causal, _dot_nt(cb16, bmat.astype(BF16)), 0.0)

        tiles = []
        for m in range(W // LANES):
            res = []
            for jj in range(2):
                j = 2 * m + jj
                col = acs_x[:, j * P:j * P + 1]
                row = acst_ref[c, pl.ds(g * SSD_HPG + j, 1), :]
                dec = jnp.exp2(jnp.minimum(col - row, 0.0))
                sc = (cb * dec).astype(BF16)
                res.append(_dot(sc, xb[:, m * LANES:(m + 1) * LANES]))
            tiles.append(jnp.where(lane < P, res[0], res[1]))
        y = jnp.concatenate(tiles, axis=1)

        st = st_ref[...]
        y = y + _dot(cb16, st.astype(BF16)) * jnp.exp2(acs_x)
        xd = (x * jnp.exp2(acs_last - acs_x)).astype(BF16)
        st_ref[...] = st * jnp.exp2(acs_last) + _dot(bmat.T.astype(BF16), xd)

        y = y + dsk_ref[...] * xs
        z = z_ref[r0:r0 + L, :]
        gt = y * (z * jax.nn.sigmoid(z))
        gn = gt * lax.rsqrt(jnp.mean(gt * gt, axis=-1, keepdims=True) + EPS)
        o_ref[r0:r0 + L, :] = (gn * nw_ref[...]).astype(o_ref.dtype)


def _ssd(p_main, dts, acss, acst, conv_w, conv_b, d_skip_x, ssd_norm, cast=None, *, batch, seq, rows=1024):
    T = p_main.shape[0]
    G = SSD_GROUPS
    N = SSD_STATE
    W = SSD_HPG * SSD_HD
    nr = seq // rows
    hb = rows // SUBLANES
    z0, x0 = 3 * ATTN_HEADS * ATTN_HD // W, (3 * ATTN_HEADS * ATTN_HD + G * W) // W
    b0 = (3 * ATTN_HEADS * ATTN_HD + 2 * G * W) // N
    c0 = b0 + G
    cwb0, cwc0 = G * W // N, G * W // N + G

    def rowblk(b, g, r):
        return b * nr + r

    def halo(b, g, r):
        return jnp.maximum((b * nr + r) * hb - 1, 0)

    cast_in, cast_specs, cast_shapes = [], [], []
    if cast is not None:
        cast_a, cast_b, lc = cast
        (_, Ra, Ca), (_, Rb, Cb) = cast_a.shape, cast_b.shape
        n_bg = batch * G
        assert Ra % n_bg == 0 and Ca % nr == 0 and Rb % nr == 0 and Cb % n_bg == 0
        blk_a, blk_b = (Ra // n_bg, Ca // nr), (Rb // nr, Cb // n_bg)
        cast_in = [pl.BlockSpec((None,) + blk_a, lambda b, g, r: (lc, b * G + g, r)),
                   pl.BlockSpec((None,) + blk_b, lambda b, g, r: (lc, r, b * G + g))]
        cast_specs = [pl.BlockSpec((None,) + blk_a, lambda b, g, r: (0, b * G + g, r)),
                      pl.BlockSpec((None,) + blk_b, lambda b, g, r: (0, r, b * G + g))]
        cast_shapes = [jax.ShapeDtypeStruct((1, Ra, Ca), BF16), jax.ShapeDtypeStruct((1, Rb, Cb), BF16)]

    body = functools.partial(_ssd_body, rows=rows, with_cast=cast is not None)
    outs = pl.pallas_call(
        body,
        grid=(batch, G, nr),
        in_specs=[
            pl.BlockSpec((rows, W), lambda b, g, r: (rowblk(b, g, r), x0 + g)),
            pl.BlockSpec((rows, N), lambda b, g, r: (rowblk(b, g, r), b0 + g)),
            pl.BlockSpec((rows, N), lambda b, g, r: (rowblk(b, g, r), c0 + g)),
            pl.BlockSpec((SUBLANES, W), lambda b, g, r: (halo(b, g, r), x0 + g)),
            pl.BlockSpec((SUBLANES, N), lambda b, g, r: (halo(b, g, r), b0 + g)),
            pl.BlockSpec((SUBLANES, N), lambda b, g, r: (halo(b, g, r), c0 + g)),
            pl.BlockSpec((rows, W), lambda b, g, r: (rowblk(b, g, r), z0 + g)),
            pl.BlockSpec((rows, 2 * LANES), lambda b, g, r: (rowblk(b, g, r), 0)),
            pl.BlockSpec((rows, 2 * LANES), lambda b, g, r: (rowblk(b, g, r), 0)),
            pl.BlockSpec((rows // SSD_CHUNK, LANES, SSD_CHUNK), lambda b, g, r: (rowblk(b, g, r), 0, 0)),
            pl.BlockSpec((SSD_CONV, W), lambda b, g, r: (0, g)),
            pl.BlockSpec((SSD_CONV, N), lambda b, g, r: (0, cwb0 + g)),
            pl.BlockSpec((SSD_CONV, N), lambda b, g, r: (0, cwc0 + g)),
            pl.BlockSpec((1, W), lambda b, g, r: (0, g)),
            pl.BlockSpec((1, N), lambda b, g, r: (0, cwb0 + g)),
            pl.BlockSpec((1, N), lambda b, g, r: (0, cwc0 + g)),
            pl.BlockSpec((1, W), lambda b, g, r: (0, g)),
            pl.BlockSpec((1, W), lambda b, g, r: (0, g)),
        ] + cast_in,
        out_specs=[pl.BlockSpec((rows, W), lambda b, g, r: (rowblk(b, g, r), g))] + cast_specs,
        out_shape=[jax.ShapeDtypeStruct((T, G * W), BF16)] + cast_shapes,
        scratch_shapes=[
            pltpu.VMEM((SUBLANES + rows, W), F32),
            pltpu.VMEM((SUBLANES + rows, N), F32),
            pltpu.VMEM((SUBLANES + rows, N), F32),
            pltpu.VMEM((N, W), F32),
        ],
        compiler_params=_cparams(("parallel", "parallel", "arbitrary")),
        name="ssd",
    )(p_main, p_main, p_main, p_main, p_main, p_main, p_main, dts, acss, acst,
      conv_w, conv_w, conv_w, conv_b, conv_b, conv_b, d_skip_x, ssd_norm,
      *([cast[0], cast[1]] if cast is not None else []))
    return outs[0], outs[1:]


def _outproj_body(x_ref, ya_ref, ys_ref, w_ref, o_ref):
    ka = ya_ref.shape[1]
    o_ref[...] = x_ref[...] + _dot(ya_ref[...], w_ref[0:ka, :]) + _dot(ys_ref[...], w_ref[ka:, :])


def _out_proj(x, ya, ys, w_out, l, *, tm=512, tn=1024):
    T, D = x.shape
    Ka, Ks = ya.shape[1], ys.shape[1]
    return pl.pallas_call(
        _outproj_body,
        grid=(D // tn, T // tm),
        in_specs=[
            pl.BlockSpec((tm, tn), lambda j, i: (i, j)),
            pl.BlockSpec((tm, Ka), lambda j, i: (i, 0)),
            pl.BlockSpec((tm, Ks), lambda j, i: (i, 0)),
            pl.BlockSpec((None, Ka + Ks, tn), lambda j, i: (l, 0, j)),
        ],
        out_specs=pl.BlockSpec((tm, tn), lambda j, i: (i, j)),
        out_shape=jax.ShapeDtypeStruct((T, D), F32),
        compiler_params=_cparams(("parallel", "parallel")),
        name="out_proj",
    )(x, ya, ys, w_out)


def _memkv_body(m_ref, nw_ref, wk_ref, wv_ref, kn_ref, k_ref, v_ref):
    h = _rms(m_ref[...], nw_ref[...]).astype(BF16)
    k = _dot(h, wk_ref[...])
    v_ref[...] = _dot(h, wv_ref[...]).astype(BF16)
    for hh in range(MEM_HEADS):
        sl = slice(hh * MEM_HD, (hh + 1) * MEM_HD)
        k_ref[:, sl] = _rms(k[:, sl], kn_ref[...]).astype(BF16)


def _mem_kv(mem, nw, wk, wv, kn, l):
    M, D = mem.shape
    Wm = wk.shape[2]
    full = lambda shape: pl.BlockSpec(shape, lambda i: (0,) * len(shape))
    layer = lambda shape: pl.BlockSpec((None,) + shape, lambda i: (l,) + (0,) * len(shape))
    return pl.pallas_call(
        _memkv_body,
        grid=(1,),
        in_specs=[full((M, D)), full((1, D)), layer((D, Wm)), layer((D, Wm)), full((1, MEM_HD))],
        out_specs=[full((M, Wm)), full((M, Wm))],
        out_shape=[jax.ShapeDtypeStruct((M, Wm), BF16)] * 2,
        compiler_params=_cparams(("arbitrary",)),
        name="mem_kv",
    )(mem, nw, wk, wv, kn)


def _memattn_body(x_ref, nw_ref, wq_ref, qn_ref, k_ref, v_ref, wo_ref, o_ref):
    x = x_ref[...]
    h = _rms(x, nw_ref[...]).astype(BF16)
    q = _dot(h, wq_ref[...])
    scale = MEM_HD ** -0.5
    outs = []
    for hh in range(MEM_HEADS):
        sl = slice(hh * MEM_HD, (hh + 1) * MEM_HD)
        qh = _rms(q[:, sl], qn_ref[...]).astype(BF16)
        s = _dot_nt(qh, k_ref[:, sl]) * scale
        e = jnp.exp(s - jnp.max(s, axis=-1, keepdims=True))
        p = e / jnp.sum(e, axis=-1, keepdims=True)
        outs.append(_dot(p.astype(BF16), v_ref[:, sl]))
    o = jnp.concatenate(outs, axis=1).astype(BF16)
    o_ref[...] = x + _dot(o, wo_ref[...])


def _mem_attn(x, nw, wq, qn, k, v, wo, l, *, seq, mem_len, tm=512):
    T, D = x.shape
    Wm = wq.shape[2]
    per_b = seq // tm
    return pl.pallas_call(
        _memattn_body,
        grid=(T // tm,),
        in_specs=[
            pl.BlockSpec((tm, D), lambda i: (i, 0)),
            pl.BlockSpec((1, D), lambda i: (0, 0)),
            pl.BlockSpec((None, D, Wm), lambda i: (l, 0, 0)),
            pl.BlockSpec((1, MEM_HD), lambda i: (0, 0)),
            pl.BlockSpec((mem_len, Wm), lambda i: (i // per_b, 0)),
            pl.BlockSpec((mem_len, Wm), lambda i: (i // per_b, 0)),
            pl.BlockSpec((None, Wm, D), lambda i: (l, 0, 0)),
        ],
        out_specs=pl.BlockSpec((tm, D), lambda i: (i, 0)),
        out_shape=jax.ShapeDtypeStruct((T, D), F32),
        compiler_params=_cparams(("parallel",)),
        name="mem_attn",
    )(x, nw, wq, qn, k, v, wo)


def _pad_lanes(v):
    return jnp.pad(v.astype(F32), (0, LANES - v.shape[0])).reshape(1, LANES)


def kernel(x, mem, ff1_norm, ff1_w_gu, ff1_w_down, mix_norm, w_in, q_norm, k_norm, conv_w, conv_b,
           dt_bias, a_log, d_skip, ssd_norm, w_out, xmem_norm, mem_norm, mem_wq, mem_wk, mem_wv,
           mem_q_norm, mem_k_norm, mem_wo, ff2_norm, ff2_w_gu, ff2_w_down):
    B, S, D = x.shape
    M = mem.shape[1]
    depth = w_in.shape[0]
    n_dt = SSD_GROUPS * SSD_HPG
    row = lambda v: v.reshape(1, -1)

    slopes = jnp.exp2(-8.0 * jnp.arange(1, ATTN_HEADS + 1, dtype=F32) / ATTN_HEADS)
    slopes = jnp.broadcast_to(slopes[:, None, None], (ATTN_HEADS, 1, LANES))
    bf = _cast_bf16
    ff1_bf = (bf(ff1_w_gu, layer=0), bf(ff1_w_down, layer=0))
    n_main = w_in.shape[2] - n_dt
    w_main = w_in[0:1, :, :n_main].astype(BF16)
    w_dt = jnp.pad(w_in[:, :, n_main:], ((0, 0), (0, 0), (0, LANES - n_dt))).astype(BF16)
    mem_wq, mem_wk, mem_wv, mem_wo = bf(mem_wq), bf(mem_wk), bf(mem_wv), bf(mem_wo)

    xf = x.reshape(B * S, D)
    memf = mem.reshape(B * M, D)
    for l in range(depth):
        xf = _ffn(xf, row(ff1_norm[l]), ff1_bf[0], ff1_bf[1], 0)

        p_main, dt_raw, w2_gu, w2_down = _in_proj(xf, row(mix_norm[l]), w_main, w_dt[l:l + 1], 0,
                                                  ff2_w_gu, ff2_w_down, l)
        more = l + 1 < depth
        jobs = [(w_out, l, w_out.shape[2])] + ([(w_in, l + 1, n_main)] if more else [])
        y_attn, cast = _moba(p_main, row(q_norm[l]), row(k_norm[l]), slopes, jobs, batch=B, seq=S)
        w_out_l = cast[0]
        if more:
            w_main = cast[1]
        dts, acss, acst = _ssd_dt(dt_raw, _pad_lanes(dt_bias[l]), _pad_lanes(a_log[l]))
        nxt = (ff1_w_gu, ff1_w_down, l + 1) if more else None
        y_ssd, ff1_next = _ssd(p_main, dts, acss, acst, conv_w[l], row(conv_b[l]),
                               row(jnp.repeat(d_skip[l], SSD_HD)), row(ssd_norm[l]), nxt, batch=B, seq=S)
        ff1_bf = ff1_next or ff1_bf
        xf = _out_proj(xf, y_attn, y_ssd, w_out_l, 0)

        k_mem, v_mem = _mem_kv(memf, row(mem_norm[l]), mem_wk, mem_wv, row(mem_k_norm[l]), l)
        xf = _mem_attn(xf, row(xmem_norm[l]), mem_wq, row(mem_q_norm[l]), k_mem, v_mem, mem_wo, l,
                       seq=S, mem_len=M)

        xf = _ffn(xf, row(ff2_norm[l]), w2_gu, w2_down, 0)
    return xf.reshape(B, S, D)
```

```python
import functools

import jax
import jax.numpy as jnp
from jax import lax
from jax.experimental import pallas as pl
from jax.experimental.pallas import tpu as pltpu

F32 = jnp.float32
BF16 = jnp.bfloat16
EPS = 1e-6
NEG = -1e30
LOG2E = 1.4426950408889634

LANES = 128
SUBLANES = 8
VMEM_LIMIT = 56 * 1024 * 1024

ATTN_HEADS = 8
ATTN_HD = 128
MOBA_BLOCK = 256
MOBA_TOPK = 3
SSD_HD = 64
SSD_GROUPS = 8
SSD_HPG = 6
SSD_STATE = 128
SSD_CONV = 4
SSD_CHUNK = 128
MEM_HEADS = 4
MEM_HD = 128


def _cparams(sem):
    return pltpu.CompilerParams(dimension_semantics=sem, vmem_limit_bytes=VMEM_LIMIT)


def _rms(x, w):
    return x * lax.rsqrt(jnp.mean(x * x, axis=-1, keepdims=True) + EPS) * w


def _dot(a, b):
    return jnp.dot(a, b, preferred_element_type=F32)


def _dot_nt(a, b, precision=None):
    return lax.dot_general(a, b, (((1,), (1,)), ((), ())), preferred_element_type=F32,
                           precision=precision)


CAST_BLOCK_BYTES = 12 * 1024 * 1024


def _cast_body(w_ref, o_ref):
    o_ref[...] = w_ref[...].astype(BF16)


def _cast_bf16(w, layer=None):
    nl, R, C = w.shape
    first = 0
    if layer is not None:
        nl, first = 1, layer
    tr = R
    while tr * C * 4 > CAST_BLOCK_BYTES and tr % 32 == 0:
        tr //= 2
    return pl.pallas_call(
        _cast_body,
        grid=(nl, R // tr),
        in_specs=[pl.BlockSpec((None, tr, C), lambda l, i: (first + l, i, 0))],
        out_specs=pl.BlockSpec((None, tr, C), lambda l, i: (l, i, 0)),
        out_shape=jax.ShapeDtypeStruct((nl, R, C), BF16),
        compiler_params=_cparams(("parallel", "parallel")),
        name="cast_bf16",
    )(w)


def _ffn_body(x_ref, nw_ref, wg_ref, wu_ref, wd_ref, o_ref, h_ref):
    j = pl.program_id(1)

    @pl.when(j == 0)
    def _():
        x = x_ref[...]
        h_ref[...] = _rms(x, nw_ref[...]).astype(BF16)
        o_ref[...] = x

    h = h_ref[...]
    g = _dot(h, wg_ref[...])
    u = _dot(h, wu_ref[...])
    a = (0.5 * (g * jax.nn.sigmoid(g)) * u).astype(BF16)
    o_ref[...] += _dot(a, wd_ref[...])


def _ffn(x, nw, w_gu, w_down, l, *, tm=1024, tf=512):
    T, D = x.shape
    F = w_down.shape[1]
    nf = F // tf
    return pl.pallas_call(
        _ffn_body,
        grid=(T // tm, nf),
        in_specs=[
            pl.BlockSpec((tm, D), lambda i, j: (i, 0)),
            pl.BlockSpec((1, D), lambda i, j: (0, 0)),
            pl.BlockSpec((None, D, tf), lambda i, j: (l, 0, j)),
            pl.BlockSpec((None, D, tf), lambda i, j: (l, 0, j + nf)),
            pl.BlockSpec((None, tf, D), lambda i, j: (l, j, 0)),
        ],
        out_specs=pl.BlockSpec((tm, D), lambda i, j: (i, 0)),
        out_shape=jax.ShapeDtypeStruct((T, D), F32),
        scratch_shapes=[pltpu.VMEM((tm, D), BF16)],
        compiler_params=_cparams(("parallel", "arbitrary")),
        name="ffn",
    )(x, nw, w_gu, w_gu, w_down)


def _inproj_body(x_ref, nw_ref, w_ref, wdt_ref, fa_ref, fb_ref, p_ref, dt_ref, fa_o, fb_o, h_ref):
    j = pl.program_id(1)

    @pl.when(j == 0)
    def _():
        h = _rms(x_ref[...], nw_ref[...]).astype(BF16)
        h_ref[...] = h
        dt_ref[...] = _dot(h, wdt_ref[...])

    p_ref[...] = _dot(h_ref[...], w_ref[...])
    fa_o[...] = fa_ref[...].astype(BF16)
    fb_o[...] = fb_ref[...].astype(BF16)


def _in_proj(x, nw, w_main, w_dt, l, cast_a, cast_b, lc, *, tm=1024, tn=1024):
    T, D = x.shape
    N = w_main.shape[2]
    assert N % tn == 0 and tn % LANES == 0
    ni, nj = T // tm, N // tn
    (_, Ra, Ca), (_, Rb, Cb) = cast_a.shape, cast_b.shape
    assert Ra % ni == 0 and Ca % nj == 0 and Rb % nj == 0 and Cb % ni == 0
    blk_a, blk_b = (Ra // ni, Ca // nj), (Rb // nj, Cb // ni)
    return pl.pallas_call(
        _inproj_body,
        grid=(ni, nj),
        in_specs=[
            pl.BlockSpec((tm, D), lambda i, j: (i, 0)),
            pl.BlockSpec((1, D), lambda i, j: (0, 0)),
            pl.BlockSpec((None, D, tn), lambda i, j: (l, 0, j)),
            pl.BlockSpec((None, D, LANES), lambda i, j: (l, 0, 0)),
            pl.BlockSpec((None,) + blk_a, lambda i, j: (lc, i, j)),
            pl.BlockSpec((None,) + blk_b, lambda i, j: (lc, j, i)),
        ],
        out_specs=[
            pl.BlockSpec((tm, tn), lambda i, j: (i, j)),
            pl.BlockSpec((tm, LANES), lambda i, j: (i, 0)),
            pl.BlockSpec((None,) + blk_a, lambda i, j: (0, i, j)),
            pl.BlockSpec((None,) + blk_b, lambda i, j: (0, j, i)),
        ],
        out_shape=[jax.ShapeDtypeStruct((T, N), F32), jax.ShapeDtypeStruct((T, LANES), F32),
                   jax.ShapeDtypeStruct((1, Ra, Ca), BF16), jax.ShapeDtypeStruct((1, Rb, Cb), BF16)],
        scratch_shapes=[pltpu.VMEM((tm, D), BF16)],
        compiler_params=_cparams(("parallel", "arbitrary")),
        name="in_proj",
    )(x, nw, w_main, w_dt, cast_a, cast_b)


MOBA_EXT = LANES
MOBA_SEL0 = 16
MOBA_VARIANT = 2
MOBA_QBLOCKS = 2


def _split3(x):
    hi = x.astype(BF16).astype(F32)
    r = x - hi
    mid = r.astype(BF16).astype(F32)
    lo = (r - mid).astype(BF16).astype(F32)
    return hi, mid, lo


def _rows16(rows, width):
    r_i = lax.broadcasted_iota(jnp.int32, (2 * SUBLANES, width), 0)
    out = jnp.zeros((2 * SUBLANES, width), F32)
    for k, row in enumerate(rows):
        out = jnp.where(r_i == k, row, out)
    return out


def _moba_body(q_ref, k_ref, v_ref, qn_ref, kn_ref, slope_ref, *rest, nb, n_cast):
    cast_in, rest = rest[:n_cast], rest[n_cast:]
    o_ref, cast_out, rest = rest[0], rest[1:1 + n_cast], rest[1 + n_cast:]
    for src, dst in zip(cast_in, cast_out):
        dst[...] = src[...].astype(BF16)
    _moba_attend(q_ref, k_ref, v_ref, qn_ref, kn_ref, slope_ref, o_ref, *rest, nb=nb)


def _moba_attend(q_ref, k_ref, v_ref, qn_ref, kn_ref, slope_ref, o_ref,
                 kx_ref, vt_ref, kmean_ref, qx_ref, *, nb):
    qi = pl.program_id(2)
    blk = MOBA_BLOCK
    seq = nb * blk
    scale = ATTN_HD ** -0.5
    slope = slope_ref[0:1, 0:1]

    @pl.when((pl.program_id(0) == 0) & (pl.program_id(1) == 0) & (qi == 0))
    def _():
        pos = lax.broadcasted_iota(jnp.int32, (seq, MOBA_EXT), 0)
        c = lax.broadcasted_iota(jnp.int32, (seq, MOBA_EXT), 1)
        n_of = lax.shift_right_logical(pos, blk.bit_length() - 1)
        j_of = pos & (blk - 1)
        ext = jnp.where(c < 3, 1, jnp.where(c < 6, n_of, jnp.where(c < 9, j_of, 0)))
        ext = jnp.where(c - MOBA_SEL0 == n_of, 1, ext)
        kx_ref[:, ATTN_HD:] = ext.astype(F32).astype(BF16)

    @pl.when(qi == 0)
    def _():
        kf = _rms(k_ref[...], kn_ref[...])
        kx_ref[:, 0:ATTN_HD] = kf.astype(BF16)
        for n in range(nb):
            kmean_ref[n:n + 1, :] = jnp.mean(kf[n * blk:(n + 1) * blk], axis=0, keepdims=True)
        vt_ref[...] = v_ref[...].T.astype(BF16)

        qn = _rms(q_ref[...], qn_ref[...])
        gate = _dot_nt(kmean_ref[...], qn, precision=lax.Precision.HIGHEST)
        n_iota = lax.broadcasted_iota(jnp.int32, gate.shape, 0).astype(F32)
        t_i = lax.broadcasted_iota(jnp.int32, (1, seq), 1)
        past_blk = n_iota < lax.shift_right_logical(t_i, blk.bit_length() - 1).astype(F32)
        gate = jnp.where(past_blk, gate, -jnp.inf)
        selb = jnp.full(gate.shape, NEG, F32)
        for _ in range(MOBA_TOPK):
            best = jnp.max(gate, axis=0, keepdims=True)
            first = jnp.min(jnp.where(gate == best, n_iota, float(nb)), axis=0, keepdims=True)
            pick = n_iota == first
            selb = jnp.where(pick, 0.0, selb)
            gate = jnp.where(pick, -jnp.inf, gate)
        selb = jnp.where(past_blk, selb, NEG)
        slope2 = slope * LOG2E
        ones = jnp.ones((1, seq), F32)
        bias_rows = (list(_split3(-slope2 * t_i.astype(F32))) + [v * ones for v in _split3(slope2 * float(blk))]
                     + [v * ones for v in _split3(slope2)])
        qx_ref[0:ATTN_HD, :] = (qn * (scale * LOG2E)).T.astype(BF16)
        qx_ref[ATTN_HD:ATTN_HD + MOBA_SEL0, :] = _rows16(bias_rows, seq).astype(BF16)
        qx_ref[ATTN_HD + MOBA_SEL0:ATTN_HD + MOBA_SEL0 + nb, :] = selb.astype(BF16)
        qx_ref[ATTN_HD + MOBA_SEL0 + nb:, :] = jnp.zeros((MOBA_EXT - MOBA_SEL0 - nb, seq), BF16)

    qw = MOBA_QBLOCKS * blk
    qcols = pl.ds(pl.multiple_of(qi * qw, qw), qw)
    qx_past = qx_ref[:, qcols]
    qx_own = jnp.concatenate([qx_ref[0:ATTN_HD + MOBA_SEL0, qcols],
                              jnp.zeros((MOBA_EXT - MOBA_SEL0, qw), BF16)], axis=0)

    key_i = lax.broadcasted_iota(jnp.int32, (blk, blk), 0)
    qry_i = lax.broadcasted_iota(jnp.int32, (blk, blk), 1)
    lanes = lambda a, u: a[:, u * blk:(u + 1) * blk]
    row_cat = lambda parts: jnp.concatenate(parts, axis=1)

    def attend(n_past):
        n = n_past * blk
        owns = [pl.ds(pl.multiple_of((qi * MOBA_QBLOCKS + u) * blk, blk), blk) for u in range(MOBA_QBLOCKS)]
        s_own = [jnp.where(qry_i >= key_i, _dot(kx_ref[owns[u], :], lanes(qx_own, u)), NEG)
                 for u in range(MOBA_QBLOCKS)]
        h = (n_past // 2) * blk
        s_a = _dot(kx_ref[0:h, :], qx_past) if h else None
        s_b = _dot(kx_ref[h:n, :], qx_past)
        parts = []
        if h:
            m_a = jnp.max(s_a, axis=0, keepdims=True)
            p_a = jnp.exp2(s_a - m_a)
            parts.append((m_a, jnp.sum(p_a, axis=0, keepdims=True), _dot(vt_ref[:, 0:h], p_a.astype(BF16))))
        m_b = jnp.maximum(jnp.max(s_b, axis=0, keepdims=True),
                          row_cat([jnp.max(so, axis=0, keepdims=True) for so in s_own]))
        p_b = jnp.exp2(s_b - m_b)
        p_own = [jnp.exp2(s_own[u] - lanes(m_b, u)) for u in range(MOBA_QBLOCKS)]
        l_b = jnp.sum(p_b, axis=0, keepdims=True) + row_cat([jnp.sum(po, axis=0, keepdims=True) for po in p_own])
        acc_b = _dot(vt_ref[:, h:n], p_b.astype(BF16)) + row_cat(
            [_dot(vt_ref[:, owns[u]], p_own[u].astype(BF16)) for u in range(MOBA_QBLOCKS)])
        parts.append((m_b, l_b, acc_b))
        m = functools.reduce(jnp.maximum, [pm for pm, _, _ in parts])
        l = sum(jnp.exp2(pm - m) * pl_ for pm, pl_, _ in parts)
        acc = sum(jnp.exp2(pm - m) * pa for pm, _, pa in parts)
        o_ref[...] = (acc / l).T.astype(o_ref.dtype)

    per = MOBA_VARIANT // MOBA_QBLOCKS
    for v in range(nb // MOBA_VARIANT):
        pl.when((qi >= v * per) & (qi < (v + 1) * per))(
            functools.partial(attend, (v + 1) * MOBA_VARIANT - 1))


def _moba(p_main, q_norm, k_norm, slopes, casts=(), *, batch, seq):
    T = p_main.shape[0]
    blk = MOBA_BLOCK
    nb = seq // blk
    assert seq % blk == 0 and nb % MOBA_VARIANT == 0 and nb <= MOBA_EXT - MOBA_SEL0
    assert MOBA_VARIANT % MOBA_QBLOCKS == 0
    H = ATTN_HEADS
    qw = MOBA_QBLOCKS * blk
    nq = seq // qw
    cast_in, cast_out, cast_shapes = [], [], []
    for w, lc, C in casts:
        R = w.shape[1]
        assert R % (batch * H) == 0 and C % nq == 0 and (C // nq) % LANES == 0
        cblk = (None, R // (batch * H), C // nq)
        cast_in.append(pl.BlockSpec(cblk, lambda b, h, i, lc=lc: (lc, b * H + h, i)))
        cast_out.append(pl.BlockSpec(cblk, lambda b, h, i: (0, b * H + h, i)))
        cast_shapes.append(jax.ShapeDtypeStruct((1, R, C), BF16))
    body = functools.partial(_moba_body, nb=nb, n_cast=len(casts))
    outs = pl.pallas_call(
        body,
        grid=(batch, H, nq),
        in_specs=[
            pl.BlockSpec((seq, ATTN_HD), lambda b, h, i: (b, h)),
            pl.BlockSpec((seq, ATTN_HD), lambda b, h, i: (b, H + h)),
            pl.BlockSpec((seq, ATTN_HD), lambda b, h, i: (b, 2 * H + h)),
            pl.BlockSpec((1, ATTN_HD), lambda b, h, i: (0, 0)),
            pl.BlockSpec((1, ATTN_HD), lambda b, h, i: (0, 0)),
            pl.BlockSpec((None, 1, LANES), lambda b, h, i: (h, 0, 0)),
        ] + cast_in,
        out_specs=[pl.BlockSpec((qw, ATTN_HD), lambda b, h, i: (b * nq + i, h))] + cast_out,
        out_shape=[jax.ShapeDtypeStruct((T, H * ATTN_HD), BF16)] + cast_shapes,
        scratch_shapes=[
            pltpu.VMEM((seq, ATTN_HD + MOBA_EXT), BF16),
            pltpu.VMEM((ATTN_HD, seq), BF16),
            pltpu.VMEM((nb, ATTN_HD), F32),
            pltpu.VMEM((ATTN_HD + MOBA_EXT, seq), BF16),
        ],
        compiler_params=_cparams(("arbitrary", "arbitrary", "arbitrary")),
        name="moba",
    )(p_main, p_main, p_main, q_norm, k_norm, slopes, *[w for w, _, _ in casts])
    return outs[0], outs[1:]


def _softplus(x):
    return jnp.maximum(x, 0.0) + jnp.log1p(jnp.exp(-jnp.abs(x)))


def _split_hi_lo(x):
    hi = x.astype(BF16)
    lo = (x - hi.astype(F32)).astype(BF16)
    return jnp.concatenate([hi, lo], axis=1)


def _ssd_dt_body(dt_ref, dtb_ref, alog_ref, dts_ref, acss_ref, acst_ref, *, rows):
    L = SSD_CHUNK
    li = lax.broadcasted_iota(jnp.int32, (L, L), 0)
    si = lax.broadcasted_iota(jnp.int32, (L, L), 1)
    tril = (li >= si).astype(F32)
    a2_row = -jnp.exp(alog_ref[...]) * LOG2E
    for c in range(rows // L):
        rs = slice(c * L, (c + 1) * L)
        dtv = _softplus(dt_ref[rs, :] + dtb_ref[...])
        acs = jnp.dot(tril, dtv * a2_row, preferred_element_type=F32,
                      precision=lax.Precision.HIGHEST)
        dts_ref[rs, :] = _split_hi_lo(dtv)
        acss_ref[rs, :] = _split_hi_lo(acs)
        acst_ref[c] = acs.T


def _ssd_dt(dt_raw, dt_bias_p, a_log_p, *, rows=1024):
    T = dt_raw.shape[0]
    L = SSD_CHUNK
    body = functools.partial(_ssd_dt_body, rows=rows)
    return pl.pallas_call(
        body,
        grid=(T // rows,),
        in_specs=[
            pl.BlockSpec((rows, LANES), lambda i: (i, 0)),
            pl.BlockSpec((1, LANES), lambda i: (0, 0)),
            pl.BlockSpec((1, LANES), lambda i: (0, 0)),
        ],
        out_specs=[
            pl.BlockSpec((rows, 2 * LANES), lambda i: (i, 0)),
            pl.BlockSpec((rows, 2 * LANES), lambda i: (i, 0)),
            pl.BlockSpec((rows // L, LANES, L), lambda i: (i, 0, 0)),
        ],
        out_shape=[jax.ShapeDtypeStruct((T, 2 * LANES), BF16), jax.ShapeDtypeStruct((T, 2 * LANES), BF16),
                   jax.ShapeDtypeStruct((T // L, LANES, L), F32)],
        compiler_params=_cparams(("parallel",)),
        name="ssd_dt",
    )(dt_raw, dt_bias_p, a_log_p)


def _ssd_body(xs_ref, bm_ref, cm_ref, xs_h_ref, bm_h_ref, cm_h_ref, z_ref, dts_ref, acss_ref, acst_ref,
              wx_ref, wb_ref, wc_ref, bx_ref, bb_ref, bc_ref, dsk_ref, nw_ref, *rest, rows, with_cast):
    if with_cast:
        fa_ref, fb_ref, o_ref, fa_o, fb_o, ux_ref, ub_ref, uc_ref, st_ref = rest
        fa_o[...] = fa_ref[...].astype(BF16)
        fb_o[...] = fb_ref[...].astype(BF16)
    else:
        o_ref, ux_ref, ub_ref, uc_ref, st_ref = rest
    g = pl.program_id(1)
    r = pl.program_id(2)
    L = SSD_CHUNK
    P = SSD_HD
    W = SSD_HPG * P
    HALO = SUBLANES

    @pl.when(r == 0)
    def _():
        st_ref[...] = jnp.zeros_like(st_ref)

    first = r == 0

    def conv(u_ref, halo_ref, buf_ref, w_ref, b_ref):
        halo = halo_ref[...]
        buf_ref[0:HALO, :] = jnp.where(first, jnp.zeros_like(halo), halo)
        buf_ref[HALO:, :] = u_ref[...]
        acc = b_ref[...] + w_ref[SSD_CONV - 1:SSD_CONV, :] * u_ref[...]
        for k in range(SSD_CONV - 1):
            shift = SSD_CONV - 1 - k
            acc = acc + w_ref[k:k + 1, :] * buf_ref[HALO - shift:HALO - shift + rows, :]
        buf_ref[HALO:, :] = acc * jax.nn.sigmoid(acc)

    conv(xs_ref, xs_h_ref, ux_ref, wx_ref, bx_ref)
    conv(bm_ref, bm_h_ref, ub_ref, wb_ref, bb_ref)
    conv(cm_ref, cm_h_ref, uc_ref, wc_ref, bc_ref)

    k_i = lax.broadcasted_iota(jnp.int32, (2 * LANES, W), 0)
    c_i = lax.broadcasted_iota(jnp.int32, (2 * LANES, W), 1)
    head = g * SSD_HPG + lax.shift_right_logical(c_i, P.bit_length() - 1)
    expand = ((k_i == head) | (k_i == head + LANES)).astype(BF16)

    li = lax.broadcasted_iota(jnp.int32, (L, L), 0)
    si = lax.broadcasted_iota(jnp.int32, (L, L), 1)
    causal = li >= si
    lane = lax.broadcasted_iota(jnp.int32, (L, LANES), 1)

    for c in range(rows // L):
        r0 = c * L
        rs = slice(HALO + r0, HALO + r0 + L)
        dt_x = _dot(dts_ref[r0:r0 + L, :], expand)
        acs_x = _dot(acss_ref[r0:r0 + L, :], expand)
        acs_last = acs_x[L - 1:L, :]

        xs = ux_ref[rs, :]
        bmat = ub_ref[rs, :]
        cmat = uc_ref[rs, :]
        x = xs * dt_x
        xb = x.astype(BF16)
        cb16 = cmat.astype(BF16)
        cb = jnp.where(causal, _dot_nt(cb16, bmat.astype(BF16)), 0.0)

        tiles = []
        for m in range(W // LANES):
            res = []
            for jj in range(2):
                j = 2 * m + jj
                col = acs_x[:, j * P:j * P + 1]
                row = acst_ref[c, pl.ds(g * SSD_HPG + j, 1), :]
                dec = jnp.exp2(jnp.minimum(col - row, 0.0))
                sc = (cb * dec).astype(BF16)
                res.append(_dot(sc, xb[:, m * LANES:(m + 1) * LANES]))
            tiles.append(jnp.where(lane < P, res[0], res[1]))
        y = jnp.concatenate(tiles, axis=1)

        st = st_ref[...]
        y = y + _dot(cb16, st.astype(BF16)) * jnp.exp2(acs_x)
        xd = (x * jnp.exp2(acs_last - acs_x)).astype(BF16)
        st_ref[...] = st * jnp.exp2(acs_last) + _dot(bmat.T.astype(BF16), xd)

        y = y + dsk_ref[...] * xs
        z = z_ref[r0:r0 + L, :]
        gt = y * (z * jax.nn.sigmoid(z))
        gn = gt * lax.rsqrt(jnp.mean(gt * gt, axis=-1, keepdims=True) + EPS)
        o_ref[r0:r0 + L, :] = (gn * nw_ref[...]).astype(o_ref.dtype)


def _ssd(p_main, dts, acss, acst, conv_w, conv_b, d_skip_x, ssd_norm, cast=None, *, batch, seq, rows=1024):
    T = p_main.shape[0]
    G = SSD_GROUPS
    N = SSD_STATE
    W = SSD_HPG * SSD_HD
    nr = seq // rows
    hb = rows // SUBLANES
    z0, x0 = 3 * ATTN_HEADS * ATTN_HD // W, (3 * ATTN_HEADS * ATTN_HD + G * W) // W
    b0 = (3 * ATTN_HEADS * ATTN_HD + 2 * G * W) // N
    c0 = b0 + G
    cwb0, cwc0 = G * W // N, G * W // N + G

    def rowblk(b, g, r):
        return b * nr + r

    def halo(b, g, r):
        return jnp.maximum((b * nr + r) * hb - 1, 0)

    cast_in, cast_specs, cast_shapes = [], [], []
    if cast is not None:
        cast_a, cast_b, lc = cast
        (_, Ra, Ca), (_, Rb, Cb) = cast_a.shape, cast_b.shape
        n_bg = batch * G
        assert Ra % n_bg == 0 and Ca % nr == 0 and Rb % nr == 0 and Cb % n_bg == 0
        blk_a, blk_b = (Ra // n_bg, Ca // nr), (Rb // nr, Cb // n_bg)
        cast_in = [pl.BlockSpec((None,) + blk_a, lambda b, g, r: (lc, b * G + g, r)),
                   pl.BlockSpec((None,) + blk_b, lambda b, g, r: (lc, r, b * G + g))]
        cast_specs = [pl.BlockSpec((None,) + blk_a, lambda b, g, r: (0, b * G + g, r)),
                      pl.BlockSpec((None,) + blk_b, lambda b, g, r: (0, r, b * G + g))]
        cast_shapes = [jax.ShapeDtypeStruct((1, Ra, Ca), BF16), jax.ShapeDtypeStruct((1, Rb, Cb), BF16)]

    body = functools.partial(_ssd_body, rows=rows, with_cast=cast is not None)
    outs = pl.pallas_call(
        body,
        grid=(batch, G, nr),
        in_specs=[
            pl.BlockSpec((rows, W), lambda b, g, r: (rowblk(b, g, r), x0 + g)),
            pl.BlockSpec((rows, N), lambda b, g, r: (rowblk(b, g, r), b0 + g)),
            pl.BlockSpec((rows, N), lambda b, g, r: (rowblk(b, g, r), c0 + g)),
            pl.BlockSpec((SUBLANES, W), lambda b, g, r: (halo(b, g, r), x0 + g)),
            pl.BlockSpec((SUBLANES, N), lambda b, g, r: (halo(b, g, r), b0 + g)),
            pl.BlockSpec((SUBLANES, N), lambda b, g, r: (halo(b, g, r), c0 + g)),
            pl.BlockSpec((rows, W), lambda b, g, r: (rowblk(b, g, r), z0 + g)),
            pl.BlockSpec((rows, 2 * LANES), lambda b, g, r: (rowblk(b, g, r), 0)),
            pl.BlockSpec((rows, 2 * LANES), lambda b, g, r: (rowblk(b, g, r), 0)),
            pl.BlockSpec((rows // SSD_CHUNK, LANES, SSD_CHUNK), lambda b, g, r: (rowblk(b, g, r), 0, 0)),
            pl.BlockSpec((SSD_CONV, W), lambda b, g, r: (0, g)),
            pl.BlockSpec((SSD_CONV, N), lambda b, g, r: (0, cwb0 + g)),
            pl.BlockSpec((SSD_CONV, N), lambda b, g, r: (0, cwc0 + g)),
            pl.BlockSpec((1, W), lambda b, g, r: (0, g)),
            pl.BlockSpec((1, N), lambda b, g, r: (0, cwb0 + g)),
            pl.BlockSpec((1, N), lambda b, g, r: (0, cwc0 + g)),
            pl.BlockSpec((1, W), lambda b, g, r: (0, g)),
            pl.BlockSpec((1, W), lambda b, g, r: (0, g)),
        ] + cast_in,
        out_specs=[pl.BlockSpec((rows, W), lambda b, g, r: (rowblk(b, g, r), g))] + cast_specs,
        out_shape=[jax.ShapeDtypeStruct((T, G * W), BF16)] + cast_shapes,
        scratch_shapes=[
            pltpu.VMEM((SUBLANES + rows, W), F32),
            pltpu.VMEM((SUBLANES + rows, N), F32),
            pltpu.VMEM((SUBLANES + rows, N), F32),
            pltpu.VMEM((N, W), F32),
        ],
        compiler_params=_cparams(("parallel", "parallel", "arbitrary")),
        name="ssd",
    )(p_main, p_main, p_main, p_main, p_main, p_main, p_main, dts, acss, acst,
      conv_w, conv_w, conv_w, conv_b, conv_b, conv_b, d_skip_x, ssd_norm,
      *([cast[0], cast[1]] if cast is not None else []))
    return outs[0], outs[1:]


def _outproj_body(x_ref, ya_ref, ys_ref, w_ref, o_ref):
    ka = ya_ref.shape[1]
    o_ref[...] = x_ref[...] + _dot(ya_ref[...], w_ref[0:ka, :]) + _dot(ys_ref[...], w_ref[ka:, :])


def _out_proj(x, ya, ys, w_out, l, *, tm=512, tn=1024):
    T, D = x.shape
    Ka, Ks = ya.shape[1], ys.shape[1]
    return pl.pallas_call(
        _outproj_body,
        grid=(D // tn, T // tm),
        in_specs=[
            pl.BlockSpec((tm, tn), lambda j, i: (i, j)),
            pl.BlockSpec((tm, Ka), lambda j, i: (i, 0)),
            pl.BlockSpec((tm, Ks), lambda j, i: (i, 0)),
            pl.BlockSpec((None, Ka + Ks, tn), lambda j, i: (l, 0, j)),
        ],
        out_specs=pl.BlockSpec((tm, tn), lambda j, i: (i, j)),
        out_shape=jax.ShapeDtypeStruct((T, D), F32),
        compiler_params=_cparams(("parallel", "parallel")),
        name="out_proj",
    )(x, ya, ys, w_out)


def _memkv_body(m_ref, nw_ref, wk_ref, wv_ref, kn_ref, k_ref, v_ref):
    h = _rms(m_ref[...], nw_ref[...]).astype(BF16)
    k = _dot(h, wk_ref[...])
    v_ref[...] = _dot(h, wv_ref[...]).astype(BF16)
    for hh in range(MEM_HEADS):
        sl = slice(hh * MEM_HD, (hh + 1) * MEM_HD)
        k_ref[:, sl] = _rms(k[:, sl], kn_ref[...]).astype(BF16)


def _mem_kv(mem, nw, wk, wv, kn, l):
    M, D = mem.shape
    Wm = wk.shape[2]
    full = lambda shape: pl.BlockSpec(shape, lambda i: (0,) * len(shape))
    layer = lambda shape: pl.BlockSpec((None,) + shape, lambda i: (l,) + (0,) * len(shape))
    return pl.pallas_call(
        _memkv_body,
        grid=(1,),
        in_specs=[full((M, D)), full((1, D)), layer((D, Wm)), layer((D, Wm)), full((1, MEM_HD))],
        out_specs=[full((M, Wm)), full((M, Wm))],
        out_shape=[jax.ShapeDtypeStruct((M, Wm), BF16)] * 2,
        compiler_params=_cparams(("arbitrary",)),
        name="mem_kv",
    )(mem, nw, wk, wv, kn)


def _memattn_body(x_ref, nw_ref, wq_ref, qn_ref, k_ref, v_ref, wo_ref, o_ref):
    x = x_ref[...]
    h = _rms(x, nw_ref[...]).astype(BF16)
    q = _dot(h, wq_ref[...])
    scale = MEM_HD ** -0.5
    outs = []
    for hh in range(MEM_HEADS):
        sl = slice(hh * MEM_HD, (hh + 1) * MEM_HD)
        qh = _rms(q[:, sl], qn_ref[...]).astype(BF16)
        s = _dot_nt(qh, k_ref[:, sl]) * scale
        e = jnp.exp(s - jnp.max(s, axis=-1, keepdims=True))
        p = e / jnp.sum(e, axis=-1, keepdims=True)
        outs.append(_dot(p.astype(BF16), v_ref[:, sl]))
    o = jnp.concatenate(outs, axis=1).astype(BF16)
    o_ref[...] = x + _dot(o, wo_ref[...])


def _mem_attn(x, nw, wq, qn, k, v, wo, l, *, seq, mem_len, tm=512):
    T, D = x.shape
    Wm = wq.shape[2]
    per_b = seq // tm
    return pl.pallas_call(
        _memattn_body,
        grid=(T // tm,),
        in_specs=[
            pl.BlockSpec((tm, D), lambda i: (i, 0)),
            pl.BlockSpec((1, D), lambda i: (0, 0)),
            pl.BlockSpec((None, D, Wm), lambda i: (l, 0, 0)),
            pl.BlockSpec((1, MEM_HD), lambda i: (0, 0)),
            pl.BlockSpec((mem_len, Wm), lambda i: (i // per_b, 0)),
            pl.BlockSpec((mem_len, Wm), lambda i: (i // per_b, 0)),
            pl.BlockSpec((None, Wm, D), lambda i: (l, 0, 0)),
        ],
        out_specs=pl.BlockSpec((tm, D), lambda i: (i, 0)),
        out_shape=jax.ShapeDtypeStruct((T, D), F32),
        compiler_params=_cparams(("parallel",)),
        name="mem_attn",
    )(x, nw, wq, qn, k, v, wo)


def _pad_lanes(v):
    return jnp.pad(v.astype(F32), (0, LANES - v.shape[0])).reshape(1, LANES)


def kernel(x, mem, ff1_norm, ff1_w_gu, ff1_w_down, mix_norm, w_in, q_norm, k_norm, conv_w, conv_b,
           dt_bias, a_log, d_skip, ssd_norm, w_out, xmem_norm, mem_norm, mem_wq, mem_wk, mem_wv,
           mem_q_norm, mem_k_norm, mem_wo, ff2_norm, ff2_w_gu, ff2_w_down):
    B, S, D = x.shape
    M = mem.shape[1]
    depth = w_in.shape[0]
    n_dt = SSD_GROUPS * SSD_HPG
    row = lambda v: v.reshape(1, -1)

    slopes = jnp.exp2(-8.0 * jnp.arange(1, ATTN_HEADS + 1, dtype=F32) / ATTN_HEADS)
    slopes = jnp.broadcast_to(slopes[:, None, None], (ATTN_HEADS, 1, LANES))
    bf = _cast_bf16
    ff1_bf = (bf(ff1_w_gu, layer=0), bf(ff1_w_down, layer=0))
    n_main = w_in.shape[2] - n_dt
    w_main = w_in[:, :, :n_main].astype(BF16)
    w_dt = jnp.pad(w_in[:, :, n_main:], ((0, 0), (0, 0), (0, LANES - n_dt))).astype(BF16)
    mem_wq, mem_wk, mem_wv, mem_wo = bf(mem_wq), bf(mem_wk), bf(mem_wv), bf(mem_wo)

    xf = x.reshape(B * S, D)
    memf = mem.reshape(B * M, D)
    for l in range(depth):
        xf = _ffn(xf, row(ff1_norm[l]), ff1_bf[0], ff1_bf[1], 0)

        p_main, dt_raw, w2_gu, w2_down = _in_proj(xf, row(mix_norm[l]), w_main, w_dt, l,
                                                  ff2_w_gu, ff2_w_down, l)
        y_attn, (w_out_l,) = _moba(p_main, row(q_norm[l]), row(k_norm[l]), slopes,
                                   [(w_out, l, w_out.shape[2])], batch=B, seq=S)
        dts, acss, acst = _ssd_dt(dt_raw, _pad_lanes(dt_bias[l]), _pad_lanes(a_log[l]))
        nxt = (ff1_w_gu, ff1_w_down, l + 1) if l + 1 < depth else None
        y_ssd, ff1_next = _ssd(p_main, dts, acss, acst, conv_w[l], row(conv_b[l]),
                               row(jnp.repeat(d_skip[l], SSD_HD)), row(ssd_norm[l]), nxt, batch=B, seq=S)
        ff1_bf = ff1_next or ff1_bf
        xf = _out_proj(xf, y_attn, y_ssd, w_out_l, 0)

        k_mem, v_mem = _mem_kv(memf, row(mem_norm[l]), mem_wk, mem_wv, row(mem_k_norm[l]), l)
        xf = _mem_attn(xf, row(xmem_norm[l]), mem_wq, row(mem_q_norm[l]), k_mem, v_mem, mem_wo, l,
                       seq=S, mem_len=M)

        xf = _ffn(xf, row(ff2_norm[l]), w2_gu, w2_down, 0)
    return xf.reshape(B, S, D)
```

```python
import functools

import jax
import jax.numpy as jnp
from jax import lax
from jax.experimental import pallas as pl
from jax.experimental.pallas import tpu as pltpu

F32 = jnp.float32
BF16 = jnp.bfloat16
EPS = 1e-6
NEG = -1e30
LOG2E = 1.4426950408889634

LANES = 128
SUBLANES = 8
VMEM_LIMIT = 56 * 1024 * 1024

ATTN_HEADS = 8
ATTN_HD = 128
MOBA_BLOCK = 256
MOBA_TOPK = 3
SSD_HD = 64
SSD_GROUPS = 8
SSD_HPG = 6
SSD_STATE = 128
SSD_CONV = 4
SSD_CHUNK = 128
MEM_HEADS = 4
MEM_HD = 128


def _cparams(sem):
    return pltpu.CompilerParams(dimension_semantics=sem, vmem_limit_bytes=VMEM_LIMIT)


def _rms(x, w):
    return x * lax.rsqrt(jnp.mean(x * x, axis=-1, keepdims=True) + EPS) * w


def _dot(a, b):
    return jnp.dot(a, b, preferred_element_type=F32)


def _dot_nt(a, b, precision=None):
    return lax.dot_general(a, b, (((1,), (1,)), ((), ())), preferred_element_type=F32,
                           precision=precision)


CAST_BLOCK_BYTES = 12 * 1024 * 1024


def _cast_body(w_ref, o_ref):
    o_ref[...] = w_ref[...].astype(BF16)


def _cast_bf16(w, layer=None):
    nl, R, C = w.shape
    first = 0
    if layer is not None:
        nl, first = 1, layer
    tr = R
    while tr * C * 4 > CAST_BLOCK_BYTES and tr % 32 == 0:
        tr //= 2
    return pl.pallas_call(
        _cast_body,
        grid=(nl, R // tr),
        in_specs=[pl.BlockSpec((None, tr, C), lambda l, i: (first + l, i, 0))],
        out_specs=pl.BlockSpec((None, tr, C), lambda l, i: (l, i, 0)),
        out_shape=jax.ShapeDtypeStruct((nl, R, C), BF16),
        compiler_params=_cparams(("parallel", "parallel")),
        name="cast_bf16",
    )(w)


def _ffn_body(x_ref, nw_ref, wg_ref, wu_ref, wd_ref, o_ref, h_ref):
    j = pl.program_id(1)

    @pl.when(j == 0)
    def _():
        x = x_ref[...]
        h_ref[...] = _rms(x, nw_ref[...]).astype(BF16)
        o_ref[...] = x

    h = h_ref[...]
    g = _dot(h, wg_ref[...])
    u = _dot(h, wu_ref[...])
    a = (0.5 * (g * jax.nn.sigmoid(g)) * u).astype(BF16)
    o_ref[...] += _dot(a, wd_ref[...])


def _ffn(x, nw, w_gu, w_down, l, *, tm=1024, tf=512):
    T, D = x.shape
    F = w_down.shape[1]
    nf = F // tf
    return pl.pallas_call(
        _ffn_body,
        grid=(T // tm, nf),
        in_specs=[
            pl.BlockSpec((tm, D), lambda i, j: (i, 0)),
            pl.BlockSpec((1, D), lambda i, j: (0, 0)),
            pl.BlockSpec((None, D, tf), lambda i, j: (l, 0, j)),
            pl.BlockSpec((None, D, tf), lambda i, j: (l, 0, j + nf)),
            pl.BlockSpec((None, tf, D), lambda i, j: (l, j, 0)),
        ],
        out_specs=pl.BlockSpec((tm, D), lambda i, j: (i, 0)),
        out_shape=jax.ShapeDtypeStruct((T, D), F32),
        scratch_shapes=[pltpu.VMEM((tm, D), BF16)],
        compiler_params=_cparams(("parallel", "arbitrary")),
        name="ffn",
    )(x, nw, w_gu, w_gu, w_down)


def _inproj_body(x_ref, nw_ref, w_ref, wdt_ref, fa_ref, fb_ref, p_ref, dt_ref, fa_o, fb_o, h_ref):
    j = pl.program_id(1)

    @pl.when(j == 0)
    def _():
        h = _rms(x_ref[...], nw_ref[...]).astype(BF16)
        h_ref[...] = h
        dt_ref[...] = _dot(h, wdt_ref[...])

    p_ref[...] = _dot(h_ref[...], w_ref[...])
    fa_o[...] = fa_ref[...].astype(BF16)
    fb_o[...] = fb_ref[...].astype(BF16)


def _in_proj(x, nw, w_main, w_dt, l, N, cast_a, cast_b, lc, *, tm=1024, tn=1024):
    T, D = x.shape
    assert N % tn == 0 and tn % LANES == 0 and N <= w_main.shape[2]
    ni, nj = T // tm, N // tn
    (_, Ra, Ca), (_, Rb, Cb) = cast_a.shape, cast_b.shape
    assert Ra % ni == 0 and Ca % nj == 0 and Rb % nj == 0 and Cb % ni == 0
    blk_a, blk_b = (Ra // ni, Ca // nj), (Rb // nj, Cb // ni)
    return pl.pallas_call(
        _inproj_body,
        grid=(ni, nj),
        in_specs=[
            pl.BlockSpec((tm, D), lambda i, j: (i, 0)),
            pl.BlockSpec((1, D), lambda i, j: (0, 0)),
            pl.BlockSpec((None, D, tn), lambda i, j: (l, 0, j)),
            pl.BlockSpec((None, D, LANES), lambda i, j: (l, 0, 0)),
            pl.BlockSpec((None,) + blk_a, lambda i, j: (lc, i, j)),
            pl.BlockSpec((None,) + blk_b, lambda i, j: (lc, j, i)),
        ],
        out_specs=[
            pl.BlockSpec((tm, tn), lambda i, j: (i, j)),
            pl.BlockSpec((tm, LANES), lambda i, j: (i, 0)),
            pl.BlockSpec((None,) + blk_a, lambda i, j: (0, i, j)),
            pl.BlockSpec((None,) + blk_b, lambda i, j: (0, j, i)),
        ],
        out_shape=[jax.ShapeDtypeStruct((T, N), F32), jax.ShapeDtypeStruct((T, LANES), F32),
                   jax.ShapeDtypeStruct((1, Ra, Ca), BF16), jax.ShapeDtypeStruct((1, Rb, Cb), BF16)],
        scratch_shapes=[pltpu.VMEM((tm, D), BF16)],
        compiler_params=_cparams(("parallel", "arbitrary")),
        name="in_proj",
    )(x, nw, w_main, w_dt, cast_a, cast_b)


MOBA_EXT = LANES
MOBA_SEL0 = 16
MOBA_VARIANT = 2
MOBA_QBLOCKS = 2


def _split3(x):
    hi = x.astype(BF16).astype(F32)
    r = x - hi
    mid = r.astype(BF16).astype(F32)
    lo = (r - mid).astype(BF16).astype(F32)
    return hi, mid, lo


def _rows16(rows, width):
    r_i = lax.broadcasted_iota(jnp.int32, (2 * SUBLANES, width), 0)
    out = jnp.zeros((2 * SUBLANES, width), F32)
    for k, row in enumerate(rows):
        out = jnp.where(r_i == k, row, out)
    return out


def _moba_body(q_ref, k_ref, v_ref, qn_ref, kn_ref, slope_ref, *rest, nb, n_cast):
    cast_in, rest = rest[:n_cast], rest[n_cast:]
    o_ref, cast_out, rest = rest[0], rest[1:1 + n_cast], rest[1 + n_cast:]
    for src, dst in zip(cast_in, cast_out):
        dst[...] = src[...].astype(BF16)
    _moba_attend(q_ref, k_ref, v_ref, qn_ref, kn_ref, slope_ref, o_ref, *rest, nb=nb)


def _moba_attend(q_ref, k_ref, v_ref, qn_ref, kn_ref, slope_ref, o_ref,
                 kx_ref, vt_ref, kmean_ref, qx_ref, *, nb):
    qi = pl.program_id(2)
    blk = MOBA_BLOCK
    seq = nb * blk
    scale = ATTN_HD ** -0.5
    slope = slope_ref[0:1, 0:1]

    @pl.when((pl.program_id(0) == 0) & (pl.program_id(1) == 0) & (qi == 0))
    def _():
        pos = lax.broadcasted_iota(jnp.int32, (seq, MOBA_EXT), 0)
        c = lax.broadcasted_iota(jnp.int32, (seq, MOBA_EXT), 1)
        n_of = lax.shift_right_logical(pos, blk.bit_length() - 1)
        j_of = pos & (blk - 1)
        ext = jnp.where(c < 3, 1, jnp.where(c < 6, n_of, jnp.where(c < 9, j_of, 0)))
        ext = jnp.where(c - MOBA_SEL0 == n_of, 1, ext)
        kx_ref[:, ATTN_HD:] = ext.astype(F32).astype(BF16)

    @pl.when(qi == 0)
    def _():
        kf = _rms(k_ref[...], kn_ref[...])
        kx_ref[:, 0:ATTN_HD] = kf.astype(BF16)
        for n in range(nb):
            kmean_ref[n:n + 1, :] = jnp.mean(kf[n * blk:(n + 1) * blk], axis=0, keepdims=True)
        vt_ref[...] = v_ref[...].T.astype(BF16)

        qn = _rms(q_ref[...], qn_ref[...])
        gate = _dot_nt(kmean_ref[...], qn, precision=lax.Precision.HIGHEST)
        n_iota = lax.broadcasted_iota(jnp.int32, gate.shape, 0).astype(F32)
        t_i = lax.broadcasted_iota(jnp.int32, (1, seq), 1)
        past_blk = n_iota < lax.shift_right_logical(t_i, blk.bit_length() - 1).astype(F32)
        gate = jnp.where(past_blk, gate, -jnp.inf)
        selb = jnp.full(gate.shape, NEG, F32)
        for _ in range(MOBA_TOPK):
            best = jnp.max(gate, axis=0, keepdims=True)
            first = jnp.min(jnp.where(gate == best, n_iota, float(nb)), axis=0, keepdims=True)
            pick = n_iota == first
            selb = jnp.where(pick, 0.0, selb)
            gate = jnp.where(pick, -jnp.inf, gate)
        selb = jnp.where(past_blk, selb, NEG)
        slope2 = slope * LOG2E
        ones = jnp.ones((1, seq), F32)
        bias_rows = (list(_split3(-slope2 * t_i.astype(F32))) + [v * ones for v in _split3(slope2 * float(blk))]
                     + [v * ones for v in _split3(slope2)])
        qx_ref[0:ATTN_HD, :] = (qn * (scale * LOG2E)).T.astype(BF16)
        qx_ref[ATTN_HD:ATTN_HD + MOBA_SEL0, :] = _rows16(bias_rows, seq).astype(BF16)
        qx_ref[ATTN_HD + MOBA_SEL0:ATTN_HD + MOBA_SEL0 + nb, :] = selb.astype(BF16)
        qx_ref[ATTN_HD + MOBA_SEL0 + nb:, :] = jnp.zeros((MOBA_EXT - MOBA_SEL0 - nb, seq), BF16)

    qw = MOBA_QBLOCKS * blk
    qcols = pl.ds(pl.multiple_of(qi * qw, qw), qw)
    qx_past = qx_ref[:, qcols]
    qx_own = jnp.concatenate([qx_ref[0:ATTN_HD + MOBA_SEL0, qcols],
                              jnp.zeros((MOBA_EXT - MOBA_SEL0, qw), BF16)], axis=0)

    key_i = lax.broadcasted_iota(jnp.int32, (blk, blk), 0)
    qry_i = lax.broadcasted_iota(jnp.int32, (blk, blk), 1)
    lanes = lambda a, u: a[:, u * blk:(u + 1) * blk]
    row_cat = lambda parts: jnp.concatenate(parts, axis=1)

    def attend(n_past):
        n = n_past * blk
        owns = [pl.ds(pl.multiple_of((qi * MOBA_QBLOCKS + u) * blk, blk), blk) for u in range(MOBA_QBLOCKS)]
        s_own = [jnp.where(qry_i >= key_i, _dot(kx_ref[owns[u], :], lanes(qx_own, u)), NEG)
                 for u in range(MOBA_QBLOCKS)]
        h = (n_past // 2) * blk
        s_a = _dot(kx_ref[0:h, :], qx_past) if h else None
        s_b = _dot(kx_ref[h:n, :], qx_past)
        parts = []
        if h:
            m_a = jnp.max(s_a, axis=0, keepdims=True)
            p_a = jnp.exp2(s_a - m_a)
            parts.append((m_a, jnp.sum(p_a, axis=0, keepdims=True), _dot(vt_ref[:, 0:h], p_a.astype(BF16))))
        m_b = jnp.maximum(jnp.max(s_b, axis=0, keepdims=True),
                          row_cat([jnp.max(so, axis=0, keepdims=True) for so in s_own]))
        p_b = jnp.exp2(s_b - m_b)
        p_own = [jnp.exp2(s_own[u] - lanes(m_b, u)) for u in range(MOBA_QBLOCKS)]
        l_b = jnp.sum(p_b, axis=0, keepdims=True) + row_cat([jnp.sum(po, axis=0, keepdims=True) for po in p_own])
        acc_b = _dot(vt_ref[:, h:n], p_b.astype(BF16)) + row_cat(
            [_dot(vt_ref[:, owns[u]], p_own[u].astype(BF16)) for u in range(MOBA_QBLOCKS)])
        parts.append((m_b, l_b, acc_b))
        m = functools.reduce(jnp.maximum, [pm for pm, _, _ in parts])
        l = sum(jnp.exp2(pm - m) * pl_ for pm, pl_, _ in parts)
        acc = sum(jnp.exp2(pm - m) * pa for pm, _, pa in parts)
        o_ref[...] = (acc / l).T.astype(o_ref.dtype)

    per = MOBA_VARIANT // MOBA_QBLOCKS
    for v in range(nb // MOBA_VARIANT):
        pl.when((qi >= v * per) & (qi < (v + 1) * per))(
            functools.partial(attend, (v + 1) * MOBA_VARIANT - 1))


def _moba(p_main, q_norm, k_norm, slopes, casts=(), *, batch, seq):
    T = p_main.shape[0]
    blk = MOBA_BLOCK
    nb = seq // blk
    assert seq % blk == 0 and nb % MOBA_VARIANT == 0 and nb <= MOBA_EXT - MOBA_SEL0
    assert MOBA_VARIANT % MOBA_QBLOCKS == 0
    H = ATTN_HEADS
    qw = MOBA_QBLOCKS * blk
    nq = seq // qw
    cast_in, cast_out, cast_shapes = [], [], []
    for w, lc, C in casts:
        R = w.shape[1]
        assert R % (batch * H) == 0 and C % nq == 0 and (C // nq) % LANES == 0
        cblk = (None, R // (batch * H), C // nq)
        cast_in.append(pl.BlockSpec(cblk, lambda b, h, i, lc=lc: (lc, b * H + h, i)))
        cast_out.append(pl.BlockSpec(cblk, lambda b, h, i: (0, b * H + h, i)))
        cast_shapes.append(jax.ShapeDtypeStruct((1, R, C), BF16))
    body = functools.partial(_moba_body, nb=nb, n_cast=len(casts))
    outs = pl.pallas_call(
        body,
        grid=(batch, H, nq),
        in_specs=[
            pl.BlockSpec((seq, ATTN_HD), lambda b, h, i: (b, h)),
            pl.BlockSpec((seq, ATTN_HD), lambda b, h, i: (b, H + h)),
            pl.BlockSpec((seq, ATTN_HD), lambda b, h, i: (b, 2 * H + h)),
            pl.BlockSpec((1, ATTN_HD), lambda b, h, i: (0, 0)),
            pl.BlockSpec((1, ATTN_HD), lambda b, h, i: (0, 0)),
            pl.BlockSpec((None, 1, LANES), lambda b, h, i: (h, 0, 0)),
        ] + cast_in,
        out_specs=[pl.BlockSpec((qw, ATTN_HD), lambda b, h, i: (b * nq + i, h))] + cast_out,
        out_shape=[jax.ShapeDtypeStruct((T, H * ATTN_HD), BF16)] + cast_shapes,
        scratch_shapes=[
            pltpu.VMEM((seq, ATTN_HD + MOBA_EXT), BF16),
            pltpu.VMEM((ATTN_HD, seq), BF16),
            pltpu.VMEM((nb, ATTN_HD), F32),
            pltpu.VMEM((ATTN_HD + MOBA_EXT, seq), BF16),
        ],
        compiler_params=_cparams(("arbitrary", "arbitrary", "arbitrary")),
        name="moba",
    )(p_main, p_main, p_main, q_norm, k_norm, slopes, *[w for w, _, _ in casts])
    return outs[0], outs[1:]


def _softplus(x):
    return jnp.maximum(x, 0.0) + jnp.log1p(jnp.exp(-jnp.abs(x)))


def _split_hi_lo(x):
    hi = x.astype(BF16)
    lo = (x - hi.astype(F32)).astype(BF16)
    return jnp.concatenate([hi, lo], axis=1)


def _ssd_dt_body(dt_ref, dtb_ref, alog_ref, dts_ref, acss_ref, acst_ref, *, rows):
    L = SSD_CHUNK
    li = lax.broadcasted_iota(jnp.int32, (L, L), 0)
    si = lax.broadcasted_iota(jnp.int32, (L, L), 1)
    tril = (li >= si).astype(F32)
    a2_row = -jnp.exp(alog_ref[...]) * LOG2E
    for c in range(rows // L):
        rs = slice(c * L, (c + 1) * L)
        dtv = _softplus(dt_ref[rs, :] + dtb_ref[...])
        acs = jnp.dot(tril, dtv * a2_row, preferred_element_type=F32,
                      precision=lax.Precision.HIGHEST)
        dts_ref[rs, :] = _split_hi_lo(dtv)
        acss_ref[rs, :] = _split_hi_lo(acs)
        acst_ref[c] = acs.T


def _ssd_dt(dt_raw, dt_bias_p, a_log_p, *, rows=1024):
    T = dt_raw.shape[0]
    L = SSD_CHUNK
    body = functools.partial(_ssd_dt_body, rows=rows)
    return pl.pallas_call(
        body,
        grid=(T // rows,),
        in_specs=[
            pl.BlockSpec((rows, LANES), lambda i: (i, 0)),
            pl.BlockSpec((1, LANES), lambda i: (0, 0)),
            pl.BlockSpec((1, LANES), lambda i: (0, 0)),
        ],
        out_specs=[
            pl.BlockSpec((rows, 2 * LANES), lambda i: (i, 0)),
            pl.BlockSpec((rows, 2 * LANES), lambda i: (i, 0)),
            pl.BlockSpec((rows // L, LANES, L), lambda i: (i, 0, 0)),
        ],
        out_shape=[jax.ShapeDtypeStruct((T, 2 * LANES), BF16), jax.ShapeDtypeStruct((T, 2 * LANES), BF16),
                   jax.ShapeDtypeStruct((T // L, LANES, L), F32)],
        compiler_params=_cparams(("parallel",)),
        name="ssd_dt",
    )(dt_raw, dt_bias_p, a_log_p)


def _ssd_body(xs_ref, bm_ref, cm_ref, xs_h_ref, bm_h_ref, cm_h_ref, z_ref, dts_ref, acss_ref, acst_ref,
              wx_ref, wb_ref, wc_ref, bx_ref, bb_ref, bc_ref, dsk_ref, nw_ref, *rest, rows, with_cast):
    if with_cast:
        fa_ref, fb_ref, o_ref, fa_o, fb_o, ux_ref, ub_ref, uc_ref, st_ref = rest
        fa_o[...] = fa_ref[...].astype(BF16)
        fb_o[...] = fb_ref[...].astype(BF16)
    else:
        o_ref, ux_ref, ub_ref, uc_ref, st_ref = rest
    g = pl.program_id(1)
    r = pl.program_id(2)
    L = SSD_CHUNK
    P = SSD_HD
    W = SSD_HPG * P
    HALO = SUBLANES

    @pl.when(r == 0)
    def _():
        st_ref[...] = jnp.zeros_like(st_ref)

    first = r == 0

    def conv(u_ref, halo_ref, buf_ref, w_ref, b_ref):
        halo = halo_ref[...]
        buf_ref[0:HALO, :] = jnp.where(first, jnp.zeros_like(halo), halo)
        buf_ref[HALO:, :] = u_ref[...]
        acc = b_ref[...] + w_ref[SSD_CONV - 1:SSD_CONV, :] * u_ref[...]
        for k in range(SSD_CONV - 1):
            shift = SSD_CONV - 1 - k
            acc = acc + w_ref[k:k + 1, :] * buf_ref[HALO - shift:HALO - shift + rows, :]
        buf_ref[HALO:, :] = acc * jax.nn.sigmoid(acc)

    conv(xs_ref, xs_h_ref, ux_ref, wx_ref, bx_ref)
    conv(bm_ref, bm_h_ref, ub_ref, wb_ref, bb_ref)
    conv(cm_ref, cm_h_ref, uc_ref, wc_ref, bc_ref)

    k_i = lax.broadcasted_iota(jnp.int32, (2 * LANES, W), 0)
    c_i = lax.broadcasted_iota(jnp.int32, (2 * LANES, W), 1)
    head = g * SSD_HPG + lax.shift_right_logical(c_i, P.bit_length() - 1)
    expand = ((k_i == head) | (k_i == head + LANES)).astype(BF16)

    li = lax.broadcasted_iota(jnp.int32, (L, L), 0)
    si = lax.broadcasted_iota(jnp.int32, (L, L), 1)
    causal = li >= si
    lane = lax.broadcasted_iota(jnp.int32, (L, LANES), 1)

    for c in range(rows // L):
        r0 = c * L
        rs = slice(HALO + r0, HALO + r0 + L)
        dt_x = _dot(dts_ref[r0:r0 + L, :], expand)
        acs_x = _dot(acss_ref[r0:r0 + L, :], expand)
        acs_last = acs_x[L - 1:L, :]

        xs = ux_ref[rs, :]
        bmat = ub_ref[rs, :]
        cmat = uc_ref[rs, :]
        x = xs * dt_x
        xb = x.astype(BF16)
        cb16 = cmat.astype(BF16)
        cb = jnp.where(causal, _dot_nt(cb16, bmat.astype(BF16)), 0.0)

        tiles = []
        for m in range(W // LANES):
            res = []
            for jj in range(2):
                j = 2 * m + jj
                col = acs_x[:, j * P:j * P + 1]
                row = acst_ref[c, pl.ds(g * SSD_HPG + j, 1), :]
                dec = jnp.exp2(jnp.minimum(col - row, 0.0))
                sc = (cb * dec).astype(BF16)
                res.append(_dot(sc, xb[:, m * LANES:(m + 1) * LANES]))
            tiles.append(jnp.where(lane < P, res[0], res[1]))
        y = jnp.concatenate(tiles, axis=1)

        st = st_ref[...]
        y = y + _dot(cb16, st.astype(BF16)) * jnp.exp2(acs_x)
        xd = (x * jnp.exp2(acs_last - acs_x)).astype(BF16)
        st_ref[...] = st * jnp.exp2(acs_last) + _dot(bmat.T.astype(BF16), xd)

        y = y + dsk_ref[...] * xs
        z = z_ref[r0:r0 + L, :]
        gt = y * (z * jax.nn.sigmoid(z))
        gn = gt * lax.rsqrt(jnp.mean(gt * gt, axis=-1, keepdims=True) + EPS)
        o_ref[r0:r0 + L, :] = (gn * nw_ref[...]).astype(o_ref.dtype)


def _ssd(p_main, dts, acss, acst, conv_w, conv_b, d_skip_x, ssd_norm, cast=None, *, batch, seq, rows=1024):
    T = p_main.shape[0]
    G = SSD_GROUPS
    N = SSD_STATE
    W = SSD_HPG * SSD_HD
    nr = seq // rows
    hb = rows // SUBLANES
    z0, x0 = 3 * ATTN_HEADS * ATTN_HD // W, (3 * ATTN_HEADS * ATTN_HD + G * W) // W
    b0 = (3 * ATTN_HEADS * ATTN_HD + 2 * G * W) // N
    c0 = b0 + G
    cwb0, cwc0 = G * W // N, G * W // N + G

    def rowblk(b, g, r):
        return b * nr + r

    def halo(b, g, r):
        return jnp.maximum((b * nr + r) * hb - 1, 0)

    cast_in, cast_specs, cast_shapes = [], [], []
    if cast is not None:
        cast_a, cast_b, lc = cast
        (_, Ra, Ca), (_, Rb, Cb) = cast_a.shape, cast_b.shape
        n_bg = batch * G
        assert Ra % n_bg == 0 and Ca % nr == 0 and Rb % nr == 0 and Cb % n_bg == 0
        blk_a, blk_b = (Ra // n_bg, Ca // nr), (Rb // nr, Cb // n_bg)
        cast_in = [pl.BlockSpec((None,) + blk_a, lambda b, g, r: (lc, b * G + g, r)),
                   pl.BlockSpec((None,) + blk_b, lambda b, g, r: (lc, r, b * G + g))]
        cast_specs = [pl.BlockSpec((None,) + blk_a, lambda b, g, r: (0, b * G + g, r)),
                      pl.BlockSpec((None,) + blk_b, lambda b, g, r: (0, r, b * G + g))]
        cast_shapes = [jax.ShapeDtypeStruct((1, Ra, Ca), BF16), jax.ShapeDtypeStruct((1, Rb, Cb), BF16)]

    body = functools.partial(_ssd_body, rows=rows, with_cast=cast is not None)
    outs = pl.pallas_call(
        body,
        grid=(batch, G, nr),
        in_specs=[
            pl.BlockSpec((rows, W), lambda b, g, r: (rowblk(b, g, r), x0 + g)),
            pl.BlockSpec((rows, N), lambda b, g, r: (rowblk(b, g, r), b0 + g)),
            pl.BlockSpec((rows, N), lambda b, g, r: (rowblk(b, g, r), c0 + g)),
            pl.BlockSpec((SUBLANES, W), lambda b, g, r: (halo(b, g, r), x0 + g)),
            pl.BlockSpec((SUBLANES, N), lambda b, g, r: (halo(b, g, r), b0 + g)),
            pl.BlockSpec((SUBLANES, N), lambda b, g, r: (halo(b, g, r), c0 + g)),
            pl.BlockSpec((rows, W), lambda b, g, r: (rowblk(b, g, r), z0 + g)),
            pl.BlockSpec((rows, 2 * LANES), lambda b, g, r: (rowblk(b, g, r), 0)),
            pl.BlockSpec((rows, 2 * LANES), lambda b, g, r: (rowblk(b, g, r), 0)),
            pl.BlockSpec((rows // SSD_CHUNK, LANES, SSD_CHUNK), lambda b, g, r: (rowblk(b, g, r), 0, 0)),
            pl.BlockSpec((SSD_CONV, W), lambda b, g, r: (0, g)),
            pl.BlockSpec((SSD_CONV, N), lambda b, g, r: (0, cwb0 + g)),
            pl.BlockSpec((SSD_CONV, N), lambda b, g, r: (0, cwc0 + g)),
            pl.BlockSpec((1, W), lambda b, g, r: (0, g)),
            pl.BlockSpec((1, N), lambda b, g, r: (0, cwb0 + g)),
            pl.BlockSpec((1, N), lambda b, g, r: (0, cwc0 + g)),
            pl.BlockSpec((1, W), lambda b, g, r: (0, g)),
            pl.BlockSpec((1, W), lambda b, g, r: (0, g)),
        ] + cast_in,
        out_specs=[pl.BlockSpec((rows, W), lambda b, g, r: (rowblk(b, g, r), g))] + cast_specs,
        out_shape=[jax.ShapeDtypeStruct((T, G * W), BF16)] + cast_shapes,
        scratch_shapes=[
            pltpu.VMEM((SUBLANES + rows, W), F32),
            pltpu.VMEM((SUBLANES + rows, N), F32),
            pltpu.VMEM((SUBLANES + rows, N), F32),
            pltpu.VMEM((N, W), F32),
        ],
        compiler_params=_cparams(("parallel", "parallel", "arbitrary")),
        name="ssd",
    )(p_main, p_main, p_main, p_main, p_main, p_main, p_main, dts, acss, acst,
      conv_w, conv_w, conv_w, conv_b, conv_b, conv_b, d_skip_x, ssd_norm,
      *([cast[0], cast[1]] if cast is not None else []))
    return outs[0], outs[1:]


def _outproj_body(x_ref, ya_ref, ys_ref, w_ref, o_ref):
    ka = ya_ref.shape[1]
    o_ref[...] = x_ref[...] + _dot(ya_ref[...], w_ref[0:ka, :]) + _dot(ys_ref[...], w_ref[ka:, :])


def _out_proj(x, ya, ys, w_out, l, *, tm=512, tn=1024):
    T, D = x.shape
    Ka, Ks = ya.shape[1], ys.shape[1]
    return pl.pallas_call(
        _outproj_body,
        grid=(D // tn, T // tm),
        in_specs=[
            pl.BlockSpec((tm, tn), lambda j, i: (i, j)),
            pl.BlockSpec((tm, Ka), lambda j, i: (i, 0)),
            pl.BlockSpec((tm, Ks), lambda j, i: (i, 0)),
            pl.BlockSpec((None, Ka + Ks, tn), lambda j, i: (l, 0, j)),
        ],
        out_specs=pl.BlockSpec((tm, tn), lambda j, i: (i, j)),
        out_shape=jax.ShapeDtypeStruct((T, D), F32),
        compiler_params=_cparams(("parallel", "parallel")),
        name="out_proj",
    )(x, ya, ys, w_out)


def _memkv_body(m_ref, nw_ref, wk_ref, wv_ref, kn_ref, k_ref, v_ref):
    h = _rms(m_ref[...], nw_ref[...]).astype(BF16)
    k = _dot(h, wk_ref[...])
    v_ref[...] = _dot(h, wv_ref[...]).astype(BF16)
    for hh in range(MEM_HEADS):
        sl = slice(hh * MEM_HD, (hh + 1) * MEM_HD)
        k_ref[:, sl] = _rms(k[:, sl], kn_ref[...]).astype(BF16)


def _mem_kv(mem, nw, wk, wv, kn, l):
    M, D = mem.shape
    Wm = wk.shape[2]
    full = lambda shape: pl.BlockSpec(shape, lambda i: (0,) * len(shape))
    layer = lambda shape: pl.BlockSpec((None,) + shape, lambda i: (l,) + (0,) * len(shape))
    return pl.pallas_call(
        _memkv_body,
        grid=(1,),
        in_specs=[full((M, D)), full((1, D)), layer((D, Wm)), layer((D, Wm)), full((1, MEM_HD))],
        out_specs=[full((M, Wm)), full((M, Wm))],
        out_shape=[jax.ShapeDtypeStruct((M, Wm), BF16)] * 2,
        compiler_params=_cparams(("arbitrary",)),
        name="mem_kv",
    )(mem, nw, wk, wv, kn)


def _memattn_body(x_ref, nw_ref, wq_ref, qn_ref, k_ref, v_ref, wo_ref, o_ref):
    x = x_ref[...]
    h = _rms(x, nw_ref[...]).astype(BF16)
    q = _dot(h, wq_ref[...])
    scale = MEM_HD ** -0.5
    outs = []
    for hh in range(MEM_HEADS):
        sl = slice(hh * MEM_HD, (hh + 1) * MEM_HD)
        qh = _rms(q[:, sl], qn_ref[...]).astype(BF16)
        s = _dot_nt(qh, k_ref[:, sl]) * scale
        e = jnp.exp(s - jnp.max(s, axis=-1, keepdims=True))
        p = e / jnp.sum(e, axis=-1, keepdims=True)
        outs.append(_dot(p.astype(BF16), v_ref[:, sl]))
    o = jnp.concatenate(outs, axis=1).astype(BF16)
    o_ref[...] = x + _dot(o, wo_ref[...])


def _mem_attn(x, nw, wq, qn, k, v, wo, l, *, seq, mem_len, tm=512):
    T, D = x.shape
    Wm = wq.shape[2]
    per_b = seq // tm
    return pl.pallas_call(
        _memattn_body,
        grid=(T // tm,),
        in_specs=[
            pl.BlockSpec((tm, D), lambda i: (i, 0)),
            pl.BlockSpec((1, D), lambda i: (0, 0)),
            pl.BlockSpec((None, D, Wm), lambda i: (l, 0, 0)),
            pl.BlockSpec((1, MEM_HD), lambda i: (0, 0)),
            pl.BlockSpec((mem_len, Wm), lambda i: (i // per_b, 0)),
            pl.BlockSpec((mem_len, Wm), lambda i: (i // per_b, 0)),
            pl.BlockSpec((None, Wm, D), lambda i: (l, 0, 0)),
        ],
        out_specs=pl.BlockSpec((tm, D), lambda i: (i, 0)),
        out_shape=jax.ShapeDtypeStruct((T, D), F32),
        compiler_params=_cparams(("parallel",)),
        name="mem_attn",
    )(x, nw, wq, qn, k, v, wo)


def _pad_lanes(v):
    return jnp.pad(v.astype(F32), (0, LANES - v.shape[0])).reshape(1, LANES)


def kernel(x, mem, ff1_norm, ff1_w_gu, ff1_w_down, mix_norm, w_in, q_norm, k_norm, conv_w, conv_b,
           dt_bias, a_log, d_skip, ssd_norm, w_out, xmem_norm, mem_norm, mem_wq, mem_wk, mem_wv,
           mem_q_norm, mem_k_norm, mem_wo, ff2_norm, ff2_w_gu, ff2_w_down):
    B, S, D = x.shape
    M = mem.shape[1]
    depth = w_in.shape[0]
    n_dt = SSD_GROUPS * SSD_HPG
    row = lambda v: v.reshape(1, -1)

    slopes = jnp.exp2(-8.0 * jnp.arange(1, ATTN_HEADS + 1, dtype=F32) / ATTN_HEADS)
    slopes = jnp.broadcast_to(slopes[:, None, None], (ATTN_HEADS, 1, LANES))
    bf = _cast_bf16
    ff1_bf = (bf(ff1_w_gu, layer=0), bf(ff1_w_down, layer=0))
    n_main = w_in.shape[2] - n_dt
    w_main = w_in.astype(BF16)
    w_dt = jnp.pad(w_in[:, :, n_main:], ((0, 0), (0, 0), (0, LANES - n_dt))).astype(BF16)
    mem_wq, mem_wk, mem_wv, mem_wo = bf(mem_wq), bf(mem_wk), bf(mem_wv), bf(mem_wo)

    xf = x.reshape(B * S, D)
    memf = mem.reshape(B * M, D)
    for l in range(depth):
        xf = _ffn(xf, row(ff1_norm[l]), ff1_bf[0], ff1_bf[1], 0)

        p_main, dt_raw, w2_gu, w2_down = _in_proj(xf, row(mix_norm[l]), w_main, w_dt, l, n_main,
                                                  ff2_w_gu, ff2_w_down, l)
        y_attn, (w_out_l,) = _moba(p_main, row(q_norm[l]), row(k_norm[l]), slopes,
                                   [(w_out, l, w_out.shape[2])], batch=B, seq=S)
        dts, acss, acst = _ssd_dt(dt_raw, _pad_lanes(dt_bias[l]), _pad_lanes(a_log[l]))
        nxt = (ff1_w_gu, ff1_w_down, l + 1) if l + 1 < depth else None
        y_ssd, ff1_next = _ssd(p_main, dts, acss, acst, conv_w[l], row(conv_b[l]),
                               row(jnp.repeat(d_skip[l], SSD_HD)), row(ssd_norm[l]), nxt, batch=B, seq=S)
        ff1_bf = ff1_next or ff1_bf
        xf = _out_proj(xf, y_attn, y_ssd, w_out_l, 0)

        k_mem, v_mem = _mem_kv(memf, row(mem_norm[l]), mem_wk, mem_wv, row(mem_k_norm[l]), l)
        xf = _mem_attn(xf, row(xmem_norm[l]), mem_wq, row(mem_q_norm[l]), k_mem, v_mem, mem_wo, l,
                       seq=S, mem_len=M)

        xf = _ffn(xf, row(ff2_norm[l]), w2_gu, w2_down, 0)
    return xf.reshape(B, S, D)
```

```python
import functools

import jax
import jax.numpy as jnp
from jax import lax
from jax.experimental import pallas as pl
from jax.experimental.pallas import tpu as pltpu

F32 = jnp.float32
BF16 = jnp.bfloat16
EPS = 1e-6
NEG = -1e30
LOG2E = 1.4426950408889634

LANES = 128
SUBLANES = 8
VMEM_LIMIT = 56 * 1024 * 1024

ATTN_HEADS = 8
ATTN_HD = 128
MOBA_BLOCK = 256
MOBA_TOPK = 3
SSD_HD = 64
SSD_GROUPS = 8
SSD_HPG = 6
SSD_STATE = 128
SSD_CONV = 4
SSD_CHUNK = 128
MEM_HEADS = 4
MEM_HD = 128


def _cparams(sem):
    return pltpu.CompilerParams(dimension_semantics=sem, vmem_limit_bytes=VMEM_LIMIT)


def _rms(x, w):
    return x * lax.rsqrt(jnp.mean(x * x, axis=-1, keepdims=True) + EPS) * w


def _dot(a, b):
    return jnp.dot(a, b, preferred_element_type=F32)


def _dot_nt(a, b, precision=None):
    return lax.dot_general(a, b, (((1,), (1,)), ((), ())), preferred_element_type=F32,
                           precision=precision)


CAST_BLOCK_BYTES = 12 * 1024 * 1024


def _cast_body(w_ref, o_ref):
    o_ref[...] = w_ref[...].astype(BF16)


def _cast_bf16(w, layer=None):
    nl, R, C = w.shape
    first = 0
    if layer is not None:
        nl, first = 1, layer
    tr = R
    while tr * C * 4 > CAST_BLOCK_BYTES and tr % 32 == 0:
        tr //= 2
    return pl.pallas_call(
        _cast_body,
        grid=(nl, R // tr),
        in_specs=[pl.BlockSpec((None, tr, C), lambda l, i: (first + l, i, 0))],
        out_specs=pl.BlockSpec((None, tr, C), lambda l, i: (l, i, 0)),
        out_shape=jax.ShapeDtypeStruct((nl, R, C), BF16),
        compiler_params=_cparams(("parallel", "parallel")),
        name="cast_bf16",
    )(w)


def _ffn_body(x_ref, nw_ref, wg_ref, wu_ref, wd_ref, o_ref, h_ref):
    j = pl.program_id(1)

    @pl.when(j == 0)
    def _():
        x = x_ref[...]
        h_ref[...] = _rms(x, nw_ref[...]).astype(BF16)
        o_ref[...] = x

    h = h_ref[...]
    g = _dot(h, wg_ref[...])
    u = _dot(h, wu_ref[...])
    a = (0.5 * (g * jax.nn.sigmoid(g)) * u).astype(BF16)
    o_ref[...] += _dot(a, wd_ref[...])


def _ffn(x, nw, w_gu, w_down, l, *, tm=1024, tf=512):
    T, D = x.shape
    F = w_down.shape[1]
    nf = F // tf
    return pl.pallas_call(
        _ffn_body,
        grid=(T // tm, nf),
        in_specs=[
            pl.BlockSpec((tm, D), lambda i, j: (i, 0)),
            pl.BlockSpec((1, D), lambda i, j: (0, 0)),
            pl.BlockSpec((None, D, tf), lambda i, j: (l, 0, j)),
            pl.BlockSpec((None, D, tf), lambda i, j: (l, 0, j + nf)),
            pl.BlockSpec((None, tf, D), lambda i, j: (l, j, 0)),
        ],
        out_specs=pl.BlockSpec((tm, D), lambda i, j: (i, 0)),
        out_shape=jax.ShapeDtypeStruct((T, D), F32),
        scratch_shapes=[pltpu.VMEM((tm, D), BF16)],
        compiler_params=_cparams(("parallel", "arbitrary")),
        name="ffn",
    )(x, nw, w_gu, w_gu, w_down)


def _inproj_body(x_ref, nw_ref, w_ref, wdt_ref, fa_ref, fb_ref, p_ref, dt_ref, fa_o, fb_o, h_ref):
    j = pl.program_id(1)

    @pl.when(j == 0)
    def _():
        h = _rms(x_ref[...], nw_ref[...]).astype(BF16)
        h_ref[...] = h
        dt_ref[...] = _dot(h, wdt_ref[...])

    p_ref[...] = _dot(h_ref[...], w_ref[...])
    fa_o[...] = fa_ref[...].astype(BF16)
    fb_o[...] = fb_ref[...].astype(BF16)


def _in_proj(x, nw, w_main, w_dt, l, N, cast_a, cast_b, lc, *, tm=1024, tn=1024):
    T, D = x.shape
    assert N % tn == 0 and tn % LANES == 0 and N <= w_main.shape[2]
    ni, nj = T // tm, N // tn
    (_, Ra, Ca), (_, Rb, Cb) = cast_a.shape, cast_b.shape
    assert Ra % ni == 0 and Ca % nj == 0 and Rb % nj == 0 and Cb % ni == 0
    blk_a, blk_b = (Ra // ni, Ca // nj), (Rb // nj, Cb // ni)
    return pl.pallas_call(
        _inproj_body,
        grid=(ni, nj),
        in_specs=[
            pl.BlockSpec((tm, D), lambda i, j: (i, 0)),
            pl.BlockSpec((1, D), lambda i, j: (0, 0)),
            pl.BlockSpec((None, D, tn), lambda i, j: (l, 0, j)),
            pl.BlockSpec((None, D, LANES), lambda i, j: (l, 0, 0)),
            pl.BlockSpec((None,) + blk_a, lambda i, j: (lc, i, j)),
            pl.BlockSpec((None,) + blk_b, lambda i, j: (lc, j, i)),
        ],
        out_specs=[
            pl.BlockSpec((tm, tn), lambda i, j: (i, j)),
            pl.BlockSpec((tm, LANES), lambda i, j: (i, 0)),
            pl.BlockSpec((None,) + blk_a, lambda i, j: (0, i, j)),
            pl.BlockSpec((None,) + blk_b, lambda i, j: (0, j, i)),
        ],
        out_shape=[jax.ShapeDtypeStruct((T, N), F32), jax.ShapeDtypeStruct((T, LANES), F32),
                   jax.ShapeDtypeStruct((1, Ra, Ca), BF16), jax.ShapeDtypeStruct((1, Rb, Cb), BF16)],
        scratch_shapes=[pltpu.VMEM((tm, D), BF16)],
        compiler_params=_cparams(("parallel", "arbitrary")),
        name="in_proj",
    )(x, nw, w_main, w_dt, cast_a, cast_b)


MOBA_EXT = LANES
MOBA_SEL0 = 16
MOBA_VARIANT = 2
MOBA_QBLOCKS = 2


def _split3(x):
    hi = x.astype(BF16).astype(F32)
    r = x - hi
    mid = r.astype(BF16).astype(F32)
    lo = (r - mid).astype(BF16).astype(F32)
    return hi, mid, lo


def _rows16(rows, width):
    r_i = lax.broadcasted_iota(jnp.int32, (2 * SUBLANES, width), 0)
    out = jnp.zeros((2 * SUBLANES, width), F32)
    for k, row in enumerate(rows):
        out = jnp.where(r_i == k, row, out)
    return out


def _moba_body(q_ref, k_ref, v_ref, qn_ref, kn_ref, slope_ref, *rest, nb, n_cast):
    cast_in, rest = rest[:n_cast], rest[n_cast:]
    o_ref, cast_out, rest = rest[0], rest[1:1 + n_cast], rest[1 + n_cast:]
    for src, dst in zip(cast_in, cast_out):
        dst[...] = src[...].astype(BF16)
    _moba_attend(q_ref, k_ref, v_ref, qn_ref, kn_ref, slope_ref, o_ref, *rest, nb=nb)


def _moba_attend(q_ref, k_ref, v_ref, qn_ref, kn_ref, slope_ref, o_ref,
                 kx_ref, vt_ref, kmean_ref, qx_ref, *, nb):
    qi = pl.program_id(2)
    blk = MOBA_BLOCK
    seq = nb * blk
    scale = ATTN_HD ** -0.5
    slope = slope_ref[0:1, 0:1]

    @pl.when((pl.program_id(0) == 0) & (pl.program_id(1) == 0) & (qi == 0))
    def _():
        pos = lax.broadcasted_iota(jnp.int32, (seq, MOBA_EXT), 0)
        c = lax.broadcasted_iota(jnp.int32, (seq, MOBA_EXT), 1)
        n_of = lax.shift_right_logical(pos, blk.bit_length() - 1)
        j_of = pos & (blk - 1)
        ext = jnp.where(c < 3, 1, jnp.where(c < 6, n_of, jnp.where(c < 9, j_of, 0)))
        ext = jnp.where(c - MOBA_SEL0 == n_of, 1, ext)
        kx_ref[:, ATTN_HD:] = ext.astype(F32).astype(BF16)

    @pl.when(qi == 0)
    def _():
        kf = _rms(k_ref[...], kn_ref[...])
        kx_ref[:, 0:ATTN_HD] = kf.astype(BF16)
        for n in range(nb):
            kmean_ref[n:n + 1, :] = jnp.mean(kf[n * blk:(n + 1) * blk], axis=0, keepdims=True)
        vt_ref[...] = v_ref[...].T.astype(BF16)

        qn = _rms(q_ref[...], qn_ref[...])
        gate = _dot_nt(kmean_ref[...], qn, precision=lax.Precision.HIGHEST)
        n_iota = lax.broadcasted_iota(jnp.int32, gate.shape, 0).astype(F32)
        t_i = lax.broadcasted_iota(jnp.int32, (1, seq), 1)
        past_blk = n_iota < lax.shift_right_logical(t_i, blk.bit_length() - 1).astype(F32)
        gate = jnp.where(past_blk, gate, -jnp.inf)
        selb = jnp.full(gate.shape, NEG, F32)
        for _ in range(MOBA_TOPK):
            best = jnp.max(gate, axis=0, keepdims=True)
            first = jnp.min(jnp.where(gate == best, n_iota, float(nb)), axis=0, keepdims=True)
            pick = n_iota == first
            selb = jnp.where(pick, 0.0, selb)
            gate = jnp.where(pick, -jnp.inf, gate)
        selb = jnp.where(past_blk, selb, NEG)
        slope2 = slope * LOG2E
        ones = jnp.ones((1, seq), F32)
        bias_rows = (list(_split3(-slope2 * t_i.astype(F32))) + [v * ones for v in _split3(slope2 * float(blk))]
                     + [v * ones for v in _split3(slope2)])
        qx_ref[0:ATTN_HD, :] = (qn * (scale * LOG2E)).T.astype(BF16)
        qx_ref[ATTN_HD:ATTN_HD + MOBA_SEL0, :] = _rows16(bias_rows, seq).astype(BF16)
        qx_ref[ATTN_HD + MOBA_SEL0:ATTN_HD + MOBA_SEL0 + nb, :] = selb.astype(BF16)
        qx_ref[ATTN_HD + MOBA_SEL0 + nb:, :] = jnp.zeros((MOBA_EXT - MOBA_SEL0 - nb, seq), BF16)

    qw = MOBA_QBLOCKS * blk
    qcols = pl.ds(pl.multiple_of(qi * qw, qw), qw)
    qx_past = qx_ref[:, qcols]
    qx_own = jnp.concatenate([qx_ref[0:ATTN_HD + MOBA_SEL0, qcols],
                              jnp.zeros((MOBA_EXT - MOBA_SEL0, qw), BF16)], axis=0)

    key_i = lax.broadcasted_iota(jnp.int32, (blk, blk), 0)
    qry_i = lax.broadcasted_iota(jnp.int32, (blk, blk), 1)
    lanes = lambda a, u: a[:, u * blk:(u + 1) * blk]
    row_cat = lambda parts: jnp.concatenate(parts, axis=1)

    def attend(n_past):
        n = n_past * blk
        owns = [pl.ds(pl.multiple_of((qi * MOBA_QBLOCKS + u) * blk, blk), blk) for u in range(MOBA_QBLOCKS)]
        s_own = [jnp.where(qry_i >= key_i, _dot(kx_ref[owns[u], :], lanes(qx_own, u)), NEG)
                 for u in range(MOBA_QBLOCKS)]
        h = (n_past // 2) * blk
        s_a = _dot(kx_ref[0:h, :], qx_past) if h else None
        s_b = _dot(kx_ref[h:n, :], qx_past)
        parts = []
        if h:
            m_a = jnp.max(s_a, axis=0, keepdims=True)
            p_a = jnp.exp2(s_a - m_a)
            parts.append((m_a, jnp.sum(p_a, axis=0, keepdims=True), _dot(vt_ref[:, 0:h], p_a.astype(BF16))))
        m_b = jnp.maximum(jnp.max(s_b, axis=0, keepdims=True),
                          row_cat([jnp.max(so, axis=0, keepdims=True) for so in s_own]))
        p_b = jnp.exp2(s_b - m_b)
        p_own = [jnp.exp2(s_own[u] - lanes(m_b, u)) for u in range(MOBA_QBLOCKS)]
        l_b = jnp.sum(p_b, axis=0, keepdims=True) + row_cat([jnp.sum(po, axis=0, keepdims=True) for po in p_own])
        acc_b = _dot(vt_ref[:, h:n], p_b.astype(BF16)) + row_cat(
            [_dot(vt_ref[:, owns[u]], p_own[u].astype(BF16)) for u in range(MOBA_QBLOCKS)])
        parts.append((m_b, l_b, acc_b))
        m = functools.reduce(jnp.maximum, [pm for pm, _, _ in parts])
        l = sum(jnp.exp2(pm - m) * pl_ for pm, pl_, _ in parts)
        acc = sum(jnp.exp2(pm - m) * pa for pm, _, pa in parts)
        o_ref[...] = (acc / l).T.astype(o_ref.dtype)

    per = MOBA_VARIANT // MOBA_QBLOCKS
    for v in range(nb // MOBA_VARIANT):
        pl.when((qi >= v * per) & (qi < (v + 1) * per))(
            functools.partial(attend, (v + 1) * MOBA_VARIANT - 1))


def _moba(p_main, q_norm, k_norm, slopes, casts=(), *, batch, seq):
    T = p_main.shape[0]
    blk = MOBA_BLOCK
    nb = seq // blk
    assert seq % blk == 0 and nb % MOBA_VARIANT == 0 and nb <= MOBA_EXT - MOBA_SEL0
    assert MOBA_VARIANT % MOBA_QBLOCKS == 0
    H = ATTN_HEADS
    qw = MOBA_QBLOCKS * blk
    nq = seq // qw
    cast_in, cast_out, cast_shapes = [], [], []
    for w, lc, C in casts:
        R = w.shape[1]
        assert R % (batch * H) == 0 and C % nq == 0 and (C // nq) % LANES == 0
        cblk = (None, R // (batch * H), C // nq)
        cast_in.append(pl.BlockSpec(cblk, lambda b, h, i, lc=lc: (lc, b * H + h, i)))
        cast_out.append(pl.BlockSpec(cblk, lambda b, h, i: (0, b * H + h, i)))
        cast_shapes.append(jax.ShapeDtypeStruct((1, R, C), BF16))
    body = functools.partial(_moba_body, nb=nb, n_cast=len(casts))
    outs = pl.pallas_call(
        body,
        grid=(batch, H, nq),
        in_specs=[
            pl.BlockSpec((seq, ATTN_HD), lambda b, h, i: (b, h)),
            pl.BlockSpec((seq, ATTN_HD), lambda b, h, i: (b, H + h)),
            pl.BlockSpec((seq, ATTN_HD), lambda b, h, i: (b, 2 * H + h)),
            pl.BlockSpec((1, ATTN_HD), lambda b, h, i: (0, 0)),
            pl.BlockSpec((1, ATTN_HD), lambda b, h, i: (0, 0)),
            pl.BlockSpec((None, 1, LANES), lambda b, h, i: (h, 0, 0)),
        ] + cast_in,
        out_specs=[pl.BlockSpec((qw, ATTN_HD), lambda b, h, i: (b * nq + i, h))] + cast_out,
        out_shape=[jax.ShapeDtypeStruct((T, H * ATTN_HD), BF16)] + cast_shapes,
        scratch_shapes=[
            pltpu.VMEM((seq, ATTN_HD + MOBA_EXT), BF16),
            pltpu.VMEM((ATTN_HD, seq), BF16),
            pltpu.VMEM((nb, ATTN_HD), F32),
            pltpu.VMEM((ATTN_HD + MOBA_EXT, seq), BF16),
        ],
        compiler_params=_cparams(("arbitrary", "arbitrary", "arbitrary")),
        name="moba",
    )(p_main, p_main, p_main, q_norm, k_norm, slopes, *[w for w, _, _ in casts])
    return outs[0], outs[1:]


def _softplus(x):
    return jnp.maximum(x, 0.0) + jnp.log1p(jnp.exp(-jnp.abs(x)))


def _split_hi_lo(x):
    hi = x.astype(BF16)
    lo = (x - hi.astype(F32)).astype(BF16)
    return jnp.concatenate([hi, lo], axis=1)


def _ssd_dt_body(dt_ref, dtb_ref, alog_ref, dts_ref, acss_ref, acst_ref, *, rows):
    L = SSD_CHUNK
    li = lax.broadcasted_iota(jnp.int32, (L, L), 0)
    si = lax.broadcasted_iota(jnp.int32, (L, L), 1)
    tril = (li >= si).astype(F32)
    a2_row = -jnp.exp(alog_ref[...]) * LOG2E
    for c in range(rows // L):
        rs = slice(c * L, (c + 1) * L)
        dtv = _softplus(dt_ref[rs, :] + dtb_ref[...])
        acs = jnp.dot(tril, dtv * a2_row, preferred_element_type=F32,
                      precision=lax.Precision.HIGHEST)
        dts_ref[rs, :] = _split_hi_lo(dtv)
        acss_ref[rs, :] = _split_hi_lo(acs)
        acst_ref[c] = acs.T


def _ssd_dt(dt_raw, dt_bias_p, a_log_p, *, rows=1024):
    T = dt_raw.shape[0]
    L = SSD_CHUNK
    body = functools.partial(_ssd_dt_body, rows=rows)
    return pl.pallas_call(
        body,
        grid=(T // rows,),
        in_specs=[
            pl.BlockSpec((rows, LANES), lambda i: (i, 0)),
            pl.BlockSpec((1, LANES), lambda i: (0, 0)),
            pl.BlockSpec((1, LANES), lambda i: (0, 0)),
        ],
        out_specs=[
            pl.BlockSpec((rows, 2 * LANES), lambda i: (i, 0)),
            pl.BlockSpec((rows, 2 * LANES), lambda i: (i, 0)),
            pl.BlockSpec((rows // L, LANES, L), lambda i: (i, 0, 0)),
        ],
        out_shape=[jax.ShapeDtypeStruct((T, 2 * LANES), BF16), jax.ShapeDtypeStruct((T, 2 * LANES), BF16),
                   jax.ShapeDtypeStruct((T // L, LANES, L), F32)],
        compiler_params=_cparams(("parallel",)),
        name="ssd_dt",
    )(dt_raw, dt_bias_p, a_log_p)


def _ssd_body(xs_ref, bm_ref, cm_ref, xs_h_ref, bm_h_ref, cm_h_ref, z_ref, dts_ref, acss_ref, acst_ref,
              wx_ref, wb_ref, wc_ref, bx_ref, bb_ref, bc_ref, dsk_ref, nw_ref, *rest, rows, with_cast):
    if with_cast:
        fa_ref, fb_ref, o_ref, fa_o, fb_o, ux_ref, ub_ref, uc_ref, st_ref = rest
        fa_o[...] = fa_ref[...].astype(BF16)
        fb_o[...] = fb_ref[...].astype(BF16)
    else:
        o_ref, ux_ref, ub_ref, uc_ref, st_ref = rest
    g = pl.program_id(1)
    r = pl.program_id(2)
    L = SSD_CHUNK
    P = SSD_HD
    W = SSD_HPG * P
    HALO = SUBLANES

    @pl.when(r == 0)
    def _():
        st_ref[...] = jnp.zeros_like(st_ref)

    first = r == 0

    def conv(u_ref, halo_ref, buf_ref, w_ref, b_ref):
        halo = halo_ref[...]
        buf_ref[0:HALO, :] = jnp.where(first, jnp.zeros_like(halo), halo)
        buf_ref[HALO:, :] = u_ref[...]
        acc = b_ref[...] + w_ref[SSD_CONV - 1:SSD_CONV, :] * u_ref[...]
        for k in range(SSD_CONV - 1):
            shift = SSD_CONV - 1 - k
            acc = acc + w_ref[k:k + 1, :] * buf_ref[HALO - shift:HALO - shift + rows, :]
        buf_ref[HALO:, :] = acc * jax.nn.sigmoid(acc)

    conv(xs_ref, xs_h_ref, ux_ref, wx_ref, bx_ref)
    conv(bm_ref, bm_h_ref, ub_ref, wb_ref, bb_ref)
    conv(cm_ref, cm_h_ref, uc_ref, wc_ref, bc_ref)

    k_i = lax.broadcasted_iota(jnp.int32, (2 * LANES, W), 0)
    c_i = lax.broadcasted_iota(jnp.int32, (2 * LANES, W), 1)
    head = g * SSD_HPG + lax.shift_right_logical(c_i, P.bit_length() - 1)
    expand = ((k_i == head) | (k_i == head + LANES)).astype(BF16)

    li = lax.broadcasted_iota(jnp.int32, (L, L), 0)
    si = lax.broadcasted_iota(jnp.int32, (L, L), 1)
    causal = li >= si
    lane = lax.broadcasted_iota(jnp.int32, (L, LANES), 1)

    for c in range(rows // L):
        r0 = c * L
        rs = slice(HALO + r0, HALO + r0 + L)
        dt_x = _dot(dts_ref[r0:r0 + L, :], expand)
        acs_x = _dot(acss_ref[r0:r0 + L, :], expand)
        acs_last = acs_x[L - 1:L, :]

        xs = ux_ref[rs, :]
        bmat = ub_ref[rs, :]
        cmat = uc_ref[rs, :]
        x = xs * dt_x
        xb = x.astype(BF16)
        cb16 = cmat.astype(BF16)
        cb = jnp.where(causal, _dot_nt(cb16, bmat.astype(BF16)), 0.0)

        tiles = []
        for m in range(W // LANES):
            res = []
            for jj in range(2):
                j = 2 * m + jj
                col = acs_x[:, j * P:j * P + 1]
                row = acst_ref[c, pl.ds(g * SSD_HPG + j, 1), :]
                dec = jnp.exp2(jnp.minimum(col - row, 0.0))
                sc = (cb * dec).astype(BF16)
                res.append(_dot(sc, xb[:, m * LANES:(m + 1) * LANES]))
            tiles.append(jnp.where(lane < P, res[0], res[1]))
        y = jnp.concatenate(tiles, axis=1)

        st = st_ref[...]
        y = y + _dot(cb16, st.astype(BF16)) * jnp.exp2(acs_x)
        xd = (x * jnp.exp2(acs_last - acs_x)).astype(BF16)
        st_ref[...] = st * jnp.exp2(acs_last) + _dot(bmat.T.astype(BF16), xd)

        y = y + dsk_ref[...] * xs
        z = z_ref[r0:r0 + L, :]
        gt = y * (z * jax.nn.sigmoid(z))
        gn = gt * lax.rsqrt(jnp.mean(gt * gt, axis=-1, keepdims=True) + EPS)
        o_ref[r0:r0 + L, :] = (gn * nw_ref[...]).astype(o_ref.dtype)


def _ssd(p_main, dts, acss, acst, conv_w, conv_b, d_skip_x, ssd_norm, cast=None, *, batch, seq, rows=2048):
    T = p_main.shape[0]
    G = SSD_GROUPS
    N = SSD_STATE
    W = SSD_HPG * SSD_HD
    nr = seq // rows
    hb = rows // SUBLANES
    z0, x0 = 3 * ATTN_HEADS * ATTN_HD // W, (3 * ATTN_HEADS * ATTN_HD + G * W) // W
    b0 = (3 * ATTN_HEADS * ATTN_HD + 2 * G * W) // N
    c0 = b0 + G
    cwb0, cwc0 = G * W // N, G * W // N + G

    def rowblk(b, g, r):
        return b * nr + r

    def halo(b, g, r):
        return jnp.maximum((b * nr + r) * hb - 1, 0)

    cast_in, cast_specs, cast_shapes = [], [], []
    if cast is not None:
        cast_a, cast_b, lc = cast
        (_, Ra, Ca), (_, Rb, Cb) = cast_a.shape, cast_b.shape
        n_bg = batch * G
        assert Ra % n_bg == 0 and Ca % nr == 0 and Rb % nr == 0 and Cb % n_bg == 0
        blk_a, blk_b = (Ra // n_bg, Ca // nr), (Rb // nr, Cb // n_bg)
        cast_in = [pl.BlockSpec((None,) + blk_a, lambda b, g, r: (lc, b * G + g, r)),
                   pl.BlockSpec((None,) + blk_b, lambda b, g, r: (lc, r, b * G + g))]
        cast_specs = [pl.BlockSpec((None,) + blk_a, lambda b, g, r: (0, b * G + g, r)),
                      pl.BlockSpec((None,) + blk_b, lambda b, g, r: (0, r, b * G + g))]
        cast_shapes = [jax.ShapeDtypeStruct((1, Ra, Ca), BF16), jax.ShapeDtypeStruct((1, Rb, Cb), BF16)]

    body = functools.partial(_ssd_body, rows=rows, with_cast=cast is not None)
    outs = pl.pallas_call(
        body,
        grid=(batch, G, nr),
        in_specs=[
            pl.BlockSpec((rows, W), lambda b, g, r: (rowblk(b, g, r), x0 + g)),
            pl.BlockSpec((rows, N), lambda b, g, r: (rowblk(b, g, r), b0 + g)),
            pl.BlockSpec((rows, N), lambda b, g, r: (rowblk(b, g, r), c0 + g)),
            pl.BlockSpec((SUBLANES, W), lambda b, g, r: (halo(b, g, r), x0 + g)),
            pl.BlockSpec((SUBLANES, N), lambda b, g, r: (halo(b, g, r), b0 + g)),
            pl.BlockSpec((SUBLANES, N), lambda b, g, r: (halo(b, g, r), c0 + g)),
            pl.BlockSpec((rows, W), lambda b, g, r: (rowblk(b, g, r), z0 + g)),
            pl.BlockSpec((rows, 2 * LANES), lambda b, g, r: (rowblk(b, g, r), 0)),
            pl.BlockSpec((rows, 2 * LANES), lambda b, g, r: (rowblk(b, g, r), 0)),
            pl.BlockSpec((rows // SSD_CHUNK, LANES, SSD_CHUNK), lambda b, g, r: (rowblk(b, g, r), 0, 0)),
            pl.BlockSpec((SSD_CONV, W), lambda b, g, r: (0, g)),
            pl.BlockSpec((SSD_CONV, N), lambda b, g, r: (0, cwb0 + g)),
            pl.BlockSpec((SSD_CONV, N), lambda b, g, r: (0, cwc0 + g)),
            pl.BlockSpec((1, W), lambda b, g, r: (0, g)),
            pl.BlockSpec((1, N), lambda b, g, r: (0, cwb0 + g)),
            pl.BlockSpec((1, N), lambda b, g, r: (0, cwc0 + g)),
            pl.BlockSpec((1, W), lambda b, g, r: (0, g)),
            pl.BlockSpec((1, W), lambda b, g, r: (0, g)),
        ] + cast_in,
        out_specs=[pl.BlockSpec((rows, W), lambda b, g, r: (rowblk(b, g, r), g))] + cast_specs,
        out_shape=[jax.ShapeDtypeStruct((T, G * W), BF16)] + cast_shapes,
        scratch_shapes=[
            pltpu.VMEM((SUBLANES + rows, W), F32),
            pltpu.VMEM((SUBLANES + rows, N), F32),
            pltpu.VMEM((SUBLANES + rows, N), F32),
            pltpu.VMEM((N, W), F32),
        ],
        compiler_params=_cparams(("parallel", "parallel", "arbitrary")),
        name="ssd",
    )(p_main, p_main, p_main, p_main, p_main, p_main, p_main, dts, acss, acst,
      conv_w, conv_w, conv_w, conv_b, conv_b, conv_b, d_skip_x, ssd_norm,
      *([cast[0], cast[1]] if cast is not None else []))
    return outs[0], outs[1:]


def _outproj_body(x_ref, ya_ref, ys_ref, w_ref, o_ref):
    ka = ya_ref.shape[1]
    o_ref[...] = x_ref[...] + _dot(ya_ref[...], w_ref[0:ka, :]) + _dot(ys_ref[...], w_ref[ka:, :])


def _out_proj(x, ya, ys, w_out, l, *, tm=512, tn=1024):
    T, D = x.shape
    Ka, Ks = ya.shape[1], ys.shape[1]
    return pl.pallas_call(
        _outproj_body,
        grid=(D // tn, T // tm),
        in_specs=[
            pl.BlockSpec((tm, tn), lambda j, i: (i, j)),
            pl.BlockSpec((tm, Ka), lambda j, i: (i, 0)),
            pl.BlockSpec((tm, Ks), lambda j, i: (i, 0)),
            pl.BlockSpec((None, Ka + Ks, tn), lambda j, i: (l, 0, j)),
        ],
        out_specs=pl.BlockSpec((tm, tn), lambda j, i: (i, j)),
        out_shape=jax.ShapeDtypeStruct((T, D), F32),
        compiler_params=_cparams(("parallel", "parallel")),
        name="out_proj",
    )(x, ya, ys, w_out)


def _memkv_body(m_ref, nw_ref, wk_ref, wv_ref, kn_ref, k_ref, v_ref):
    h = _rms(m_ref[...], nw_ref[...]).astype(BF16)
    k = _dot(h, wk_ref[...].astype(BF16))
    v_ref[...] = _dot(h, wv_ref[...].astype(BF16)).astype(BF16)
    for hh in range(MEM_HEADS):
        sl = slice(hh * MEM_HD, (hh + 1) * MEM_HD)
        k_ref[:, sl] = _rms(k[:, sl], kn_ref[...]).astype(BF16)


def _mem_kv(mem, nw, wk, wv, kn, l):
    M, D = mem.shape
    Wm = wk.shape[2]
    full = lambda shape: pl.BlockSpec(shape, lambda i: (0,) * len(shape))
    layer = lambda shape: pl.BlockSpec((None,) + shape, lambda i: (l,) + (0,) * len(shape))
    return pl.pallas_call(
        _memkv_body,
        grid=(1,),
        in_specs=[full((M, D)), full((1, D)), layer((D, Wm)), layer((D, Wm)), full((1, MEM_HD))],
        out_specs=[full((M, Wm)), full((M, Wm))],
        out_shape=[jax.ShapeDtypeStruct((M, Wm), BF16)] * 2,
        compiler_params=_cparams(("arbitrary",)),
        name="mem_kv",
    )(mem, nw, wk, wv, kn)


def _memattn_body(x_ref, nw_ref, wq_ref, qn_ref, k_ref, v_ref, wo_ref, o_ref):
    x = x_ref[...]
    h = _rms(x, nw_ref[...]).astype(BF16)
    q = _dot(h, wq_ref[...].astype(BF16))
    scale = MEM_HD ** -0.5
    outs = []
    for hh in range(MEM_HEADS):
        sl = slice(hh * MEM_HD, (hh + 1) * MEM_HD)
        qh = _rms(q[:, sl], qn_ref[...]).astype(BF16)
        s = _dot_nt(qh, k_ref[:, sl]) * scale
        e = jnp.exp(s - jnp.max(s, axis=-1, keepdims=True))
        p = e / jnp.sum(e, axis=-1, keepdims=True)
        outs.append(_dot(p.astype(BF16), v_ref[:, sl]))
    o = jnp.concatenate(outs, axis=1).astype(BF16)
    o_ref[...] = x + _dot(o, wo_ref[...].astype(BF16))


def _mem_attn(x, nw, wq, qn, k, v, wo, l, *, seq, mem_len, tm=512):
    T, D = x.shape
    Wm = wq.shape[2]
    per_b = seq // tm
    return pl.pallas_call(
        _memattn_body,
        grid=(T // tm,),
        in_specs=[
            pl.BlockSpec((tm, D), lambda i: (i, 0)),
            pl.BlockSpec((1, D), lambda i: (0, 0)),
            pl.BlockSpec((None, D, Wm), lambda i: (l, 0, 0)),
            pl.BlockSpec((1, MEM_HD), lambda i: (0, 0)),
            pl.BlockSpec((mem_len, Wm), lambda i: (i // per_b, 0)),
            pl.BlockSpec((mem_len, Wm), lambda i: (i // per_b, 0)),
            pl.BlockSpec((None, Wm, D), lambda i: (l, 0, 0)),
        ],
        out_specs=pl.BlockSpec((tm, D), lambda i: (i, 0)),
        out_shape=jax.ShapeDtypeStruct((T, D), F32),
        compiler_params=_cparams(("parallel",)),
        name="mem_attn",
    )(x, nw, wq, qn, k, v, wo)


def _pad_lanes(v):
    return jnp.pad(v.astype(F32), (0, LANES - v.shape[0])).reshape(1, LANES)


def kernel(x, mem, ff1_norm, ff1_w_gu, ff1_w_down, mix_norm, w_in, q_norm, k_norm, conv_w, conv_b,
           dt_bias, a_log, d_skip, ssd_norm, w_out, xmem_norm, mem_norm, mem_wq, mem_wk, mem_wv,
           mem_q_norm, mem_k_norm, mem_wo, ff2_norm, ff2_w_gu, ff2_w_down):
    B, S, D = x.shape
    M = mem.shape[1]
    depth = w_in.shape[0]
    n_dt = SSD_GROUPS * SSD_HPG
    row = lambda v: v.reshape(1, -1)

    slopes = jnp.exp2(-8.0 * jnp.arange(1, ATTN_HEADS + 1, dtype=F32) / ATTN_HEADS)
    slopes = jnp.broadcast_to(slopes[:, None, None], (ATTN_HEADS, 1, LANES))
    bf = _cast_bf16
    ff1_bf = (bf(ff1_w_gu, layer=0), bf(ff1_w_down, layer=0))
    n_main = w_in.shape[2] - n_dt
    w_main = w_in.astype(BF16)
    w_dt = jnp.pad(w_in[:, :, n_main:], ((0, 0), (0, 0), (0, LANES - n_dt))).astype(BF16)

    xf = x.reshape(B * S, D)
    memf = mem.reshape(B * M, D)
    for l in range(depth):
        xf = _ffn(xf, row(ff1_norm[l]), ff1_bf[0], ff1_bf[1], 0)

        p_main, dt_raw, w2_gu, w2_down = _in_proj(xf, row(mix_norm[l]), w_main, w_dt, l, n_main,
                                                  ff2_w_gu, ff2_w_down, l)
        y_attn, (w_out_l,) = _moba(p_main, row(q_norm[l]), row(k_norm[l]), slopes,
                                   [(w_out, l, w_out.shape[2])], batch=B, seq=S)
        dts, acss, acst = _ssd_dt(dt_raw, _pad_lanes(dt_bias[l]), _pad_lanes(a_log[l]))
        nxt = (ff1_w_gu, ff1_w_down, l + 1) if l + 1 < depth else None
        y_ssd, ff1_next = _ssd(p_main, dts, acss, acst, conv_w[l], row(conv_b[l]),
                               row(jnp.repeat(d_skip[l], SSD_HD)), row(ssd_norm[l]), nxt, batch=B, seq=S)
        ff1_bf = ff1_next or ff1_bf
        xf = _out_proj(xf, y_attn, y_ssd, w_out_l, 0)

        k_mem, v_mem = _mem_kv(memf, row(mem_norm[l]), mem_wk, mem_wv, row(mem_k_norm[l]), l)
        xf = _mem_attn(xf, row(xmem_norm[l]), mem_wq, row(mem_q_norm[l]), k_mem, v_mem, mem_wo, l,
                       seq=S, mem_len=M)

        xf = _ffn(xf, row(ff2_norm[l]), w2_gu, w2_down, 0)
    return xf.reshape(B, S, D)
```

```python
import functools

import jax
import jax.numpy as jnp
from jax import lax
from jax.experimental import pallas as pl
from jax.experimental.pallas import tpu as pltpu

F32 = jnp.float32
BF16 = jnp.bfloat16
EPS = 1e-6
NEG = -1e30
LOG2E = 1.4426950408889634

LANES = 128
SUBLANES = 8
VMEM_LIMIT = 56 * 1024 * 1024

ATTN_HEADS = 8
ATTN_HD = 128
MOBA_BLOCK = 256
MOBA_TOPK = 3
SSD_HD = 64
SSD_GROUPS = 8
SSD_HPG = 6
SSD_STATE = 128
SSD_CONV = 4
SSD_CHUNK = 128
MEM_HEADS = 4
MEM_HD = 128


def _cparams(sem):
    return pltpu.CompilerParams(dimension_semantics=sem, vmem_limit_bytes=VMEM_LIMIT)


def _rms(x, w):
    return x * lax.rsqrt(jnp.mean(x * x, axis=-1, keepdims=True) + EPS) * w


def _dot(a, b):
    return jnp.dot(a, b, preferred_element_type=F32)


def _dot_nt(a, b, precision=None):
    return lax.dot_general(a, b, (((1,), (1,)), ((), ())), preferred_element_type=F32,
                           precision=precision)


CAST_BLOCK_BYTES = 12 * 1024 * 1024


def _cast_body(w_ref, o_ref):
    o_ref[...] = w_ref[...].astype(BF16)


def _cast_bf16(w, layer=None):
    nl, R, C = w.shape
    first = 0
    if layer is not None:
        nl, first = 1, layer
    tr = R
    while tr * C * 4 > CAST_BLOCK_BYTES and tr % 32 == 0:
        tr //= 2
    return pl.pallas_call(
        _cast_body,
        grid=(nl, R // tr),
        in_specs=[pl.BlockSpec((None, tr, C), lambda l, i: (first + l, i, 0))],
        out_specs=pl.BlockSpec((None, tr, C), lambda l, i: (l, i, 0)),
        out_shape=jax.ShapeDtypeStruct((nl, R, C), BF16),
        compiler_params=_cparams(("parallel", "parallel")),
        name="cast_bf16",
    )(w)


def _ffn_body(x_ref, nw_ref, wg_ref, wu_ref, wd_ref, o_ref, h_ref):
    j = pl.program_id(1)

    @pl.when(j == 0)
    def _():
        x = x_ref[...]
        h_ref[...] = _rms(x, nw_ref[...]).astype(BF16)
        o_ref[...] = x

    h = h_ref[...]
    g = _dot(h, wg_ref[...])
    u = _dot(h, wu_ref[...])
    a = (0.5 * (g * jax.nn.sigmoid(g)) * u).astype(BF16)
    o_ref[...] += _dot(a, wd_ref[...])


def _ffn(x, nw, w_gu, w_down, l, *, tm=1024, tf=512):
    T, D = x.shape
    F = w_down.shape[1]
    nf = F // tf
    return pl.pallas_call(
        _ffn_body,
        grid=(T // tm, nf),
        in_specs=[
            pl.BlockSpec((tm, D), lambda i, j: (i, 0)),
            pl.BlockSpec((1, D), lambda i, j: (0, 0)),
            pl.BlockSpec((None, D, tf), lambda i, j: (l, 0, j)),
            pl.BlockSpec((None, D, tf), lambda i, j: (l, 0, j + nf)),
            pl.BlockSpec((None, tf, D), lambda i, j: (l, j, 0)),
        ],
        out_specs=pl.BlockSpec((tm, D), lambda i, j: (i, 0)),
        out_shape=jax.ShapeDtypeStruct((T, D), F32),
        scratch_shapes=[pltpu.VMEM((tm, D), BF16)],
        compiler_params=_cparams(("parallel", "arbitrary")),
        name="ffn",
    )(x, nw, w_gu, w_gu, w_down)


def _inproj_body(x_ref, nw_ref, w_ref, wdt_ref, fa_ref, fb_ref, p_ref, dt_ref, fa_o, fb_o, h_ref):
    j = pl.program_id(1)

    @pl.when(j == 0)
    def _():
        h = _rms(x_ref[...], nw_ref[...]).astype(BF16)
        h_ref[...] = h
        dt_ref[...] = _dot(h, wdt_ref[...])

    p_ref[...] = _dot(h_ref[...], w_ref[...])
    fa_o[...] = fa_ref[...].astype(BF16)
    fb_o[...] = fb_ref[...].astype(BF16)


def _in_proj(x, nw, w_main, w_dt, l, N, cast_a, cast_b, lc, *, tm=1024, tn=1024):
    T, D = x.shape
    assert N % tn == 0 and tn % LANES == 0 and N <= w_main.shape[2]
    ni, nj = T // tm, N // tn
    (_, Ra, Ca), (_, Rb, Cb) = cast_a.shape, cast_b.shape
    assert Ra % ni == 0 and Ca % nj == 0 and Rb % nj == 0 and Cb % ni == 0
    blk_a, blk_b = (Ra // ni, Ca // nj), (Rb // nj, Cb // ni)
    return pl.pallas_call(
        _inproj_body,
        grid=(ni, nj),
        in_specs=[
            pl.BlockSpec((tm, D), lambda i, j: (i, 0)),
            pl.BlockSpec((1, D), lambda i, j: (0, 0)),
            pl.BlockSpec((None, D, tn), lambda i, j: (l, 0, j)),
            pl.BlockSpec((None, D, LANES), lambda i, j: (l, 0, 0)),
            pl.BlockSpec((None,) + blk_a, lambda i, j: (lc, i, j)),
            pl.BlockSpec((None,) + blk_b, lambda i, j: (lc, j, i)),
        ],
        out_specs=[
            pl.BlockSpec((tm, tn), lambda i, j: (i, j)),
            pl.BlockSpec((tm, LANES), lambda i, j: (i, 0)),
            pl.BlockSpec((None,) + blk_a, lambda i, j: (0, i, j)),
            pl.BlockSpec((None,) + blk_b, lambda i, j: (0, j, i)),
        ],
        out_shape=[jax.ShapeDtypeStruct((T, N), F32), jax.ShapeDtypeStruct((T, LANES), F32),
                   jax.ShapeDtypeStruct((1, Ra, Ca), BF16), jax.ShapeDtypeStruct((1, Rb, Cb), BF16)],
        scratch_shapes=[pltpu.VMEM((tm, D), BF16)],
        compiler_params=_cparams(("parallel", "arbitrary")),
        name="in_proj",
    )(x, nw, w_main, w_dt, cast_a, cast_b)


MOBA_EXT = LANES
MOBA_SEL0 = 16
MOBA_VARIANT = 2
MOBA_QBLOCKS = 2


def _split3(x):
    hi = x.astype(BF16).astype(F32)
    r = x - hi
    mid = r.astype(BF16).astype(F32)
    lo = (r - mid).astype(BF16).astype(F32)
    return hi, mid, lo


def _rows16(rows, width):
    r_i = lax.broadcasted_iota(jnp.int32, (2 * SUBLANES, width), 0)
    out = jnp.zeros((2 * SUBLANES, width), F32)
    for k, row in enumerate(rows):
        out = jnp.where(r_i == k, row, out)
    return out


def _moba_body(q_ref, k_ref, v_ref, qn_ref, kn_ref, slope_ref, *rest, nb, n_cast):
    cast_in, rest = rest[:n_cast], rest[n_cast:]
    o_ref, cast_out, rest = rest[0], rest[1:1 + n_cast], rest[1 + n_cast:]
    for src, dst in zip(cast_in, cast_out):
        dst[...] = src[...].astype(BF16)
    _moba_attend(q_ref, k_ref, v_ref, qn_ref, kn_ref, slope_ref, o_ref, *rest, nb=nb)


def _moba_attend(q_ref, k_ref, v_ref, qn_ref, kn_ref, slope_ref, o_ref,
                 kx_ref, vt_ref, kmean_ref, qx_ref, *, nb):
    qi = pl.program_id(2)
    blk = MOBA_BLOCK
    seq = nb * blk
    scale = ATTN_HD ** -0.5
    slope = slope_ref[0:1, 0:1]

    @pl.when((pl.program_id(0) == 0) & (pl.program_id(1) == 0) & (qi == 0))
    def _():
        pos = lax.broadcasted_iota(jnp.int32, (seq, MOBA_EXT), 0)
        c = lax.broadcasted_iota(jnp.int32, (seq, MOBA_EXT), 1)
        n_of = lax.shift_right_logical(pos, blk.bit_length() - 1)
        j_of = pos & (blk - 1)
        ext = jnp.where(c < 3, 1, jnp.where(c < 6, n_of, jnp.where(c < 9, j_of, 0)))
        ext = jnp.where(c - MOBA_SEL0 == n_of, 1, ext)
        kx_ref[:, ATTN_HD:] = ext.astype(F32).astype(BF16)

    @pl.when(qi == 0)
    def _():
        kf = _rms(k_ref[...], kn_ref[...])
        kx_ref[:, 0:ATTN_HD] = kf.astype(BF16)
        for n in range(nb):
            kmean_ref[n:n + 1, :] = jnp.mean(kf[n * blk:(n + 1) * blk], axis=0, keepdims=True)
        vt_ref[...] = v_ref[...].T.astype(BF16)

        qn = _rms(q_ref[...], qn_ref[...])
        gate = _dot_nt(kmean_ref[...], qn, precision=lax.Precision.HIGHEST)
        n_iota = lax.broadcasted_iota(jnp.int32, gate.shape, 0).astype(F32)
        t_i = lax.broadcasted_iota(jnp.int32, (1, seq), 1)
        past_blk = n_iota < lax.shift_right_logical(t_i, blk.bit_length() - 1).astype(F32)
        gate = jnp.where(past_blk, gate, -jnp.inf)
        selb = jnp.full(gate.shape, NEG, F32)
        for _ in range(MOBA_TOPK):
            best = jnp.max(gate, axis=0, keepdims=True)
            first = jnp.min(jnp.where(gate == best, n_iota, float(nb)), axis=0, keepdims=True)
            pick = n_iota == first
            selb = jnp.where(pick, 0.0, selb)
            gate = jnp.where(pick, -jnp.inf, gate)
        selb = jnp.where(past_blk, selb, NEG)
        slope2 = slope * LOG2E
        ones = jnp.ones((1, seq), F32)
        bias_rows = (list(_split3(-slope2 * t_i.astype(F32))) + [v * ones for v in _split3(slope2 * float(blk))]
                     + [v * ones for v in _split3(slope2)])
        qx_ref[0:ATTN_HD, :] = (qn * (scale * LOG2E)).T.astype(BF16)
        qx_ref[ATTN_HD:ATTN_HD + MOBA_SEL0, :] = _rows16(bias_rows, seq).astype(BF16)
        qx_ref[ATTN_HD + MOBA_SEL0:ATTN_HD + MOBA_SEL0 + nb, :] = selb.astype(BF16)
        qx_ref[ATTN_HD + MOBA_SEL0 + nb:, :] = jnp.zeros((MOBA_EXT - MOBA_SEL0 - nb, seq), BF16)

    qw = MOBA_QBLOCKS * blk
    qcols = pl.ds(pl.multiple_of(qi * qw, qw), qw)
    qx_past = qx_ref[:, qcols]
    qx_own = jnp.concatenate([qx_ref[0:ATTN_HD + MOBA_SEL0, qcols],
                              jnp.zeros((MOBA_EXT - MOBA_SEL0, qw), BF16)], axis=0)

    key_i = lax.broadcasted_iota(jnp.int32, (blk, blk), 0)
    qry_i = lax.broadcasted_iota(jnp.int32, (blk, blk), 1)
    lanes = lambda a, u: a[:, u * blk:(u + 1) * blk]
    row_cat = lambda parts: jnp.concatenate(parts, axis=1)

    def attend(n_past):
        n = n_past * blk
        owns = [pl.ds(pl.multiple_of((qi * MOBA_QBLOCKS + u) * blk, blk), blk) for u in range(MOBA_QBLOCKS)]
        s_own = [jnp.where(qry_i >= key_i, _dot(kx_ref[owns[u], :], lanes(qx_own, u)), NEG)
                 for u in range(MOBA_QBLOCKS)]
        h = (n_past // 2) * blk
        s_a = _dot(kx_ref[0:h, :], qx_past) if h else None
        s_b = _dot(kx_ref[h:n, :], qx_past)
        parts = []
        if h:
            m_a = jnp.max(s_a, axis=0, keepdims=True)
            p_a = jnp.exp2(s_a - m_a)
            parts.append((m_a, jnp.sum(p_a, axis=0, keepdims=True), _dot(vt_ref[:, 0:h], p_a.astype(BF16))))
        m_b = jnp.maximum(jnp.max(s_b, axis=0, keepdims=True),
                          row_cat([jnp.max(so, axis=0, keepdims=True) for so in s_own]))
        p_b = jnp.exp2(s_b - m_b)
        p_own = [jnp.exp2(s_own[u] - lanes(m_b, u)) for u in range(MOBA_QBLOCKS)]
        l_b = jnp.sum(p_b, axis=0, keepdims=True) + row_cat([jnp.sum(po, axis=0, keepdims=True) for po in p_own])
        acc_b = _dot(vt_ref[:, h:n], p_b.astype(BF16)) + row_cat(
            [_dot(vt_ref[:, owns[u]], p_own[u].astype(BF16)) for u in range(MOBA_QBLOCKS)])
        parts.append((m_b, l_b, acc_b))
        m = functools.reduce(jnp.maximum, [pm for pm, _, _ in parts])
        l = sum(jnp.exp2(pm - m) * pl_ for pm, pl_, _ in parts)
        acc = sum(jnp.exp2(pm - m) * pa for pm, _, pa in parts)
        o_ref[...] = (acc / l).T.astype(o_ref.dtype)

    per = MOBA_VARIANT // MOBA_QBLOCKS
    for v in range(nb // MOBA_VARIANT):
        pl.when((qi >= v * per) & (qi < (v + 1) * per))(
            functools.partial(attend, (v + 1) * MOBA_VARIANT - 1))


def _moba(p_main, q_norm, k_norm, slopes, casts=(), *, batch, seq):
    T = p_main.shape[0]
    blk = MOBA_BLOCK
    nb = seq // blk
    assert seq % blk == 0 and nb % MOBA_VARIANT == 0 and nb <= MOBA_EXT - MOBA_SEL0
    assert MOBA_VARIANT % MOBA_QBLOCKS == 0
    H = ATTN_HEADS
    qw = MOBA_QBLOCKS * blk
    nq = seq // qw
    cast_in, cast_out, cast_shapes = [], [], []
    for w, lc, C in casts:
        R = w.shape[1]
        assert R % (batch * H) == 0 and C % nq == 0 and (C // nq) % LANES == 0
        cblk = (None, R // (batch * H), C // nq)
        cast_in.append(pl.BlockSpec(cblk, lambda b, h, i, lc=lc: (lc, b * H + h, i)))
        cast_out.append(pl.BlockSpec(cblk, lambda b, h, i: (0, b * H + h, i)))
        cast_shapes.append(jax.ShapeDtypeStruct((1, R, C), BF16))
    body = functools.partial(_moba_body, nb=nb, n_cast=len(casts))
    outs = pl.pallas_call(
        body,
        grid=(batch, H, nq),
        in_specs=[
            pl.BlockSpec((seq, ATTN_HD), lambda b, h, i: (b, h)),
            pl.BlockSpec((seq, ATTN_HD), lambda b, h, i: (b, H + h)),
            pl.BlockSpec((seq, ATTN_HD), lambda b, h, i: (b, 2 * H + h)),
            pl.BlockSpec((1, ATTN_HD), lambda b, h, i: (0, 0)),
            pl.BlockSpec((1, ATTN_HD), lambda b, h, i: (0, 0)),
            pl.BlockSpec((None, 1, LANES), lambda b, h, i: (h, 0, 0)),
        ] + cast_in,
        out_specs=[pl.BlockSpec((qw, ATTN_HD), lambda b, h, i: (b * nq + i, h))] + cast_out,
        out_shape=[jax.ShapeDtypeStruct((T, H * ATTN_HD), BF16)] + cast_shapes,
        scratch_shapes=[
            pltpu.VMEM((seq, ATTN_HD + MOBA_EXT), BF16),
            pltpu.VMEM((ATTN_HD, seq), BF16),
            pltpu.VMEM((nb, ATTN_HD), F32),
            pltpu.VMEM((ATTN_HD + MOBA_EXT, seq), BF16),
        ],
        compiler_params=_cparams(("arbitrary", "arbitrary", "arbitrary")),
        name="moba",
    )(p_main, p_main, p_main, q_norm, k_norm, slopes, *[w for w, _, _ in casts])
    return outs[0], outs[1:]


def _softplus(x):
    return jnp.maximum(x, 0.0) + jnp.log1p(jnp.exp(-jnp.abs(x)))


def _split_hi_lo(x):
    hi = x.astype(BF16)
    lo = (x - hi.astype(F32)).astype(BF16)
    return jnp.concatenate([hi, lo], axis=1)


def _ssd_dt_body(dt_ref, dtb_ref, alog_ref, dts_ref, acss_ref, acst_ref, *, rows):
    L = SSD_CHUNK
    li = lax.broadcasted_iota(jnp.int32, (L, L), 0)
    si = lax.broadcasted_iota(jnp.int32, (L, L), 1)
    tril = (li >= si).astype(F32)
    a2_row = -jnp.exp(alog_ref[...]) * LOG2E
    for c in range(rows // L):
        rs = slice(c * L, (c + 1) * L)
        dtv = _softplus(dt_ref[rs, :] + dtb_ref[...])
        acs = jnp.dot(tril, dtv * a2_row, preferred_element_type=F32,
                      precision=lax.Precision.HIGHEST)
        dts_ref[rs, :] = _split_hi_lo(dtv)
        acss_ref[rs, :] = _split_hi_lo(acs)
        acst_ref[c] = acs.T


def _ssd_dt(dt_raw, dt_bias_p, a_log_p, *, rows=1024):
    T = dt_raw.shape[0]
    L = SSD_CHUNK
    body = functools.partial(_ssd_dt_body, rows=rows)
    return pl.pallas_call(
        body,
        grid=(T // rows,),
        in_specs=[
            pl.BlockSpec((rows, LANES), lambda i: (i, 0)),
            pl.BlockSpec((1, LANES), lambda i: (0, 0)),
            pl.BlockSpec((1, LANES), lambda i: (0, 0)),
        ],
        out_specs=[
            pl.BlockSpec((rows, 2 * LANES), lambda i: (i, 0)),
            pl.BlockSpec((rows, 2 * LANES), lambda i: (i, 0)),
            pl.BlockSpec((rows // L, LANES, L), lambda i: (i, 0, 0)),
        ],
        out_shape=[jax.ShapeDtypeStruct((T, 2 * LANES), BF16), jax.ShapeDtypeStruct((T, 2 * LANES), BF16),
                   jax.ShapeDtypeStruct((T // L, LANES, L), F32)],
        compiler_params=_cparams(("parallel",)),
        name="ssd_dt",
    )(dt_raw, dt_bias_p, a_log_p)


def _ssd_body(xs_ref, bm_ref, cm_ref, xs_h_ref, bm_h_ref, cm_h_ref, z_ref, dts_ref, acss_ref, acst_ref,
              wx_ref, wb_ref, wc_ref, bx_ref, bb_ref, bc_ref, dsk_ref, nw_ref, *rest, rows, with_cast):
    if with_cast:
        fa_ref, fb_ref, o_ref, fa_o, fb_o, ux_ref, ub_ref, uc_ref, st_ref = rest
        fa_o[...] = fa_ref[...].astype(BF16)
        fb_o[...] = fb_ref[...].astype(BF16)
    else:
        o_ref, ux_ref, ub_ref, uc_ref, st_ref = rest
    g = pl.program_id(1)
    r = pl.program_id(2)
    L = SSD_CHUNK
    P = SSD_HD
    W = SSD_HPG * P
    HALO = SUBLANES

    @pl.when(r == 0)
    def _():
        st_ref[...] = jnp.zeros_like(st_ref)

    first = r == 0

    def conv(u_ref, halo_ref, buf_ref, w_ref, b_ref):
        halo = halo_ref[...]
        buf_ref[0:HALO, :] = jnp.where(first, jnp.zeros_like(halo), halo)
        buf_ref[HALO:, :] = u_ref[...]
        acc = b_ref[...] + w_ref[SSD_CONV - 1:SSD_CONV, :] * u_ref[...]
        for k in range(SSD_CONV - 1):
            shift = SSD_CONV - 1 - k
            acc = acc + w_ref[k:k + 1, :] * buf_ref[HALO - shift:HALO - shift + rows, :]
        buf_ref[HALO:, :] = acc * jax.nn.sigmoid(acc)

    conv(xs_ref, xs_h_ref, ux_ref, wx_ref, bx_ref)
    conv(bm_ref, bm_h_ref, ub_ref, wb_ref, bb_ref)
    conv(cm_ref, cm_h_ref, uc_ref, wc_ref, bc_ref)

    k_i = lax.broadcasted_iota(jnp.int32, (2 * LANES, W), 0)
    c_i = lax.broadcasted_iota(jnp.int32, (2 * LANES, W), 1)
    head = g * SSD_HPG + lax.shift_right_logical(c_i, P.bit_length() - 1)
    expand = ((k_i == head) | (k_i == head + LANES)).astype(BF16)

    li = lax.broadcasted_iota(jnp.int32, (L, L), 0)
    si = lax.broadcasted_iota(jnp.int32, (L, L), 1)
    causal = li >= si
    lane = lax.broadcasted_iota(jnp.int32, (L, LANES), 1)

    for c in range(rows // L):
        r0 = c * L
        rs = slice(HALO + r0, HALO + r0 + L)
        dt_x = _dot(dts_ref[r0:r0 + L, :], expand)
        acs_x = _dot(acss_ref[r0:r0 + L, :], expand)
        acs_last = acs_x[L - 1:L, :]

        xs = ux_ref[rs, :]
        bmat = ub_ref[rs, :]
        cmat = uc_ref[rs, :]
        x = xs * dt_x
        xb = x.astype(BF16)
        cb16 = cmat.astype(BF16)
        cb = jnp.where(causal, _dot_nt(cb16, bmat.astype(BF16)), 0.0)

        tiles = []
        for m in range(W // LANES):
            res = []
            for jj in range(2):
                j = 2 * m + jj
                col = acs_x[:, j * P:j * P + 1]
                row = acst_ref[c, pl.ds(g * SSD_HPG + j, 1), :]
                dec = jnp.exp2(jnp.minimum(col - row, 0.0))
                sc = (cb * dec).astype(BF16)
                res.append(_dot(sc, xb[:, m * LANES:(m + 1) * LANES]))
            tiles.append(jnp.where(lane < P, res[0], res[1]))
        y = jnp.concatenate(tiles, axis=1)

        st = st_ref[...]
        y = y + _dot(cb16, st.astype(BF16)) * jnp.exp2(acs_x)
        xd = (x * jnp.exp2(acs_last - acs_x)).astype(BF16)
        st_ref[...] = st * jnp.exp2(acs_last) + _dot(bmat.T.astype(BF16), xd)

        y = y + dsk_ref[...] * xs
        z = z_ref[r0:r0 + L, :]
        gt = y * (z * jax.nn.sigmoid(z))
        gn = gt * lax.rsqrt(jnp.mean(gt * gt, axis=-1, keepdims=True) + EPS)
        o_ref[r0:r0 + L, :] = (gn * nw_ref[...]).astype(o_ref.dtype)


def _ssd(p_main, dts, acss, acst, conv_w, conv_b, d_skip_x, ssd_norm, cast=None, *, batch, seq, rows=2048):
    T = p_main.shape[0]
    G = SSD_GROUPS
    N = SSD_STATE
    W = SSD_HPG * SSD_HD
    nr = seq // rows
    hb = rows // SUBLANES
    z0, x0 = 3 * ATTN_HEADS * ATTN_HD // W, (3 * ATTN_HEADS * ATTN_HD + G * W) // W
    b0 = (3 * ATTN_HEADS * ATTN_HD + 2 * G * W) // N
    c0 = b0 + G
    cwb0, cwc0 = G * W // N, G * W // N + G

    def rowblk(b, g, r):
        return b * nr + r

    def halo(b, g, r):
        return jnp.maximum((b * nr + r) * hb - 1, 0)

    cast_in, cast_specs, cast_shapes = [], [], []
    if cast is not None:
        cast_a, cast_b, lc = cast
        (_, Ra, Ca), (_, Rb, Cb) = cast_a.shape, cast_b.shape
        n_bg = batch * G
        assert Ra % n_bg == 0 and Ca % nr == 0 and Rb % nr == 0 and Cb % n_bg == 0
        blk_a, blk_b = (Ra // n_bg, Ca // nr), (Rb // nr, Cb // n_bg)
        cast_in = [pl.BlockSpec((None,) + blk_a, lambda b, g, r: (lc, b * G + g, r)),
                   pl.BlockSpec((None,) + blk_b, lambda b, g, r: (lc, r, b * G + g))]
        cast_specs = [pl.BlockSpec((None,) + blk_a, lambda b, g, r: (0, b * G + g, r)),
                      pl.BlockSpec((None,) + blk_b, lambda b, g, r: (0, r, b * G + g))]
        cast_shapes = [jax.ShapeDtypeStruct((1, Ra, Ca), BF16), jax.ShapeDtypeStruct((1, Rb, Cb), BF16)]

    body = functools.partial(_ssd_body, rows=rows, with_cast=cast is not None)
    outs = pl.pallas_call(
        body,
        grid=(batch, G, nr),
        in_specs=[
            pl.BlockSpec((rows, W), lambda b, g, r: (rowblk(b, g, r), x0 + g)),
            pl.BlockSpec((rows, N), lambda b, g, r: (rowblk(b, g, r), b0 + g)),
            pl.BlockSpec((rows, N), lambda b, g, r: (rowblk(b, g, r), c0 + g)),
            pl.BlockSpec((SUBLANES, W), lambda b, g, r: (halo(b, g, r), x0 + g)),
            pl.BlockSpec((SUBLANES, N), lambda b, g, r: (halo(b, g, r), b0 + g)),
            pl.BlockSpec((SUBLANES, N), lambda b, g, r: (halo(b, g, r), c0 + g)),
            pl.BlockSpec((rows, W), lambda b, g, r: (rowblk(b, g, r), z0 + g)),
            pl.BlockSpec((rows, 2 * LANES), lambda b, g, r: (rowblk(b, g, r), 0)),
            pl.BlockSpec((rows, 2 * LANES), lambda b, g, r: (rowblk(b, g, r), 0)),
            pl.BlockSpec((rows // SSD_CHUNK, LANES, SSD_CHUNK), lambda b, g, r: (rowblk(b, g, r), 0, 0)),
            pl.BlockSpec((SSD_CONV, W), lambda b, g, r: (0, g)),
            pl.BlockSpec((SSD_CONV, N), lambda b, g, r: (0, cwb0 + g)),
            pl.BlockSpec((SSD_CONV, N), lambda b, g, r: (0, cwc0 + g)),
            pl.BlockSpec((1, W), lambda b, g, r: (0, g)),
            pl.BlockSpec((1, N), lambda b, g, r: (0, cwb0 + g)),
            pl.BlockSpec((1, N), lambda b, g, r: (0, cwc0 + g)),
            pl.BlockSpec((1, W), lambda b, g, r: (0, g)),
            pl.BlockSpec((1, W), lambda b, g, r: (0, g)),
        ] + cast_in,
        out_specs=[pl.BlockSpec((rows, W), lambda b, g, r: (rowblk(b, g, r), g))] + cast_specs,
        out_shape=[jax.ShapeDtypeStruct((T, G * W), BF16)] + cast_shapes,
        scratch_shapes=[
            pltpu.VMEM((SUBLANES + rows, W), F32),
            pltpu.VMEM((SUBLANES + rows, N), F32),
            pltpu.VMEM((SUBLANES + rows, N), F32),
            pltpu.VMEM((N, W), F32),
        ],
        compiler_params=_cparams(("parallel", "parallel", "arbitrary")),
        name="ssd",
    )(p_main, p_main, p_main, p_main, p_main, p_main, p_main, dts, acss, acst,
      conv_w, conv_w, conv_w, conv_b, conv_b, conv_b, d_skip_x, ssd_norm,
      *([cast[0], cast[1]] if cast is not None else []))
    return outs[0], outs[1:]


def _outproj_body(x_ref, ya_ref, ys_ref, w_ref, o_ref):
    ka = ya_ref.shape[1]
    o_ref[...] = x_ref[...] + _dot(ya_ref[...], w_ref[0:ka, :]) + _dot(ys_ref[...], w_ref[ka:, :])


def _out_proj(x, ya, ys, w_out, l, *, tm=1024, tn=1024):
    T, D = x.shape
    Ka, Ks = ya.shape[1], ys.shape[1]
    return pl.pallas_call(
        _outproj_body,
        grid=(D // tn, T // tm),
        in_specs=[
            pl.BlockSpec((tm, tn), lambda j, i: (i, j)),
            pl.BlockSpec((tm, Ka), lambda j, i: (i, 0)),
            pl.BlockSpec((tm, Ks), lambda j, i: (i, 0)),
            pl.BlockSpec((None, Ka + Ks, tn), lambda j, i: (l, 0, j), pipeline_mode=pl.Buffered(1)),
        ],
        out_specs=pl.BlockSpec((tm, tn), lambda j, i: (i, j)),
        out_shape=jax.ShapeDtypeStruct((T, D), F32),
        compiler_params=_cparams(("parallel", "parallel")),
        name="out_proj",
    )(x, ya, ys, w_out)


def _memkv_body(m_ref, nw_ref, wk_ref, wv_ref, kn_ref, k_ref, v_ref):
    h = _rms(m_ref[...], nw_ref[...]).astype(BF16)
    k = _dot(h, wk_ref[...].astype(BF16))
    v_ref[...] = _dot(h, wv_ref[...].astype(BF16)).astype(BF16)
    for hh in range(MEM_HEADS):
        sl = slice(hh * MEM_HD, (hh + 1) * MEM_HD)
        k_ref[:, sl] = _rms(k[:, sl], kn_ref[...]).astype(BF16)


def _mem_kv(mem, nw, wk, wv, kn, l):
    M, D = mem.shape
    Wm = wk.shape[2]
    full = lambda shape: pl.BlockSpec(shape, lambda i: (0,) * len(shape))
    layer = lambda shape: pl.BlockSpec((None,) + shape, lambda i: (l,) + (0,) * len(shape))
    return pl.pallas_call(
        _memkv_body,
        grid=(1,),
        in_specs=[full((M, D)), full((1, D)), layer((D, Wm)), layer((D, Wm)), full((1, MEM_HD))],
        out_specs=[full((M, Wm)), full((M, Wm))],
        out_shape=[jax.ShapeDtypeStruct((M, Wm), BF16)] * 2,
        compiler_params=_cparams(("arbitrary",)),
        name="mem_kv",
    )(mem, nw, wk, wv, kn)


def _memattn_body(x_ref, nw_ref, wq_ref, qn_ref, k_ref, v_ref, wo_ref, o_ref):
    x = x_ref[...]
    h = _rms(x, nw_ref[...]).astype(BF16)
    q = _dot(h, wq_ref[...].astype(BF16))
    scale = MEM_HD ** -0.5
    outs = []
    for hh in range(MEM_HEADS):
        sl = slice(hh * MEM_HD, (hh + 1) * MEM_HD)
        qh = _rms(q[:, sl], qn_ref[...]).astype(BF16)
        s = _dot_nt(qh, k_ref[:, sl]) * scale
        e = jnp.exp(s - jnp.max(s, axis=-1, keepdims=True))
        p = e / jnp.sum(e, axis=-1, keepdims=True)
        outs.append(_dot(p.astype(BF16), v_ref[:, sl]))
    o = jnp.concatenate(outs, axis=1).astype(BF16)
    o_ref[...] = x + _dot(o, wo_ref[...].astype(BF16))


def _mem_attn(x, nw, wq, qn, k, v, wo, l, *, seq, mem_len, tm=1024):
    T, D = x.shape
    Wm = wq.shape[2]
    per_b = seq // tm
    return pl.pallas_call(
        _memattn_body,
        grid=(T // tm,),
        in_specs=[
            pl.BlockSpec((tm, D), lambda i: (i, 0)),
            pl.BlockSpec((1, D), lambda i: (0, 0)),
            pl.BlockSpec((None, D, Wm), lambda i: (l, 0, 0)),
            pl.BlockSpec((1, MEM_HD), lambda i: (0, 0)),
            pl.BlockSpec((mem_len, Wm), lambda i: (i // per_b, 0)),
            pl.BlockSpec((mem_len, Wm), lambda i: (i // per_b, 0)),
            pl.BlockSpec((None, Wm, D), lambda i: (l, 0, 0)),
        ],
        out_specs=pl.BlockSpec((tm, D), lambda i: (i, 0)),
        out_shape=jax.ShapeDtypeStruct((T, D), F32),
        compiler_params=_cparams(("parallel",)),
        name="mem_attn",
    )(x, nw, wq, qn, k, v, wo)


def _pad_lanes(v):
    return jnp.pad(v.astype(F32), (0, LANES - v.shape[0])).reshape(1, LANES)


def kernel(x, mem, ff1_norm, ff1_w_gu, ff1_w_down, mix_norm, w_in, q_norm, k_norm, conv_w, conv_b,
           dt_bias, a_log, d_skip, ssd_norm, w_out, xmem_norm, mem_norm, mem_wq, mem_wk, mem_wv,
           mem_q_norm, mem_k_norm, mem_wo, ff2_norm, ff2_w_gu, ff2_w_down):
    B, S, D = x.shape
    M = mem.shape[1]
    depth = w_in.shape[0]
    n_dt = SSD_GROUPS * SSD_HPG
    row = lambda v: v.reshape(1, -1)

    slopes = jnp.exp2(-8.0 * jnp.arange(1, ATTN_HEADS + 1, dtype=F32) / ATTN_HEADS)
    slopes = jnp.broadcast_to(slopes[:, None, None], (ATTN_HEADS, 1, LANES))
    bf = _cast_bf16
    ff1_bf = (bf(ff1_w_gu, layer=0), bf(ff1_w_down, layer=0))
    n_main = w_in.shape[2] - n_dt
    w_main = w_in.astype(BF16)
    w_dt = jnp.pad(w_in[:, :, n_main:], ((0, 0), (0, 0), (0, LANES - n_dt))).astype(BF16)

    xf = x.reshape(B * S, D)
    memf = mem.reshape(B * M, D)
    for l in range(depth):
        xf = _ffn(xf, row(ff1_norm[l]), ff1_bf[0], ff1_bf[1], 0)

        p_main, dt_raw, w2_gu, w2_down = _in_proj(xf, row(mix_norm[l]), w_main, w_dt, l, n_main,
                                                  ff2_w_gu, ff2_w_down, l)
        y_attn, (w_out_l,) = _moba(p_main, row(q_norm[l]), row(k_norm[l]), slopes,
                                   [(w_out, l, w_out.shape[2])], batch=B, seq=S)
        dts, acss, acst = _ssd_dt(dt_raw, _pad_lanes(dt_bias[l]), _pad_lanes(a_log[l]))
        nxt = (ff1_w_gu, ff1_w_down, l + 1) if l + 1 < depth else None
        y_ssd, ff1_next = _ssd(p_main, dts, acss, acst, conv_w[l], row(conv_b[l]),
                               row(jnp.repeat(d_skip[l], SSD_HD)), row(ssd_norm[l]), nxt, batch=B, seq=S)
        ff1_bf = ff1_next or ff1_bf
        xf = _out_proj(xf, y_attn, y_ssd, w_out_l, 0)

        k_mem, v_mem = _mem_kv(memf, row(mem_norm[l]), mem_wk, mem_wv, row(mem_k_norm[l]), l)
        xf = _mem_attn(xf, row(xmem_norm[l]), mem_wq, row(mem_q_norm[l]), k_mem, v_mem, mem_wo, l,
                       seq=S, mem_len=M)

        xf = _ffn(xf, row(ff2_norm[l]), w2_gu, w2_down, 0)
    return xf.reshape(B, S, D)
```
